```python
import jax, jax.numpy as jnp
from jax import lax
import numpy as np

D_MODEL = 1024
BATCH = 32
SEQ = 2048
DEPTH = 2

HG_HEADS = 4
HG_DK = 128
HG_DV = 128
HG_CHUNK = 32
FOX_HEADS = 8
FOX_DH = 64
DIFF_HEADS = 4
DIFF_DH = 64
DIFF_DV = 2 * DIFF_DH
N_DIFF_MAPS = 2
BRANCH_WIDTH = 512
N_BRANCHES = 3
Q_BLOCK = 128
ROPE_THETA = 500000.0
ROPE_DIM = DIFF_DH // 4
N_GROUPS = 4
EXPERTS_PER_GROUP = 8
N_EXPERTS = N_GROUPS * EXPERTS_PER_GROUP
TOP_K_IN_GROUP = 2
D_EXPERT = 512
ALPHA = (2 * DEPTH) ** 0.25
BETA = (8 * DEPTH) ** -0.25
LN_EPS = 1e-5
RMS_EPS = 1e-6
NEG_INF = -1e30
EXP_CLAMP = 60.0
MAX_POS_OFFSET = 4096
IN_SPLITS = (HG_HEADS * HG_DK, HG_HEADS * HG_DK, HG_HEADS * HG_DV, HG_HEADS * HG_DV,
             FOX_HEADS * FOX_DH, FOX_HEADS * FOX_DH, FOX_HEADS * FOX_DH, FOX_HEADS,
             DIFF_HEADS * N_DIFF_MAPS * DIFF_DH, DIFF_HEADS * N_DIFF_MAPS * DIFF_DH, DIFF_HEADS * DIFF_DV,
             N_BRANCHES * D_MODEL)
N_IN = sum(IN_SPLITS)

kernel_name = "hgrn2_fox_diffattn_gated_merge_hmoe_deepnorm"


def layer_norm(x, g, b):
    xf = x.astype(jnp.float32)
    mu = jnp.mean(xf, axis=-1, keepdims=True)
    var = jnp.mean(jnp.square(xf - mu), axis=-1, keepdims=True)
    y = (xf - mu) * lax.rsqrt(var + LN_EPS) * g.astype(jnp.float32) + b.astype(jnp.float32)
    return y.astype(x.dtype)


def rms_norm(x, g):
    xf = x.astype(jnp.float32)
    return xf * lax.rsqrt(jnp.mean(jnp.square(xf), axis=-1, keepdims=True) + RMS_EPS) * g.astype(jnp.float32)


def partial_rope(t, cos, sin):
    half = ROPE_DIM // 2
    c = cos[:, :, None, None, :]
    s = sin[:, :, None, None, :]
    tf = t[..., :ROPE_DIM].astype(jnp.float32)
    t1, t2 = tf[..., :half], tf[..., half:]
    rot = jnp.concatenate([t1 * c - t2 * s, t2 * c + t1 * s], axis=-1).astype(t.dtype)
    return jnp.concatenate([rot, t[..., ROPE_DIM:]], axis=-1)


def causal_block_attention(q, k, v, log_f=None):
    _, S, _, _, dk = q.shape
    scale = dk ** -0.5
    c = None if log_f is None else jnp.swapaxes(jnp.cumsum(log_f.astype(jnp.float32), axis=1), 1, 2)
    outs = []
    for blk in range(S // Q_BLOCK):
        q_lo, q_hi = blk * Q_BLOCK, (blk + 1) * Q_BLOCK
        s = jnp.einsum("bqhmd,bkhmd->bhmqk", q[:, q_lo:q_hi], k[:, :q_hi]).astype(jnp.float32) * scale
        if c is not None:
            s = s + (c[:, :, q_lo:q_hi, None] - c[:, :, None, :q_hi])[:, :, None]
        causal = (q_lo + jnp.arange(Q_BLOCK))[:, None] >= jnp.arange(q_hi)[None, :]
        s = jnp.where(causal, s, NEG_INF)
        p = jax.nn.softmax(s, axis=-1).astype(v.dtype)
        outs.append(jnp.einsum("bhmqk,bkhe->bqhme", p, v[:, :q_hi]))
    return jnp.concatenate(outs, axis=1)


def chunked_gated_recurrence(q, k, v, log_f):
    B, S, H, DK = q.shape
    DV = v.shape[-1]
    n_chunks = S // HG_CHUNK

    def to_chunks(t):
        return t.reshape(B, n_chunks, HG_CHUNK, H, t.shape[-1]).transpose(1, 0, 3, 2, 4)

    causal = jnp.tril(jnp.ones((HG_CHUNK, HG_CHUNK), dtype=bool))

    def step(state, inp):
        qc, kc, vc, gc = inp
        b = jnp.cumsum(gc, axis=2)
        o_inter = jnp.einsum("bhtc,bhcv->bhtv", qc * jnp.exp(b), state)
        rel = jnp.minimum(b[:, :, :, None, :] - b[:, :, None, :, :], 0.0)
        decay = jnp.where(causal[:, :, None], jnp.exp(rel), 0.0)
        scores = jnp.einsum("bhtc,bhsc,bhtsc->bhts", qc, kc, decay)
        o_intra = jnp.einsum("bhts,bhsv->bhtv", scores, vc)
        b_last = b[:, :, -1, :]
        k_dec = kc * jnp.exp(b_last[:, :, None, :] - b)
        state = state * jnp.exp(b_last)[..., None] + jnp.einsum("bhsc,bhsv->bhcv", k_dec, vc)
        return state, o_inter + o_intra

    state0 = jnp.zeros((B, H, DK, DV), jnp.float32)
    _, o = lax.scan(step, state0, (to_chunks(q), to_chunks(k), to_chunks(v), to_chunks(log_f)))
    return o.transpose(1, 0, 3, 2, 4).reshape(B, S, H, DV)


def hgrn2_branch(q_lin, f_lin, i_lin, g_lin, lb, norm_g):
    B, S, _ = q_lin.shape
    z = f_lin.astype(jnp.float32).reshape(B, S, HG_HEADS, HG_DK)
    lbh = lb.astype(jnp.float32).reshape(HG_HEADS, HG_DK)
    log_f = jax.nn.log_sigmoid(z) + jnp.log1p(lbh * jnp.exp(jnp.minimum(-z, EXP_CLAMP)))
    k = (1.0 - lbh) * jax.nn.sigmoid(-z)
    q = jax.nn.silu(q_lin.astype(jnp.float32)).reshape(B, S, HG_HEADS, HG_DK)
    v = i_lin.astype(jnp.float32).reshape(B, S, HG_HEADS, HG_DV)
    o = chunked_gated_recurrence(q, k, v, log_f)
    o = rms_norm(o, norm_g) * jax.nn.silu(g_lin.astype(jnp.float32).reshape(B, S, HG_HEADS, HG_DV))
    return o.reshape(B, S, HG_HEADS * HG_DV).astype(q_lin.dtype)


def fox_branch(q_lin, k_lin, v_lin, f_lin, f_bias):
    B, S, _ = q_lin.shape
    q = q_lin.reshape(B, S, FOX_HEADS, 1, FOX_DH)
    k = k_lin.reshape(B, S, FOX_HEADS, 1, FOX_DH)
    v = v_lin.reshape(B, S, FOX_HEADS, FOX_DH)
    log_f = jax.nn.log_sigmoid(f_lin.astype(jnp.float32) + f_bias.astype(jnp.float32))
    o = causal_block_attention(q, k, v, log_f)
    return o.reshape(B, S, FOX_HEADS * FOX_DH)


def diff_branch(q_lin, k_lin, v_lin, cos, sin, lam_vecs, lam_init, norm_g):
    B, S, _ = q_lin.shape
    q = partial_rope(q_lin.reshape(B, S, DIFF_HEADS, N_DIFF_MAPS, DIFF_DH), cos, sin)
    k = partial_rope(k_lin.reshape(B, S, DIFF_HEADS, N_DIFF_MAPS, DIFF_DH), cos, sin)
    v = v_lin.reshape(B, S, DIFF_HEADS, DIFF_DV)
    o = causal_block_attention(q, k, v).astype(jnp.float32)
    lv = lam_vecs.astype(jnp.float32)
    lam = jnp.exp(jnp.sum(lv[0] * lv[1])) - jnp.exp(jnp.sum(lv[2] * lv[3])) + lam_init
    o = o[:, :, :, 0] - lam * o[:, :, :, 1]
    o = rms_norm(o, norm_g) * (1.0 - lam_init)
    return o.reshape(B, S, DIFF_HEADS * DIFF_DV).astype(q_lin.dtype)


def token_mixer(x, cos, sin, w_in, lb, hg_norm_g, fox_bias, lam_vecs, lam_init, diff_norm_g, w_branch, w_out):
    B, S, D = x.shape
    proj = jnp.einsum("bsd,dn->bsn", x, w_in)
    offsets = [int(o) for o in np.cumsum(IN_SPLITS)[:-1]]
    (hq, hf, hi, hg, fq, fk, fv, ff, dq, dk, dv, gates) = jnp.split(proj, offsets, axis=-1)
    y_a = hgrn2_branch(hq, hf, hi, hg, lb, hg_norm_g)
    y_b = fox_branch(fq, fk, fv, ff, fox_bias)
    y_c = diff_branch(dq, dk, dv, cos, sin, lam_vecs, lam_init, diff_norm_g)
    ys = jnp.stack([y_a, y_b, y_c], axis=2)
    branch = jnp.einsum("bsrw,rwd->bsrd", ys, w_branch)
    g = jax.nn.sigmoid(gates.astype(jnp.float32)).astype(x.dtype).reshape(B, S, N_BRANCHES, D)
    merged = jnp.sum(g * branch, axis=2)
    return jnp.einsum("bsd,de->bse", merged, w_out)


def hier_moe(x, w_rg, b_rg, w_re, b_re, w_gate, w_up, w_down):
    B, S, D = x.shape
    xt = x.reshape(B * S, D)
    g_prob = jax.nn.softmax((xt @ w_rg).astype(jnp.float32) + b_rg.astype(jnp.float32), axis=-1)
    g_p, g_idx = lax.top_k(g_prob, 1)
    e_logits = (xt @ w_re).astype(jnp.float32).reshape(-1, N_GROUPS, EXPERTS_PER_GROUP) + b_re.astype(jnp.float32)
    e_sel = jnp.take_along_axis(e_logits, g_idx[:, :, None], axis=1)[:, 0]
    e_l, e_idx = lax.top_k(e_sel, TOP_K_IN_GROUP)
    w = g_p * jax.nn.softmax(e_l, axis=-1)
    ids = g_idx * EXPERTS_PER_GROUP + e_idx
    gate = jnp.einsum("tk,tke->te", w, jax.nn.one_hot(ids, N_EXPERTS, dtype=jnp.float32)).astype(x.dtype)
    out = jnp.zeros_like(xt)
    for e in range(N_EXPERTS):
        h = jax.nn.silu(xt @ w_gate[e]) * (xt @ w_up[e])
        out = out + gate[:, e:e + 1] * (h @ w_down[e])
    return out.reshape(B, S, D)


def setup_inputs(seed: int = 0) -> dict:
    key = jax.random.key(seed)
    ks = jax.random.split(key, 24)
    f32 = jnp.float32

    def nrm(k, shape, scale):
        return jax.random.normal(k, shape, f32) * scale

    return {
        "x": nrm(ks[0], (BATCH, SEQ, D_MODEL), 1.0),
        "positions": jax.random.randint(ks[1], (BATCH, 1), 0, MAX_POS_OFFSET, dtype=jnp.int32) + jnp.arange(SEQ, dtype=jnp.int32)[None, :],
        "ln_in_g": 1.0 + nrm(ks[2], (D_MODEL,), 0.02),
        "ln_in_b": nrm(ks[3], (D_MODEL,), 0.02),
        "w_in": nrm(ks[4], (DEPTH, D_MODEL, N_IN), D_MODEL ** -0.5),
        "hgrn_lb_logits": nrm(ks[5], (DEPTH, HG_HEADS * HG_DK), 0.1),
        "hgrn_norm_g": 1.0 + nrm(ks[6], (DEPTH, HG_DV), 0.02),
        "fox_f_bias": jax.random.uniform(ks[7], (DEPTH, FOX_HEADS), f32, 2.0, 4.0),
        "diff_lambda": nrm(ks[8], (DEPTH, 4, DIFF_DH), 0.1),
        "diff_norm_g": 1.0 + nrm(ks[9], (DEPTH, DIFF_DV), 0.02),
        "w_branch": nrm(ks[10], (DEPTH, N_BRANCHES, BRANCH_WIDTH, D_MODEL), BRANCH_WIDTH ** -0.5),
        "w_out": nrm(ks[11], (DEPTH, D_MODEL, D_MODEL), D_MODEL ** -0.5 * BETA),
        "ln1_g": 1.0 + nrm(ks[12], (DEPTH, D_MODEL), 0.02),
        "ln1_b": nrm(ks[13], (DEPTH, D_MODEL), 0.02),
        "router_g_w": nrm(ks[14], (DEPTH, D_MODEL, N_GROUPS), D_MODEL ** -0.5),
        "router_g_b": nrm(ks[15], (DEPTH, N_GROUPS), 0.01),
        "router_e_w": nrm(ks[16], (DEPTH, D_MODEL, N_EXPERTS), D_MODEL ** -0.5),
        "router_e_b": nrm(ks[17], (DEPTH, N_GROUPS, EXPERTS_PER_GROUP), 0.01),
        "expert_w_gate": nrm(ks[18], (DEPTH, N_EXPERTS, D_MODEL, D_EXPERT), D_MODEL ** -0.5),
        "expert_w_up": nrm(ks[19], (DEPTH, N_EXPERTS, D_MODEL, D_EXPERT), D_MODEL ** -0.5),
        "expert_w_down": nrm(ks[20], (DEPTH, N_EXPERTS, D_EXPERT, D_MODEL), D_EXPERT ** -0.5 * BETA),
        "ln2_g": 1.0 + nrm(ks[21], (DEPTH, D_MODEL), 0.02),
        "ln2_b": nrm(ks[22], (DEPTH, D_MODEL), 0.02),
    }


def reference(x, positions, ln_in_g, ln_in_b, w_in, hgrn_lb_logits, hgrn_norm_g, fox_f_bias,
              diff_lambda, diff_norm_g, w_branch, w_out, ln1_g, ln1_b, router_g_w, router_g_b,
              router_e_w, router_e_b, expert_w_gate, expert_w_up, expert_w_down, ln2_g, ln2_b):
    inv_freq = ROPE_THETA ** (-jnp.arange(0, ROPE_DIM, 2, dtype=jnp.float32) / ROPE_DIM)
    ang = positions.astype(jnp.float32)[..., None] * inv_freq
    cos, sin = jnp.cos(ang), jnp.sin(ang)
    lb_soft = jax.nn.softmax(hgrn_lb_logits.astype(jnp.float32), axis=0)
    lower_bounds = jnp.maximum(jnp.cumsum(lb_soft, axis=0) - lb_soft[0], 0.0)

    x = layer_norm(x, ln_in_g, ln_in_b)
    for l in range(DEPTH):
        lam_init = 0.8 - 0.6 * float(np.exp(-0.3 * l))
        h = token_mixer(x, cos, sin, w_in[l], lower_bounds[l], hgrn_norm_g[l], fox_f_bias[l],
                        diff_lambda[l], lam_init, diff_norm_g[l], w_branch[l], w_out[l])
        x = layer_norm(ALPHA * x + h, ln1_g[l], ln1_b[l])
        h = hier_moe(x, router_g_w[l], router_g_b[l], router_e_w[l], router_e_b[l],
                     expert_w_gate[l], expert_w_up[l], expert_w_down[l])
        x = layer_norm(ALPHA * x + h, ln2_g[l], ln2_b[l])
    return x
```

```python
import functools
import math

import jax
import jax.numpy as jnp
from jax import lax
from jax.experimental import pallas as pl
from jax.experimental.pallas import tpu as pltpu

F32 = jnp.float32
BF16 = jnp.bfloat16

D_MODEL = 1024
HG_HEADS, HG_D = 4, 128
FOX_HEADS, FOX_DH = 8, 64
DIFF_HEADS, DIFF_DH, DIFF_DV = 4, 64, 128
BRANCH_WIDTH = 512
N_BRANCHES = 3
ROPE_THETA = 500000.0
ROPE_DIM = DIFF_DH // 4
N_GROUPS, EXPERTS_PER_GROUP = 4, 8
N_EXPERTS = N_GROUPS * EXPERTS_PER_GROUP
D_EXPERT = 512
LN_EPS = 1e-5
RMS_EPS = 1e-6
NEG_INF = -1e30
EXP_CLAMP = 60.0

LANES = 128
N_MAIN = 5120
COL_HQ, COL_HF, COL_HI, COL_HG = 0, 4, 8, 12
COL_FQ, COL_FK, COL_FV = 16, 20, 24
COL_DQ, COL_DK, COL_DV = 28, 32, 36
ROPE_TILES = (7, 8)

HG_CHUNK = 64
HG_SUB = 8
GATE_CHUNK = 256
ATTN_BLOCK = 256
EXPERT_TILE = 256
VMEM_LIMIT = 56 * 1024 * 1024


def _dot(a, b):
    return jnp.dot(a, b, preferred_element_type=F32)


def _dot_nt(a, b):
    return lax.dot_general(a, b, (((1,), (1,)), ((), ())), preferred_element_type=F32)


def _log_sigmoid(z):
    return jnp.minimum(z, 0.0) - jnp.log1p(jnp.exp(-jnp.abs(z)))


def _split3(x):
    h1 = x.astype(BF16)
    r1 = x - h1.astype(F32)
    h2 = r1.astype(BF16)
    h3 = (r1 - h2.astype(F32)).astype(BF16)
    return h1, h2, h3


def _cumsum_rows(tri, x):
    h1, h2, h3 = _split3(x)
    return _dot(tri, h1) + _dot(tri, h2) + _dot(tri, h3)


def _layer_norm(u, g, b):
    mu = jnp.mean(u, axis=-1, keepdims=True)
    d = u - mu
    var = jnp.mean(d * d, axis=-1, keepdims=True)
    return d * lax.rsqrt(var + LN_EPS) * g + b


def _params(sem):
    return pltpu.CompilerParams(dimension_semantics=sem, vmem_limit_bytes=VMEM_LIMIT)


def _ln_kernel(x_ref, g_ref, b_ref, o_ref):
    o_ref[...] = _layer_norm(x_ref[...], g_ref[...], b_ref[...])


def _ln_call(x, g, b, tm):
    T, D = x.shape
    return pl.pallas_call(
        _ln_kernel,
        grid=(T // tm,),
        in_specs=[pl.BlockSpec((tm, D), lambda i: (i, 0)),
                  pl.BlockSpec((1, D), lambda i: (0, 0)),
                  pl.BlockSpec((1, D), lambda i: (0, 0))],
        out_specs=pl.BlockSpec((tm, D), lambda i: (i, 0)),
        out_shape=jax.ShapeDtypeStruct((T, D), F32),
        compiler_params=_params(("parallel",)),
        name="ln_in",
    )(x, g.reshape(1, D), b.reshape(1, D))


def _in_proj_kernel(x_ref, w_ref, wff_ref, cos_ref, sa_ref, sb_ref, o_ref, ff_ref):
    xb = x_ref[...].astype(BF16)
    ff_ref[...] = _dot(xb, wff_ref[...])
    for j in range(N_MAIN // 512):
        acc = _dot(xb, w_ref[:, j * 512:(j + 1) * 512])
        if j in ROPE_TILES:
            cosf, sa, sb = cos_ref[...], sa_ref[...], sb_ref[...]
            for g in range(4):
                t = acc[:, g * LANES:(g + 1) * LANES]
                r = t * cosf + pltpu.roll(t, 8, 1) * sa + pltpu.roll(t, LANES - 8, 1) * sb
                o_ref[:, j * 512 + g * LANES:j * 512 + (g + 1) * LANES] = r.astype(BF16)
        else:
            o_ref[:, j * 512:(j + 1) * 512] = acc.astype(BF16)


def _in_proj_call(x, w_main, w_ff, cosf, sina, sinb, tm):
    T, D = x.shape
    const = lambda i: (0, 0)
    row = lambda i: (i, 0)
    return pl.pallas_call(
        _in_proj_kernel,
        grid=(T // tm,),
        in_specs=[pl.BlockSpec((tm, D), row),
                  pl.BlockSpec((D, N_MAIN), const),
                  pl.BlockSpec((D, LANES), const),
                  pl.BlockSpec((tm, LANES), row),
                  pl.BlockSpec((tm, LANES), row),
                  pl.BlockSpec((tm, LANES), row)],
        out_specs=[pl.BlockSpec((tm, N_MAIN), row),
                   pl.BlockSpec((tm, LANES), row)],
        out_shape=[jax.ShapeDtypeStruct((T, N_MAIN), BF16),
                   jax.ShapeDtypeStruct((T, LANES), F32)],
        compiler_params=_params(("parallel",)),
        name="in_proj",
    )(x, w_main, w_ff, cosf, sina, sinb)


def _hgrn_kernel(q_ref, f_ref, i_ref, g_ref, lb_ref, ng_ref, o_ref, b_scr, k_scr, v_scr, *, nchunks):
    C = HG_CHUNK
    lb = lb_ref[...]
    ng = ng_ref[...]
    ri = lax.broadcasted_iota(jnp.int32, (C, C), 0)
    ci = lax.broadcasted_iota(jnp.int32, (C, C), 1)
    tri = jnp.where(ri >= ci, 1.0, 0.0).astype(BF16)
    rows = lax.broadcasted_iota(jnp.int32, (C, LANES), 0)
    sub_rows = lax.broadcasted_iota(jnp.int32, (HG_SUB, LANES), 0)
    levels = []
    m = HG_SUB
    while m < C:
        shift = int(math.log2(2 * m))
        is_query = (rows & (2 * m - 1)) >= m
        pair = ((ri >> shift) == (ci >> shift)) & ((ri & (2 * m - 1)) >= m) & ((ci & (2 * m - 1)) < m)
        levels.append((m, is_query, pair))
        m *= 2

    def body(n, state_t):
        r0 = pl.multiple_of(n * C, C)
        z = f_ref[pl.ds(r0, C), :].astype(F32)
        ql = q_ref[pl.ds(r0, C), :].astype(F32)
        v = i_ref[pl.ds(r0, C), :].astype(F32)
        gl = g_ref[pl.ds(r0, C), :].astype(F32)
        log_f = _log_sigmoid(z) + jnp.log1p(lb * jnp.exp(jnp.minimum(-z, EXP_CLAMP)))
        k = (1.0 - lb) * jax.nn.sigmoid(-z)
        q = ql * jax.nn.sigmoid(ql)
        b = _cumsum_rows(tri, log_f)
        b_scr[...] = b
        k_scr[...] = k
        v_scr[...] = v
        vb = v.astype(BF16)
        b_last = b_scr[C - 1:C, :]

        o = _dot_nt((q * jnp.exp(b)).astype(BF16), state_t.astype(BF16))

        scores = jnp.zeros((C, C), F32)
        for m, is_query, pair in levels:
            pieces = [jnp.broadcast_to(b_scr[p * 2 * m + m - 1:p * 2 * m + m, :], (2 * m, LANES))
                      for p in range(C // (2 * m))]
            b_ref_rows = pieces[0] if len(pieces) == 1 else jnp.concatenate(pieces, axis=0)
            decay = jnp.exp(-jnp.abs(b - b_ref_rows))
            qd = jnp.where(is_query, q * decay, 0.0).astype(BF16)
            kd = jnp.where(is_query, 0.0, k * decay).astype(BF16)
            scores = scores + jnp.where(pair, _dot_nt(qd, kd), 0.0)
        o = o + _dot(scores.astype(BF16), vb)

        diag = []
        for blk in range(C // HG_SUB):
            lo = blk * HG_SUB
            bb = b[lo:lo + HG_SUB]
            qq = q[lo:lo + HG_SUB]
            acc = jnp.zeros((HG_SUB, LANES), F32)
            for s in range(HG_SUB):
                bs = b_scr[lo + s:lo + s + 1, :]
                ks = k_scr[lo + s:lo + s + 1, :]
                vs = v_scr[lo + s:lo + s + 1, :]
                w = qq * ks * jnp.exp(jnp.minimum(bb - bs, 0.0))
                a_col = jnp.sum(w, axis=-1, keepdims=True)
                acc = acc + jnp.where(sub_rows >= s, a_col, 0.0) * vs
            diag.append(acc)
        o = o + jnp.concatenate(diag, axis=0)

        k_dec = (k * jnp.exp(b_last - b)).astype(BF16)
        state_t = state_t * jnp.exp(b_last) + _dot(v.T.astype(BF16), k_dec)

        ms = jnp.mean(o * o, axis=-1, keepdims=True)
        y = o * lax.rsqrt(ms + RMS_EPS) * ng * (gl * jax.nn.sigmoid(gl))
        o_ref[pl.ds(r0, C), :] = y.astype(o_ref.dtype)
        return state_t

    lax.fori_loop(0, nchunks, body, jnp.zeros((HG_D, HG_D), F32))


def _hgrn_call(proj, lb, norm_g, B, S):
    T = B * S
    blk = lambda off: pl.BlockSpec((S, LANES), lambda b, h, off=off: (b, off + h))
    return pl.pallas_call(
        functools.partial(_hgrn_kernel, nchunks=S // HG_CHUNK),
        grid=(B, HG_HEADS),
        in_specs=[blk(COL_HQ), blk(COL_HF), blk(COL_HI), blk(COL_HG),
                  pl.BlockSpec((1, LANES), lambda b, h: (0, h)),
                  pl.BlockSpec((1, LANES), lambda b, h: (0, 0))],
        out_specs=pl.BlockSpec((S, LANES), lambda b, h: (b, h)),
        out_shape=jax.ShapeDtypeStruct((T, BRANCH_WIDTH), BF16),
        scratch_shapes=[pltpu.VMEM((HG_CHUNK, LANES), F32)] * 3,
        compiler_params=_params(("parallel", "parallel")),
        name="hgrn",
    )(proj, proj, proj, proj, lb.reshape(1, HG_HEADS * HG_D), norm_g.reshape(1, HG_D))


def _fox_gate_kernel(ff_ref, bias_ref, c_ref, ct_ref, *, nchunks):
    CH = GATE_CHUNK
    ri = lax.broadcasted_iota(jnp.int32, (CH, CH), 0)
    ci = lax.broadcasted_iota(jnp.int32, (CH, CH), 1)
    tri = jnp.where(ri >= ci, 1.0, 0.0).astype(BF16)
    bias = bias_ref[...]

    def body(n, carry):
        r0 = pl.multiple_of(n * CH, CH)
        c = carry + _cumsum_rows(tri, _log_sigmoid(ff_ref[pl.ds(r0, CH), :] + bias))
        c_ref[pl.ds(r0, CH), :] = c
        ct_ref[0, n] = c.T[0:FOX_HEADS, :]
        return c[CH - 1:CH, :]

    lax.fori_loop(0, nchunks, body, jnp.zeros((1, LANES), F32))


def _fox_gate_call(ff, bias, B, S):
    T = B * S
    nkb = S // GATE_CHUNK
    bias_row = jnp.zeros((1, LANES), F32).at[0, :FOX_HEADS].set(bias.astype(F32))
    return pl.pallas_call(
        functools.partial(_fox_gate_kernel, nchunks=nkb),
        grid=(B,),
        in_specs=[pl.BlockSpec((S, LANES), lambda b: (b, 0)),
                  pl.BlockSpec((1, LANES), lambda b: (0, 0))],
        out_specs=[pl.BlockSpec((S, LANES), lambda b: (b, 0)),
                   pl.BlockSpec((1, nkb, FOX_HEADS, GATE_CHUNK), lambda b: (b, 0, 0, 0))],
        out_shape=[jax.ShapeDtypeStruct((T, LANES), F32),
                   jax.ShapeDtypeStruct((B, nkb, FOX_HEADS, GATE_CHUNK), F32)],
        compiler_params=_params(("parallel",)),
        name="fox_gate",
    )(ff, bias_row)


def _flash_rows(qm, k_ref, v_ref, n_full, bias_fn):
    tq = tk = ATTN_BLOCK
    row = lax.broadcasted_iota(jnp.int32, (tq, tk), 0)
    col = lax.broadcasted_iota(jnp.int32, (tq, tk), 1)

    def step(j, carry, causal):
        m, l, acc = carry
        k0 = pl.multiple_of(j * tk, tk)
        s = _dot_nt(qm, k_ref[pl.ds(k0, tk), :])
        s = bias_fn(s, j)
        if causal:
            s = jnp.where(row >= col, s, NEG_INF)
        m_new = jnp.maximum(m, jnp.max(s, axis=-1, keepdims=True))
        alpha = jnp.exp(m - m_new)
        p = jnp.exp(s - m_new)
        l = alpha * l + jnp.sum(p, axis=-1, keepdims=True)
        acc = alpha * acc + _dot(p.astype(BF16), v_ref[pl.ds(k0, tk), :])
        return m_new, l, acc

    carry = (jnp.full((tq, 1), NEG_INF, F32), jnp.zeros((tq, 1), F32), jnp.zeros((tq, LANES), F32))
    carry = lax.fori_loop(0, n_full, lambda j, c: step(j, c, False), carry)
    _, l, acc = step(n_full, carry, True)
    return acc, l


def _fox_kernel(q_ref, k_ref, v_ref, cq_ref, ct_ref, o_ref):
    pair = pl.program_id(1)
    qi = pl.program_id(2)
    lane = lax.broadcasted_iota(jnp.int32, (1, LANES), 1)
    q2 = q_ref[...]
    cq2 = cq_ref[...]
    scale = FOX_DH ** -0.5
    outs = []
    for hh in range(2):
        head = 2 * pair + hh
        in_head = (lane >= hh * FOX_DH) & (lane < (hh + 1) * FOX_DH)
        qm = jnp.where(in_head, q2 * scale, 0.0).astype(BF16)
        cq = jnp.sum(jnp.where(lane == head, cq2, 0.0), axis=-1, keepdims=True)

        def bias_fn(s, j, head=head, cq=cq):
            return s + (cq - ct_ref[0, j, pl.ds(head, 1), :])

        acc, l = _flash_rows(qm, k_ref, v_ref, qi, bias_fn)
        outs.append(acc / l)
    o_ref[...] = jnp.where(lane < FOX_DH, outs[0], outs[1]).astype(o_ref.dtype)


def _fox_call(proj, c, ct, B, S):
    T = B * S
    tq = ATTN_BLOCK
    nq = S // tq
    nkb = S // GATE_CHUNK
    return pl.pallas_call(
        _fox_kernel,
        grid=(B, FOX_HEADS // 2, nq),
        in_specs=[pl.BlockSpec((tq, LANES), lambda b, p, i: (b * nq + i, COL_FQ + p)),
                  pl.BlockSpec((S, LANES), lambda b, p, i: (b, COL_FK + p)),
                  pl.BlockSpec((S, LANES), lambda b, p, i: (b, COL_FV + p)),
                  pl.BlockSpec((tq, LANES), lambda b, p, i: (b * nq + i, 0)),
                  pl.BlockSpec((1, nkb, FOX_HEADS, GATE_CHUNK), lambda b, p, i: (b, 0, 0, 0))],
        out_specs=pl.BlockSpec((tq, LANES), lambda b, p, i: (b * nq + i, p)),
        out_shape=jax.ShapeDtypeStruct((T, BRANCH_WIDTH), BF16),
        compiler_params=_params(("parallel", "parallel", "arbitrary")),
        name="fox_attn",
    )(proj, proj, proj, c, ct)


def _diff_kernel(q_ref, k_ref, v_ref, lam_ref, ng_ref, o_ref):
    qi = pl.program_id(2)
    lane = lax.broadcasted_iota(jnp.int32, (1, LANES), 1)
    q2 = q_ref[...]
    scale = DIFF_DH ** -0.5
    outs = []
    for mp in range(2):
        in_map = (lane >= mp * DIFF_DH) & (lane < (mp + 1) * DIFF_DH)
        qm = jnp.where(in_map, q2 * scale, 0.0).astype(BF16)
        acc, l = _flash_rows(qm, k_ref, v_ref, qi, lambda s, j: s)
        outs.append(acc / l)
    o = outs[0] - lam_ref[...] * outs[1]
    ms = jnp.mean(o * o, axis=-1, keepdims=True)
    o_ref[...] = (o * lax.rsqrt(ms + RMS_EPS) * ng_ref[...]).astype(o_ref.dtype)


def _diff_call(proj, lam_row, norm_row, B, S):
    T = B * S
    tq = ATTN_BLOCK
    nq = S // tq
    return pl.pallas_call(
        _diff_kernel,
        grid=(B, DIFF_HEADS, nq),
        in_specs=[pl.BlockSpec((tq, LANES), lambda b, h, i: (b * nq + i, COL_DQ + h)),
                  pl.BlockSpec((S, LANES), lambda b, h, i: (b, COL_DK + h)),
                  pl.BlockSpec((S, LANES), lambda b, h, i: (b, COL_DV + h)),
                  pl.BlockSpec((1, LANES), lambda b, h, i: (0, 0)),
                  pl.BlockSpec((1, LANES), lambda b, h, i: (0, 0))],
        out_specs=pl.BlockSpec((tq, LANES), lambda b, h, i: (b * nq + i, h)),
        out_shape=jax.ShapeDtypeStruct((T, BRANCH_WIDTH), BF16),
        compiler_params=_params(("parallel", "parallel", "arbitrary")),
        name="diff_attn",
    )(proj, proj, proj, lam_row, norm_row)


def _merge_kernel(x_ref, ya_ref, yb_ref, yc_ref, wg_ref, wb_ref, wo_ref, g1_ref, b1_ref, wr_ref, br_ref,
                  x1_ref, ids_ref, wts_ref, *, alpha):
    x = x_ref[...]
    xb = x.astype(BF16)
    merged = None
    for r, y_ref in enumerate((ya_ref, yb_ref, yc_ref)):
        gate = jax.nn.sigmoid(_dot(xb, wg_ref[:, r * D_MODEL:(r + 1) * D_MODEL]))
        term = gate * _dot(y_ref[...], wb_ref[r])
        merged = term if merged is None else merged + term
    h = _dot(merged.astype(BF16), wo_ref[...])
    x1 = _layer_norm(alpha * x + h, g1_ref[...], b1_ref[...])
    x1_ref[...] = x1

    logits = _dot(x1.astype(BF16), wr_ref[...]) + br_ref[...]
    lane = lax.broadcasted_iota(jnp.int32, logits.shape, 1)
    lane_f = lane.astype(F32)
    is_group = lane < N_GROUPS
    gl = jnp.where(is_group, logits, NEG_INF)
    gmax = jnp.max(gl, axis=-1, keepdims=True)
    gsum = jnp.sum(jnp.where(is_group, jnp.exp(gl - gmax), 0.0), axis=-1, keepdims=True)
    g_p = 1.0 / gsum
    g_idx = jnp.min(jnp.where(gl == gmax, lane_f, float(LANES)), axis=-1, keepdims=True)
    lo = N_GROUPS + EXPERTS_PER_GROUP * g_idx
    in_group = (lane_f >= lo) & (lane_f < lo + EXPERTS_PER_GROUP)
    el = jnp.where(in_group, logits, NEG_INF)
    v1 = jnp.max(el, axis=-1, keepdims=True)
    i1 = jnp.min(jnp.where(el == v1, lane_f, float(LANES)), axis=-1, keepdims=True)
    el2 = jnp.where(lane_f == i1, NEG_INF, el)
    v2 = jnp.max(el2, axis=-1, keepdims=True)
    i2 = jnp.min(jnp.where(el2 == v2, lane_f, float(LANES)), axis=-1, keepdims=True)
    t = jnp.exp(v2 - v1)
    w1 = g_p / (1.0 + t)
    w2 = g_p * t / (1.0 + t)
    ids = jnp.where(lane == 0, i1 - N_GROUPS, jnp.where(lane == 1, i2 - N_GROUPS, 0.0))
    ids_ref[...] = ids.astype(jnp.int32)
    wts_ref[...] = jnp.where(lane == 0, w1, jnp.where(lane == 1, w2, 0.0))


def _merge_call(x, ya, yb, yc, w_gates, w_branch, w_out, ln_g, ln_b, w_router, b_router, alpha, tm):
    T, D = x.shape
    row = lambda i: (i, 0)
    const2 = lambda i: (0, 0)
    return pl.pallas_call(
        functools.partial(_merge_kernel, alpha=alpha),
        grid=(T // tm,),
        in_specs=[pl.BlockSpec((tm, D), row),
                  pl.BlockSpec((tm, BRANCH_WIDTH), row),
                  pl.BlockSpec((tm, BRANCH_WIDTH), row),
                  pl.BlockSpec((tm, BRANCH_WIDTH), row),
                  pl.BlockSpec((D, N_BRANCHES * D), const2),
                  pl.BlockSpec((N_BRANCHES, BRANCH_WIDTH, D), lambda i: (0, 0, 0)),
                  pl.BlockSpec((D, D), const2),
                  pl.BlockSpec((1, D), const2),
                  pl.BlockSpec((1, D), const2),
                  pl.BlockSpec((D, LANES), const2),
                  pl.BlockSpec((1, LANES), const2)],
        out_specs=[pl.BlockSpec((tm, D), row),
                   pl.BlockSpec((tm, LANES), row),
                   pl.BlockSpec((tm, LANES), row)],
        out_shape=[jax.ShapeDtypeStruct((T, D), F32),
                   jax.ShapeDtypeStruct((T, LANES), jnp.int32),
                   jax.ShapeDtypeStruct((T, LANES), F32)],
        compiler_params=_params(("parallel",)),
        name="merge_ln1_router",
    )(x, ya, yb, yc, w_gates, w_branch, w_out, ln_g.reshape(1, D), ln_b.reshape(1, D), w_router, b_router)


def _expert_kernel(te_ref, nu_ref, xs_ref, wg_ref, wu_ref, wd_ref, o_ref):
    t = pl.program_id(0)

    @pl.when(t < nu_ref[0])
    def _():
        xb = xs_ref[...].astype(BF16)
        g = _dot(xb, wg_ref[0])
        u = _dot(xb, wu_ref[0])
        h = (g * jax.nn.sigmoid(g)) * u
        o_ref[...] = _dot(h.astype(BF16), wd_ref[0])

    @pl.when(t >= nu_ref[0])
    def _():
        o_ref[...] = jnp.zeros_like(o_ref)


def _expert_call(tile_expert, n_used, xs, w_gate, w_up, w_down):
    P, D = xs.shape
    tm = EXPERT_TILE
    grid_spec = pltpu.PrefetchScalarGridSpec(
        num_scalar_prefetch=2,
        grid=(P // tm,),
        in_specs=[pl.BlockSpec((tm, D), lambda t, te, nu: (t, 0)),
                  pl.BlockSpec((1, D, D_EXPERT), lambda t, te, nu: (te[t], 0, 0)),
                  pl.BlockSpec((1, D, D_EXPERT), lambda t, te, nu: (te[t], 0, 0)),
                  pl.BlockSpec((1, D_EXPERT, D), lambda t, te, nu: (te[t], 0, 0))],
        out_specs=pl.BlockSpec((tm, D), lambda t, te, nu: (t, 0)),
    )
    return pl.pallas_call(
        _expert_kernel,
        grid_spec=grid_spec,
        out_shape=jax.ShapeDtypeStruct((P, D), F32),
        compiler_params=_params(("arbitrary",)),
        name="experts",
    )(tile_expert, n_used, xs, w_gate, w_up, w_down)


def _combine_kernel(x_ref, y0_ref, y1_ref, w_ref, g_ref, b_ref, o_ref, *, alpha):
    w = w_ref[...]
    u = alpha * x_ref[...] + w[:, 0:1] * y0_ref[...] + w[:, 1:2] * y1_ref[...]
    o_ref[...] = _layer_norm(u, g_ref[...], b_ref[...])


def _combine_call(x1, y0, y1, wts, ln_g, ln_b, alpha, tm):
    T, D = x1.shape
    row = lambda i: (i, 0)
    const2 = lambda i: (0, 0)
    return pl.pallas_call(
        functools.partial(_combine_kernel, alpha=alpha),
        grid=(T // tm,),
        in_specs=[pl.BlockSpec((tm, D), row), pl.BlockSpec((tm, D), row), pl.BlockSpec((tm, D), row),
                  pl.BlockSpec((tm, LANES), row),
                  pl.BlockSpec((1, D), const2), pl.BlockSpec((1, D), const2)],
        out_specs=pl.BlockSpec((tm, D), row),
        out_shape=jax.ShapeDtypeStruct((T, D), F32),
        compiler_params=_params(("parallel",)),
        name="combine_ln2",
    )(x1, y0, y1, wts, ln_g.reshape(1, D), ln_b.reshape(1, D))


def _rope_tables(positions):
    half = ROPE_DIM // 2
    inv_freq = ROPE_THETA ** (-jnp.arange(0, ROPE_DIM, 2, dtype=F32) / ROPE_DIM)
    ang = positions.astype(F32).reshape(-1, 1) * inv_freq[None, :]
    cos, sin = jnp.cos(ang), jnp.sin(ang)
    lane = jnp.arange(LANES)
    in_head = lane % DIFF_DH
    freq = in_head % half
    first = in_head < half
    second = (in_head >= half) & (in_head < ROPE_DIM)
    cosf = jnp.where((first | second)[None, :], cos[:, freq], 1.0)
    sin_a = jnp.where(second[None, :], sin[:, freq], 0.0)
    sin_b = jnp.where(first[None, :], -sin[:, freq], 0.0)
    return cosf, sin_a, sin_b


def _dispatch_plan(ids, T):
    tm = EXPERT_TILE
    flat = ids.reshape(-1)
    onehot = (flat[:, None] == jnp.arange(N_EXPERTS, dtype=jnp.int32)[None, :]).astype(jnp.int32)
    csum = jnp.cumsum(onehot, axis=0)
    counts = csum[-1]
    rank = jnp.take_along_axis(csum, flat[:, None], axis=1)[:, 0] - 1
    padded = ((counts + tm - 1) // tm) * tm
    starts = jnp.cumsum(padded) - padded
    pos = starts[flat] + rank
    n_rows = 2 * T + N_EXPERTS * tm
    n_tiles = n_rows // tm
    tile_start = jnp.arange(n_tiles, dtype=jnp.int32) * tm
    ends = jnp.cumsum(padded)
    tile_expert = jnp.minimum(jnp.sum((tile_start[:, None] >= ends[None, :]).astype(jnp.int32), axis=1),
                              N_EXPERTS - 1).astype(jnp.int32)
    n_used = (ends[-1] // tm).astype(jnp.int32).reshape(1)
    return pos, tile_expert, n_used, n_rows


def _layer(x, cosf, sina, sinb, B, S, lb, p, alpha, lam_init):
    T = B * S
    tm = min(512, T)
    w_in = p["w_in"]
    w_main = jnp.concatenate([w_in[:, :3584], w_in[:, 3592:5128]], axis=1).astype(BF16)
    w_ff = jnp.zeros((D_MODEL, LANES), BF16).at[:, :FOX_HEADS].set(w_in[:, 3584:3592].astype(BF16))
    w_gates = w_in[:, 5128:].astype(BF16)

    proj, ff = _in_proj_call(x, w_main, w_ff, cosf, sina, sinb, tm)
    ya = _hgrn_call(proj, lb, p["hgrn_norm_g"], B, S)
    c, ct = _fox_gate_call(ff, p["fox_f_bias"], B, S)
    yb = _fox_call(proj, c, ct, B, S)
    lv = p["diff_lambda"].astype(F32)
    lam = jnp.exp(jnp.sum(lv[0] * lv[1])) - jnp.exp(jnp.sum(lv[2] * lv[3])) + lam_init
    lam_row = jnp.full((1, LANES), lam, F32)
    norm_row = (p["diff_norm_g"].astype(F32) * (1.0 - lam_init)).reshape(1, DIFF_DV)
    yc = _diff_call(proj, lam_row, norm_row, B, S)

    w_router = jnp.zeros((D_MODEL, LANES), F32)
    w_router = w_router.at[:, :N_GROUPS].set(p["router_g_w"]).at[:, N_GROUPS:N_GROUPS + N_EXPERTS].set(p["router_e_w"])
    b_router = jnp.zeros((1, LANES), F32)
    b_router = b_router.at[0, :N_GROUPS].set(p["router_g_b"]).at[0, N_GROUPS:N_GROUPS + N_EXPERTS].set(
        p["router_e_b"].reshape(-1))
    x1, ids, wts = _merge_call(x, ya, yb, yc, w_gates, p["w_branch"].astype(BF16), p["w_out"].astype(BF16),
                               p["ln1_g"], p["ln1_b"], w_router.astype(BF16), b_router, alpha, tm)

    pos, tile_expert, n_used, n_rows = _dispatch_plan(ids[:, :2], T)
    row_token = jnp.zeros((n_rows,), jnp.int32).at[pos].set(jnp.arange(2 * T, dtype=jnp.int32) // 2)
    xs = jnp.take(x1, row_token, axis=0)
    y = _expert_call(tile_expert, n_used, xs, p["expert_w_gate"].astype(BF16), p["expert_w_up"].astype(BF16),
                     p["expert_w_down"].astype(BF16))
    yg = jnp.take(y, pos, axis=0).reshape(T, 2, D_MODEL)
    return _combine_call(x1, yg[:, 0], yg[:, 1], wts, p["ln2_g"], p["ln2_b"], alpha, tm)


def kernel(x, positions, ln_in_g, ln_in_b, w_in, hgrn_lb_logits, hgrn_norm_g, fox_f_bias, diff_lambda,
           diff_norm_g, w_branch, w_out, ln1_g, ln1_b, router_g_w, router_g_b, router_e_w, router_e_b,
           expert_w_gate, expert_w_up, expert_w_down, ln2_g, ln2_b):
    B, S, D = x.shape
    T = B * S
    depth = w_in.shape[0]
    alpha = (2 * depth) ** 0.25
    cosf, sina, sinb = _rope_tables(positions)
    lb_soft = jax.nn.softmax(hgrn_lb_logits.astype(F32), axis=0)
    lower_bounds = jnp.maximum(jnp.cumsum(lb_soft, axis=0) - lb_soft[0], 0.0)

    h = _ln_call(x.reshape(T, D), ln_in_g, ln_in_b, min(512, T))
    for l in range(depth):
        p = dict(w_in=w_in[l], hgrn_norm_g=hgrn_norm_g[l], fox_f_bias=fox_f_bias[l], diff_lambda=diff_lambda[l],
                 diff_norm_g=diff_norm_g[l], w_branch=w_branch[l], w_out=w_out[l], ln1_g=ln1_g[l], ln1_b=ln1_b[l],
                 router_g_w=router_g_w[l], router_g_b=router_g_b[l], router_e_w=router_e_w[l],
                 router_e_b=router_e_b[l], expert_w_gate=expert_w_gate[l], expert_w_up=expert_w_up[l],
                 expert_w_down=expert_w_down[l], ln2_g=ln2_g[l], ln2_b=ln2_b[l])
        lam_init = 0.8 - 0.6 * float(math.exp(-0.3 * l))
        h = _layer(h, cosf, sina, sinb, B, S, lower_bounds[l], p, alpha, lam_init)
    return h.reshape(B, S, D)
```

```python
import functools
import math

import numpy as np
import jax
import jax.numpy as jnp
from jax import lax
from jax.experimental import pallas as pl
from jax.experimental.pallas import tpu as pltpu

F32 = jnp.float32
BF16 = jnp.bfloat16

D_MODEL = 1024
HG_HEADS, HG_D = 4, 128
FOX_HEADS, FOX_DH = 8, 64
DIFF_HEADS, DIFF_DH, DIFF_DV = 4, 64, 128
BRANCH_WIDTH = 512
N_BRANCHES = 3
ROPE_THETA = 500000.0
ROPE_DIM = DIFF_DH // 4
N_GROUPS, EXPERTS_PER_GROUP = 4, 8
N_EXPERTS = N_GROUPS * EXPERTS_PER_GROUP
D_EXPERT = 512
LN_EPS = 1e-5
RMS_EPS = 1e-6
NEG_INF = -1e30
EXP_CLAMP = 60.0
LOG2E = 1.4426950408889634

LANES = 128
N_MAIN = 5120
COL_HQ, COL_HF, COL_HI, COL_HG = 0, 4, 8, 12
COL_FQ, COL_FK, COL_FV = 16, 20, 24
COL_DQ, COL_DK, COL_DV = 28, 32, 36
ROPE_TILES = (7, 8)

HG_CHUNK = 64
HG_SUB = 8
GATE_CHUNK = 256
ATTN_BLOCK = 512
EXPERT_TILE = 256
VMEM_LIMIT = 56 * 1024 * 1024


def _dot(a, b):
    return jnp.dot(a, b, preferred_element_type=F32)


def _dot_nt(a, b):
    return lax.dot_general(a, b, (((1,), (1,)), ((), ())), preferred_element_type=F32)


def _log_sigmoid(z):
    return jnp.minimum(z, 0.0) - jnp.log1p(jnp.exp(-jnp.abs(z)))


def _split3(x):
    h1 = x.astype(BF16)
    r1 = x - h1.astype(F32)
    h2 = r1.astype(BF16)
    h3 = (r1 - h2.astype(F32)).astype(BF16)
    return h1, h2, h3


def _cumsum_rows(tri, x):
    h1, h2, h3 = _split3(x)
    return _dot(tri, h1) + _dot(tri, h2) + _dot(tri, h3)


def _layer_norm(u, g, b):
    mu = jnp.mean(u, axis=-1, keepdims=True)
    d = u - mu
    var = jnp.mean(d * d, axis=-1, keepdims=True)
    return d * lax.rsqrt(var + LN_EPS) * g + b


def _params(sem):
    return pltpu.CompilerParams(dimension_semantics=sem, vmem_limit_bytes=VMEM_LIMIT)


def _ln_kernel(x_ref, g_ref, b_ref, o_ref):
    o_ref[...] = _layer_norm(x_ref[...], g_ref[...], b_ref[...])


def _ln_call(x, g, b, tm):
    T, D = x.shape
    return pl.pallas_call(
        _ln_kernel,
        grid=(T // tm,),
        in_specs=[pl.BlockSpec((tm, D), lambda i: (i, 0)),
                  pl.BlockSpec((1, D), lambda i: (0, 0)),
                  pl.BlockSpec((1, D), lambda i: (0, 0))],
        out_specs=pl.BlockSpec((tm, D), lambda i: (i, 0)),
        out_shape=jax.ShapeDtypeStruct((T, D), F32),
        compiler_params=_params(("parallel",)),
        name="ln_in",
    )(x, g.reshape(1, D), b.reshape(1, D))


def _in_proj_kernel(x_ref, w_ref, wff_ref, cos_ref, sa_ref, sb_ref, o_ref, ff_ref):
    xb = x_ref[...].astype(BF16)
    ff_ref[...] = _dot(xb, wff_ref[...])
    for j in range(N_MAIN // 512):
        acc = _dot(xb, w_ref[:, j * 512:(j + 1) * 512])
        if j in ROPE_TILES:
            cosf, sa, sb = cos_ref[...], sa_ref[...], sb_ref[...]
            for g in range(4):
                t = acc[:, g * LANES:(g + 1) * LANES]
                r = t * cosf + pltpu.roll(t, 8, 1) * sa + pltpu.roll(t, LANES - 8, 1) * sb
                o_ref[:, j * 512 + g * LANES:j * 512 + (g + 1) * LANES] = r.astype(BF16)
        else:
            o_ref[:, j * 512:(j + 1) * 512] = acc.astype(BF16)


def _in_proj_call(x, w_main, w_ff, cosf, sina, sinb, tm):
    T, D = x.shape
    const = lambda i: (0, 0)
    row = lambda i: (i, 0)
    return pl.pallas_call(
        _in_proj_kernel,
        grid=(T // tm,),
        in_specs=[pl.BlockSpec((tm, D), row),
                  pl.BlockSpec((D, N_MAIN), const),
                  pl.BlockSpec((D, LANES), const),
                  pl.BlockSpec((tm, LANES), row),
                  pl.BlockSpec((tm, LANES), row),
                  pl.BlockSpec((tm, LANES), row)],
        out_specs=[pl.BlockSpec((tm, N_MAIN), row),
                   pl.BlockSpec((tm, LANES), row)],
        out_shape=[jax.ShapeDtypeStruct((T, N_MAIN), BF16),
                   jax.ShapeDtypeStruct((T, LANES), F32)],
        compiler_params=_params(("parallel",)),
        name="in_proj",
    )(x, w_main, w_ff, cosf, sina, sinb)


def _hgrn_chunk(z, ql, v, gl, lb, ng, state_t, b_scr, k_scr, v_scr, consts):
    C = HG_CHUNK
    tri, levels, sub_rows = consts
    log_f = _log_sigmoid(z) + jnp.log1p(lb * jnp.exp(jnp.minimum(-z, EXP_CLAMP)))
    k = (1.0 - lb) * jax.nn.sigmoid(-z)
    q = ql * jax.nn.sigmoid(ql)
    b = _cumsum_rows(tri, log_f)
    b_scr[...] = b
    k_scr[...] = k
    v_scr[...] = v
    vb = v.astype(BF16)
    b_last = b_scr[C - 1:C, :]

    o = _dot_nt((q * jnp.exp(b)).astype(BF16), state_t.astype(BF16))

    scores = jnp.zeros((C, C), F32)
    for m, is_query, pair in levels:
        pieces = [jnp.broadcast_to(b_scr[p * 2 * m + m - 1:p * 2 * m + m, :], (2 * m, LANES))
                  for p in range(C // (2 * m))]
        b_ref_rows = pieces[0] if len(pieces) == 1 else jnp.concatenate(pieces, axis=0)
        decay = jnp.exp(-jnp.abs(b - b_ref_rows))
        qd = jnp.where(is_query, q * decay, 0.0).astype(BF16)
        kd = jnp.where(is_query, 0.0, k * decay).astype(BF16)
        scores = scores + jnp.where(pair, _dot_nt(qd, kd), 0.0)
    o = o + _dot(scores.astype(BF16), vb)

    diag = []
    for blk in range(C // HG_SUB):
        lo = blk * HG_SUB
        bb = b[lo:lo + HG_SUB]
        qq = q[lo:lo + HG_SUB]
        acc = jnp.zeros((HG_SUB, LANES), F32)
        for s in range(HG_SUB):
            bs = b_scr[lo + s:lo + s + 1, :]
            ks = k_scr[lo + s:lo + s + 1, :]
            vs = v_scr[lo + s:lo + s + 1, :]
            w = qq * ks * jnp.exp(jnp.minimum(bb - bs, 0.0))
            a_col = jnp.sum(w, axis=-1, keepdims=True)
            acc = acc + jnp.where(sub_rows >= s, a_col, 0.0) * vs
        diag.append(acc)
    o = o + jnp.concatenate(diag, axis=0)

    k_dec = (k * jnp.exp(b_last - b)).astype(BF16)
    state_t = state_t * jnp.exp(b_last) + _dot(v.T.astype(BF16), k_dec)

    ms = jnp.mean(o * o, axis=-1, keepdims=True)
    y = o * lax.rsqrt(ms + RMS_EPS) * ng * (gl * jax.nn.sigmoid(gl))
    return y, state_t


def _hgrn_kernel(q_ref, f_ref, i_ref, g_ref, lb_ref, ng_ref, o_ref, b_scr, k_scr, v_scr, *, nchunks):
    C = HG_CHUNK
    ng = ng_ref[...]
    ri = lax.broadcasted_iota(jnp.int32, (C, C), 0)
    ci = lax.broadcasted_iota(jnp.int32, (C, C), 1)
    tri = jnp.where(ri >= ci, 1.0, 0.0).astype(BF16)
    rows = lax.broadcasted_iota(jnp.int32, (C, LANES), 0)
    sub_rows = lax.broadcasted_iota(jnp.int32, (HG_SUB, LANES), 0)
    levels = []
    m = HG_SUB
    while m < C:
        shift = int(math.log2(2 * m))
        is_query = (rows & (2 * m - 1)) >= m
        pair = ((ri >> shift) == (ci >> shift)) & ((ri & (2 * m - 1)) >= m) & ((ci & (2 * m - 1)) < m)
        levels.append((m, is_query, pair))
        m *= 2
    consts = (tri, levels, sub_rows)

    def body(n, states):
        r0 = pl.multiple_of(n * C, C)
        new_states = []
        for h in range(HG_HEADS):
            cols = slice(h * LANES, (h + 1) * LANES)
            y, st = _hgrn_chunk(f_ref[pl.ds(r0, C), cols].astype(F32), q_ref[pl.ds(r0, C), cols].astype(F32),
                                i_ref[pl.ds(r0, C), cols].astype(F32), g_ref[pl.ds(r0, C), cols].astype(F32),
                                lb_ref[:, cols], ng, states[h], b_scr.at[h], k_scr.at[h], v_scr.at[h], consts)
            o_ref[pl.ds(r0, C), cols] = y.astype(o_ref.dtype)
            new_states.append(st)
        return tuple(new_states)

    lax.fori_loop(0, nchunks, body, tuple(jnp.zeros((HG_D, HG_D), F32) for _ in range(HG_HEADS)))


def _hgrn_call(proj, lb, norm_g, B, S):
    T = B * S
    W = HG_HEADS * HG_D
    blk = lambda off: pl.BlockSpec((S, W), lambda b, off=off: (b, off))
    return pl.pallas_call(
        functools.partial(_hgrn_kernel, nchunks=S // HG_CHUNK),
        grid=(B,),
        in_specs=[blk(COL_HQ // 4), blk(COL_HF // 4), blk(COL_HI // 4), blk(COL_HG // 4),
                  pl.BlockSpec((1, W), lambda b: (0, 0)),
                  pl.BlockSpec((1, LANES), lambda b: (0, 0))],
        out_specs=pl.BlockSpec((S, W), lambda b: (b, 0)),
        out_shape=jax.ShapeDtypeStruct((T, BRANCH_WIDTH), BF16),
        scratch_shapes=[pltpu.VMEM((HG_HEADS, HG_CHUNK, LANES), F32)] * 3,
        compiler_params=_params(("parallel",)),
        name="hgrn",
    )(proj, proj, proj, proj, lb.reshape(1, W), norm_g.reshape(1, HG_D))


def _fox_aug_tables():
    sel_q = np.zeros((3, LANES, BRANCH_WIDTH), np.float32)
    sel_k = np.zeros((3, LANES, BRANCH_WIDTH), np.float32)
    ones_q = np.zeros((1, BRANCH_WIDTH), np.float32)
    ones_k = np.zeros((1, BRANCH_WIDTH), np.float32)
    for col in range(BRANCH_WIDTH):
        pair, within = divmod(col, LANES)
        half, slot = divmod(within, FOX_DH)
        head = 2 * pair + 1 - half
        if slot < 3:
            sel_q[slot, head, col] = 1.0
            ones_k[0, col] = 1.0
        elif slot < 6:
            sel_k[slot - 3, head, col] = -1.0
            ones_q[0, col] = 1.0
    return sel_q, sel_k, ones_q, ones_k


def _fox_gate_kernel(ff_ref, bias_ref, selq_ref, selk_ref, oq_ref, ok_ref, qa_ref, ka_ref, *, nchunks):
    CH = GATE_CHUNK
    ri = lax.broadcasted_iota(jnp.int32, (CH, CH), 0)
    ci = lax.broadcasted_iota(jnp.int32, (CH, CH), 1)
    tri = jnp.where(ri >= ci, 1.0, 0.0).astype(BF16)
    bias = bias_ref[...]

    def body(n, carry):
        r0 = pl.multiple_of(n * CH, CH)
        c = carry + _cumsum_rows(tri, _log_sigmoid(ff_ref[pl.ds(r0, CH), :] + bias))
        parts = _split3(c * LOG2E)
        qa = oq_ref[...] + _dot(parts[0], selq_ref[0]) + _dot(parts[1], selq_ref[1]) + _dot(parts[2], selq_ref[2])
        ka = ok_ref[...] + _dot(parts[0], selk_ref[0]) + _dot(parts[1], selk_ref[1]) + _dot(parts[2], selk_ref[2])
        qa_ref[pl.ds(r0, CH), :] = qa.astype(BF16)
        ka_ref[pl.ds(r0, CH), :] = ka.astype(BF16)
        return c[CH - 1:CH, :]

    lax.fori_loop(0, nchunks, body, jnp.zeros((1, LANES), F32))


def _fox_gate_call(ff, bias, B, S):
    T = B * S
    bias_row = jnp.zeros((1, LANES), F32).at[0, :FOX_HEADS].set(bias.astype(F32))
    sel_q, sel_k, ones_q, ones_k = _fox_aug_tables()
    const2 = lambda b: (0, 0)
    const3 = lambda b: (0, 0, 0)
    return pl.pallas_call(
        functools.partial(_fox_gate_kernel, nchunks=S // GATE_CHUNK),
        grid=(B,),
        in_specs=[pl.BlockSpec((S, LANES), lambda b: (b, 0)),
                  pl.BlockSpec((1, LANES), const2),
                  pl.BlockSpec((3, LANES, BRANCH_WIDTH), const3),
                  pl.BlockSpec((3, LANES, BRANCH_WIDTH), const3),
                  pl.BlockSpec((1, BRANCH_WIDTH), const2),
                  pl.BlockSpec((1, BRANCH_WIDTH), const2)],
        out_specs=[pl.BlockSpec((S, BRANCH_WIDTH), lambda b: (b, 0)),
                   pl.BlockSpec((S, BRANCH_WIDTH), lambda b: (b, 0))],
        out_shape=[jax.ShapeDtypeStruct((T, BRANCH_WIDTH), BF16),
                   jax.ShapeDtypeStruct((T, BRANCH_WIDTH), BF16)],
        compiler_params=_params(("parallel",)),
        name="fox_gate",
    )(ff, bias_row, jnp.asarray(sel_q, BF16), jnp.asarray(sel_k, BF16), jnp.asarray(ones_q), jnp.asarray(ones_k))


def _flash_t(qs, k_fns, vt_fns, n_full):
    tq = tk = ATTN_BLOCK
    krow = lax.broadcasted_iota(jnp.int32, (tk, tq), 0)
    qcol = lax.broadcasted_iota(jnp.int32, (tk, tq), 1)

    def step(j, carry, causal):
        k0 = pl.multiple_of(j * tk, tk)
        out = []
        for qa, k_fn, vt_fn, (m, l, acc) in zip(qs, k_fns, vt_fns, carry):
            s = _dot_nt(k_fn(k0), qa)
            if causal:
                s = jnp.where(krow <= qcol, s, NEG_INF)
            m_new = jnp.maximum(m, jnp.max(s, axis=0, keepdims=True))
            alpha = jnp.exp2(m - m_new)
            p = jnp.exp2(s - m_new)
            l = alpha * l + jnp.sum(p, axis=0, keepdims=True)
            acc = alpha * acc + _dot(vt_fn(k0), p.astype(BF16))
            out.append((m_new, l, acc))
        return tuple(out)

    init = []
    for vt_fn in vt_fns:
        dv = vt_fn(0).shape[0]
        init.append((jnp.full((1, tq), NEG_INF, F32), jnp.zeros((1, tq), F32), jnp.zeros((dv, tq), F32)))
    carry = lax.fori_loop(0, n_full, lambda j, c: step(j, c, False), tuple(init))
    carry = step(n_full, carry, True)
    return [(acc, l) for (_, l, acc) in carry]


def _fill_vt(v_ref, vt_scr):
    S = v_ref.shape[0]
    for c in range(S // ATTN_BLOCK):
        sl = slice(c * ATTN_BLOCK, (c + 1) * ATTN_BLOCK)
        vt_scr[:, sl] = v_ref[sl, :].astype(F32).T.astype(BF16)


def _fox_kernel(q_ref, qa_ref, k_ref, ka_ref, v_ref, o_ref, vt_scr):
    qi = pl.program_id(2)
    tk = ATTN_BLOCK

    @pl.when(qi == 0)
    def _():
        _fill_vt(v_ref, vt_scr)

    lo = lax.broadcasted_iota(jnp.int32, (1, LANES), 1) < FOX_DH
    qf = q_ref[...].astype(F32) * (FOX_DH ** -0.5 * LOG2E)
    qa = qa_ref[...].astype(F32)
    qs = [jnp.where(lo, qf, qa).astype(BF16), jnp.where(lo, qa, qf).astype(BF16)]
    k_fns = [lambda k0: jnp.where(lo, k_ref[pl.ds(k0, tk), :], ka_ref[pl.ds(k0, tk), :]),
             lambda k0: jnp.where(lo, ka_ref[pl.ds(k0, tk), :], k_ref[pl.ds(k0, tk), :])]
    vt_fns = [lambda k0: vt_scr[0:FOX_DH, pl.ds(k0, tk)],
              lambda k0: vt_scr[FOX_DH:2 * FOX_DH, pl.ds(k0, tk)]]
    (acc0, l0), (acc1, l1) = _flash_t(qs, k_fns, vt_fns, qi)
    o_t = jnp.concatenate([acc0 / l0, acc1 / l1], axis=0)
    o_ref[...] = o_t.T.astype(o_ref.dtype)


def _fox_call(proj, qa, ka, B, S):
    T = B * S
    tq = ATTN_BLOCK
    nq = S // tq
    return pl.pallas_call(
        _fox_kernel,
        grid=(B, FOX_HEADS // 2, nq),
        in_specs=[pl.BlockSpec((tq, LANES), lambda b, p, i: (b * nq + i, COL_FQ + p)),
                  pl.BlockSpec((tq, LANES), lambda b, p, i: (b * nq + i, p)),
                  pl.BlockSpec((S, LANES), lambda b, p, i: (b, COL_FK + p)),
                  pl.BlockSpec((S, LANES), lambda b, p, i: (b, p)),
                  pl.BlockSpec((S, LANES), lambda b, p, i: (b, COL_FV + p))],
        out_specs=pl.BlockSpec((tq, LANES), lambda b, p, i: (b * nq + i, p)),
        out_shape=jax.ShapeDtypeStruct((T, BRANCH_WIDTH), BF16),
        scratch_shapes=[pltpu.VMEM((LANES, S), BF16)],
        compiler_params=_params(("parallel", "parallel", "arbitrary")),
        name="fox_attn",
    )(proj, qa, proj, ka, proj)


def _diff_kernel(q_ref, k_ref, v_ref, lam_ref, ng_ref, o_ref, vt_scr):
    qi = pl.program_id(2)
    tk = ATTN_BLOCK

    @pl.when(qi == 0)
    def _():
        _fill_vt(v_ref, vt_scr)

    lo = lax.broadcasted_iota(jnp.int32, (1, LANES), 1) < DIFF_DH
    qf = q_ref[...].astype(F32) * (DIFF_DH ** -0.5 * LOG2E)
    qs = [jnp.where(lo, qf, 0.0).astype(BF16), jnp.where(lo, 0.0, qf).astype(BF16)]
    k_fns = [lambda k0: k_ref[pl.ds(k0, tk), :]] * 2
    vt_fns = [lambda k0: vt_scr[:, pl.ds(k0, tk)]] * 2
    (acc0, l0), (acc1, l1) = _flash_t(qs, k_fns, vt_fns, qi)
    o = (acc0 / l0 - lam_ref[0:1, 0:1] * (acc1 / l1)).T
    ms = jnp.mean(o * o, axis=-1, keepdims=True)
    o_ref[...] = (o * lax.rsqrt(ms + RMS_EPS) * ng_ref[...]).astype(o_ref.dtype)


def _diff_call(proj, lam_row, norm_row, B, S):
    T = B * S
    tq = ATTN_BLOCK
    nq = S // tq
    return pl.pallas_call(
        _diff_kernel,
        grid=(B, DIFF_HEADS, nq),
        in_specs=[pl.BlockSpec((tq, LANES), lambda b, h, i: (b * nq + i, COL_DQ + h)),
                  pl.BlockSpec((S, LANES), lambda b, h, i: (b, COL_DK + h)),
                  pl.BlockSpec((S, LANES), lambda b, h, i: (b, COL_DV + h)),
                  pl.BlockSpec((1, LANES), lambda b, h, i: (0, 0)),
                  pl.BlockSpec((1, LANES), lambda b, h, i: (0, 0))],
        out_specs=pl.BlockSpec((tq, LANES), lambda b, h, i: (b * nq + i, h)),
        out_shape=jax.ShapeDtypeStruct((T, BRANCH_WIDTH), BF16),
        scratch_shapes=[pltpu.VMEM((LANES, S), BF16)],
        compiler_params=_params(("parallel", "parallel", "arbitrary")),
        name="diff_attn",
    )(proj, proj, proj, lam_row, norm_row)


def _merge_kernel(x_ref, ya_ref, yb_ref, yc_ref, wg_ref, wb_ref, wo_ref, g1_ref, b1_ref, wr_ref, br_ref,
                  x1_ref, ids_ref, wts_ref, *, alpha):
    x = x_ref[...]
    xb = x.astype(BF16)
    merged = None
    for r, y_ref in enumerate((ya_ref, yb_ref, yc_ref)):
        gate = jax.nn.sigmoid(_dot(xb, wg_ref[:, r * D_MODEL:(r + 1) * D_MODEL]))
        term = gate * _dot(y_ref[...], wb_ref[r])
        merged = term if merged is None else merged + term
    h = _dot(merged.astype(BF16), wo_ref[...])
    x1 = _layer_norm(alpha * x + h, g1_ref[...], b1_ref[...])
    x1_ref[...] = x1

    logits = _dot(x1.astype(BF16), wr_ref[...]) + br_ref[...]
    lane = lax.broadcasted_iota(jnp.int32, logits.shape, 1)
    lane_f = lane.astype(F32)
    is_group = lane < N_GROUPS
    gl = jnp.where(is_group, logits, NEG_INF)
    gmax = jnp.max(gl, axis=-1, keepdims=True)
    gsum = jnp.sum(jnp.where(is_group, jnp.exp(gl - gmax), 0.0), axis=-1, keepdims=True)
    g_p = 1.0 / gsum
    g_idx = jnp.min(jnp.where(gl == gmax, lane_f, float(LANES)), axis=-1, keepdims=True)
    lo = N_GROUPS + EXPERTS_PER_GROUP * g_idx
    in_group = (lane_f >= lo) & (lane_f < lo + EXPERTS_PER_GROUP)
    el = jnp.where(in_group, logits, NEG_INF)
    v1 = jnp.max(el, axis=-1, keepdims=True)
    i1 = jnp.min(jnp.where(el == v1, lane_f, float(LANES)), axis=-1, keepdims=True)
    el2 = jnp.where(lane_f == i1, NEG_INF, el)
    v2 = jnp.max(el2, axis=-1, keepdims=True)
    i2 = jnp.min(jnp.where(el2 == v2, lane_f, float(LANES)), axis=-1, keepdims=True)
    t = jnp.exp(v2 - v1)
    w1 = g_p / (1.0 + t)
    w2 = g_p * t / (1.0 + t)
    ids = jnp.where(lane == 0, i1 - N_GROUPS, jnp.where(lane == 1, i2 - N_GROUPS, 0.0))
    ids_ref[...] = ids.astype(jnp.int32)
    wts_ref[...] = jnp.where(lane == 0, w1, jnp.where(lane == 1, w2, 0.0))


def _merge_call(x, ya, yb, yc, w_gates, w_branch, w_out, ln_g, ln_b, w_router, b_router, alpha, tm):
    T, D = x.shape
    row = lambda i: (i, 0)
    const2 = lambda i: (0, 0)
    return pl.pallas_call(
        functools.partial(_merge_kernel, alpha=alpha),
        grid=(T // tm,),
        in_specs=[pl.BlockSpec((tm, D), row),
                  pl.BlockSpec((tm, BRANCH_WIDTH), row),
                  pl.BlockSpec((tm, BRANCH_WIDTH), row),
                  pl.BlockSpec((tm, BRANCH_WIDTH), row),
                  pl.BlockSpec((D, N_BRANCHES * D), const2),
                  pl.BlockSpec((N_BRANCHES, BRANCH_WIDTH, D), lambda i: (0, 0, 0)),
                  pl.BlockSpec((D, D), const2),
                  pl.BlockSpec((1, D), const2),
                  pl.BlockSpec((1, D), const2),
                  pl.BlockSpec((D, LANES), const2),
                  pl.BlockSpec((1, LANES), const2)],
        out_specs=[pl.BlockSpec((tm, D), row),
                   pl.BlockSpec((tm, LANES), row),
                   pl.BlockSpec((tm, LANES), row)],
        out_shape=[jax.ShapeDtypeStruct((T, D), F32),
                   jax.ShapeDtypeStruct((T, LANES), jnp.int32),
                   jax.ShapeDtypeStruct((T, LANES), F32)],
        compiler_params=_params(("parallel",)),
        name="merge_ln1_router",
    )(x, ya, yb, yc, w_gates, w_branch, w_out, ln_g.reshape(1, D), ln_b.reshape(1, D), w_router, b_router)


def _expert_kernel(te_ref, nu_ref, xs_ref, wg_ref, wu_ref, wd_ref, o_ref):
    t = pl.program_id(0)

    @pl.when(t < nu_ref[0])
    def _():
        xb = xs_ref[...].astype(BF16)
        g = _dot(xb, wg_ref[0])
        u = _dot(xb, wu_ref[0])
        h = (g * jax.nn.sigmoid(g)) * u
        o_ref[...] = _dot(h.astype(BF16), wd_ref[0])

    @pl.when(t >= nu_ref[0])
    def _():
        o_ref[...] = jnp.zeros_like(o_ref)


def _expert_call(tile_expert, n_used, xs, w_gate, w_up, w_down):
    P, D = xs.shape
    tm = EXPERT_TILE
    grid_spec = pltpu.PrefetchScalarGridSpec(
        num_scalar_prefetch=2,
        grid=(P // tm,),
        in_specs=[pl.BlockSpec((tm, D), lambda t, te, nu: (t, 0)),
                  pl.BlockSpec((1, D, D_EXPERT), lambda t, te, nu: (te[t], 0, 0)),
                  pl.BlockSpec((1, D, D_EXPERT), lambda t, te, nu: (te[t], 0, 0)),
                  pl.BlockSpec((1, D_EXPERT, D), lambda t, te, nu: (te[t], 0, 0))],
        out_specs=pl.BlockSpec((tm, D), lambda t, te, nu: (t, 0)),
    )
    return pl.pallas_call(
        _expert_kernel,
        grid_spec=grid_spec,
        out_shape=jax.ShapeDtypeStruct((P, D), F32),
        compiler_params=_params(("arbitrary",)),
        name="experts",
    )(tile_expert, n_used, xs, w_gate, w_up, w_down)


def _combine_kernel(x_ref, y0_ref, y1_ref, w_ref, g_ref, b_ref, o_ref, *, alpha):
    w = w_ref[...]
    u = alpha * x_ref[...] + w[:, 0:1] * y0_ref[...] + w[:, 1:2] * y1_ref[...]
    o_ref[...] = _layer_norm(u, g_ref[...], b_ref[...])


def _combine_call(x1, y0, y1, wts, ln_g, ln_b, alpha, tm):
    T, D = x1.shape
    row = lambda i: (i, 0)
    const2 = lambda i: (0, 0)
    return pl.pallas_call(
        functools.partial(_combine_kernel, alpha=alpha),
        grid=(T // tm,),
        in_specs=[pl.BlockSpec((tm, D), row), pl.BlockSpec((tm, D), row), pl.BlockSpec((tm, D), row),
                  pl.BlockSpec((tm, LANES), row),
                  pl.BlockSpec((1, D), const2), pl.BlockSpec((1, D), const2)],
        out_specs=pl.BlockSpec((tm, D), row),
        out_shape=jax.ShapeDtypeStruct((T, D), F32),
        compiler_params=_params(("parallel",)),
        name="combine_ln2",
    )(x1, y0, y1, wts, ln_g.reshape(1, D), ln_b.reshape(1, D))


def _rope_tables(positions):
    half = ROPE_DIM // 2
    inv_freq = ROPE_THETA ** (-jnp.arange(0, ROPE_DIM, 2, dtype=F32) / ROPE_DIM)
    ang = positions.astype(F32).reshape(-1, 1) * inv_freq[None, :]
    cos, sin = jnp.cos(ang), jnp.sin(ang)
    lane = jnp.arange(LANES)
    in_head = lane % DIFF_DH
    freq = in_head % half
    first = in_head < half
    second = (in_head >= half) & (in_head < ROPE_DIM)
    cosf = jnp.where((first | second)[None, :], cos[:, freq], 1.0)
    sin_a = jnp.where(second[None, :], sin[:, freq], 0.0)
    sin_b = jnp.where(first[None, :], -sin[:, freq], 0.0)
    return cosf, sin_a, sin_b


def _dispatch_plan(ids, T):
    tm = EXPERT_TILE
    flat = ids.reshape(-1)
    onehot = (flat[:, None] == jnp.arange(N_EXPERTS, dtype=jnp.int32)[None, :]).astype(jnp.int32)
    csum = jnp.cumsum(onehot, axis=0)
    counts = csum[-1]
    rank = jnp.take_along_axis(csum, flat[:, None], axis=1)[:, 0] - 1
    padded = ((counts + tm - 1) // tm) * tm
    starts = jnp.cumsum(padded) - padded
    pos = starts[flat] + rank
    n_rows = 2 * T + N_EXPERTS * tm
    n_tiles = n_rows // tm
    tile_start = jnp.arange(n_tiles, dtype=jnp.int32) * tm
    ends = jnp.cumsum(padded)
    tile_expert = jnp.minimum(jnp.sum((tile_start[:, None] >= ends[None, :]).astype(jnp.int32), axis=1),
                              N_EXPERTS - 1).astype(jnp.int32)
    n_used = (ends[-1] // tm).astype(jnp.int32).reshape(1)
    return pos, tile_expert, n_used, n_rows


def _layer(x, cosf, sina, sinb, B, S, lb, p, alpha, lam_init):
    T = B * S
    tm = min(512, T)
    w_in = p["w_in"]
    w_main = jnp.concatenate([w_in[:, :3584], w_in[:, 3592:5128]], axis=1).astype(BF16)
    w_ff = jnp.zeros((D_MODEL, LANES), BF16).at[:, :FOX_HEADS].set(w_in[:, 3584:3592].astype(BF16))
    w_gates = w_in[:, 5128:].astype(BF16)

    proj, ff = _in_proj_call(x, w_main, w_ff, cosf, sina, sinb, tm)
    ya = _hgrn_call(proj, lb, p["hgrn_norm_g"], B, S)
    qa, ka = _fox_gate_call(ff, p["fox_f_bias"], B, S)
    yb = _fox_call(proj, qa, ka, B, S)
    lv = p["diff_lambda"].astype(F32)
    lam = jnp.exp(jnp.sum(lv[0] * lv[1])) - jnp.exp(jnp.sum(lv[2] * lv[3])) + lam_init
    lam_row = jnp.full((1, LANES), lam, F32)
    norm_row = (p["diff_norm_g"].astype(F32) * (1.0 - lam_init)).reshape(1, DIFF_DV)
    yc = _diff_call(proj, lam_row, norm_row, B, S)

    w_router = jnp.zeros((D_MODEL, LANES), F32)
    w_router = w_router.at[:, :N_GROUPS].set(p["router_g_w"]).at[:, N_GROUPS:N_GROUPS + N_EXPERTS].set(p["router_e_w"])
    b_router = jnp.zeros((1, LANES), F32)
    b_router = b_router.at[0, :N_GROUPS].set(p["router_g_b"]).at[0, N_GROUPS:N_GROUPS + N_EXPERTS].set(
        p["router_e_b"].reshape(-1))
    x1, ids, wts = _merge_call(x, ya, yb, yc, w_gates, p["w_branch"].astype(BF16), p["w_out"].astype(BF16),
                               p["ln1_g"], p["ln1_b"], w_router.astype(BF16), b_router, alpha, tm)

    pos, tile_expert, n_used, n_rows = _dispatch_plan(ids[:, :2], T)
    row_token = jnp.zeros((n_rows,), jnp.int32).at[pos].set(jnp.arange(2 * T, dtype=jnp.int32) // 2)
    xs = jnp.take(x1, row_token, axis=0)
    y = _expert_call(tile_expert, n_used, xs, p["expert_w_gate"].astype(BF16), p["expert_w_up"].astype(BF16),
                     p["expert_w_down"].astype(BF16))
    yg = jnp.take(y, pos, axis=0).reshape(T, 2, D_MODEL)
    return _combine_call(x1, yg[:, 0], yg[:, 1], wts, p["ln2_g"], p["ln2_b"], alpha, tm)


def kernel(x, positions, ln_in_g, ln_in_b, w_in, hgrn_lb_logits, hgrn_norm_g, fox_f_bias, diff_lambda,
           diff_norm_g, w_branch, w_out, ln1_g, ln1_b, router_g_w, router_g_b, router_e_w, router_e_b,
           expert_w_gate, expert_w_up, expert_w_down, ln2_g, ln2_b):
    B, S, D = x.shape
    T = B * S
    depth = w_in.shape[0]
    alpha = (2 * depth) ** 0.25
    cosf, sina, sinb = _rope_tables(positions)
    lb_soft = jax.nn.softmax(hgrn_lb_logits.astype(F32), axis=0)
    lower_bounds = jnp.maximum(jnp.cumsum(lb_soft, axis=0) - lb_soft[0], 0.0)

    h = _ln_call(x.reshape(T, D), ln_in_g, ln_in_b, min(512, T))
    for l in range(depth):
        p = dict(w_in=w_in[l], hgrn_norm_g=hgrn_norm_g[l], fox_f_bias=fox_f_bias[l], diff_lambda=diff_lambda[l],
                 diff_norm_g=diff_norm_g[l], w_branch=w_branch[l], w_out=w_out[l], ln1_g=ln1_g[l], ln1_b=ln1_b[l],
                 router_g_w=router_g_w[l], router_g_b=router_g_b[l], router_e_w=router_e_w[l],
                 router_e_b=router_e_b[l], expert_w_gate=expert_w_gate[l], expert_w_up=expert_w_up[l],
                 expert_w_down=expert_w_down[l], ln2_g=ln2_g[l], ln2_b=ln2_b[l])
        lam_init = 0.8 - 0.6 * float(math.exp(-0.3 * l))
        h = _layer(h, cosf, sina, sinb, B, S, lower_bounds[l], p, alpha, lam_init)
    return h.reshape(B, S, D)
```

```python
import functools
import math

import numpy as np
import jax
import jax.numpy as jnp
from jax import lax
from jax.experimental import pallas as pl
from jax.experimental.pallas import tpu as pltpu

F32 = jnp.float32
BF16 = jnp.bfloat16

D_MODEL = 1024
HG_HEADS, HG_D = 4, 128
FOX_HEADS, FOX_DH = 8, 64
DIFF_HEADS, DIFF_DH, DIFF_DV = 4, 64, 128
BRANCH_WIDTH = 512
N_BRANCHES = 3
ROPE_THETA = 500000.0
ROPE_DIM = DIFF_DH // 4
N_GROUPS, EXPERTS_PER_GROUP = 4, 8
N_EXPERTS = N_GROUPS * EXPERTS_PER_GROUP
D_EXPERT = 512
LN_EPS = 1e-5
RMS_EPS = 1e-6
NEG_INF = -1e30
EXP_CLAMP = 60.0
LOG2E = 1.4426950408889634

LANES = 128
N_MAIN = 5120
COL_HQ, COL_HF, COL_HI, COL_HG = 0, 4, 8, 12
COL_FQ, COL_FK, COL_FV = 16, 20, 24
COL_DQ, COL_DK, COL_DV = 28, 32, 36
ROPE_TILES = (7, 8)

HG_CHUNK = 64
HG_SUB = 8
GATE_CHUNK = 256
ATTN_BLOCK = 512
EXPERT_TILE = 256
VMEM_LIMIT = 56 * 1024 * 1024


def _dot(a, b):
    return jnp.dot(a, b, preferred_element_type=F32)


def _dot_nt(a, b):
    return lax.dot_general(a, b, (((1,), (1,)), ((), ())), preferred_element_type=F32)


def _log_sigmoid(z):
    return jnp.minimum(z, 0.0) - jnp.log1p(jnp.exp(-jnp.abs(z)))


def _split3(x):
    h1 = x.astype(BF16)
    r1 = x - h1.astype(F32)
    h2 = r1.astype(BF16)
    h3 = (r1 - h2.astype(F32)).astype(BF16)
    return h1, h2, h3


def _cumsum_rows(tri, x):
    h1, h2, h3 = _split3(x)
    return _dot(tri, h1) + _dot(tri, h2) + _dot(tri, h3)


def _layer_norm(u, g, b):
    mu = jnp.mean(u, axis=-1, keepdims=True)
    d = u - mu
    var = jnp.mean(d * d, axis=-1, keepdims=True)
    return d * lax.rsqrt(var + LN_EPS) * g + b


def _params(sem):
    return pltpu.CompilerParams(dimension_semantics=sem, vmem_limit_bytes=VMEM_LIMIT)


def _ln_kernel(x_ref, g_ref, b_ref, o_ref):
    o_ref[...] = _layer_norm(x_ref[...], g_ref[...], b_ref[...])


def _ln_call(x, g, b, tm):
    T, D = x.shape
    return pl.pallas_call(
        _ln_kernel,
        grid=(T // tm,),
        in_specs=[pl.BlockSpec((tm, D), lambda i: (i, 0)),
                  pl.BlockSpec((1, D), lambda i: (0, 0)),
                  pl.BlockSpec((1, D), lambda i: (0, 0))],
        out_specs=pl.BlockSpec((tm, D), lambda i: (i, 0)),
        out_shape=jax.ShapeDtypeStruct((T, D), F32),
        compiler_params=_params(("parallel",)),
        name="ln_in",
    )(x, g.reshape(1, D), b.reshape(1, D))


def _in_proj_kernel(x_ref, w_ref, wff_ref, cos_ref, sa_ref, sb_ref, o_ref, ff_ref):
    xb = x_ref[...].astype(BF16)
    ff_ref[...] = _dot(xb, wff_ref[...])
    for j in range(N_MAIN // 512):
        acc = _dot(xb, w_ref[:, j * 512:(j + 1) * 512])
        if j in ROPE_TILES:
            cosf, sa, sb = cos_ref[...], sa_ref[...], sb_ref[...]
            for g in range(4):
                t = acc[:, g * LANES:(g + 1) * LANES]
                r = t * cosf + pltpu.roll(t, 8, 1) * sa + pltpu.roll(t, LANES - 8, 1) * sb
                o_ref[:, j * 512 + g * LANES:j * 512 + (g + 1) * LANES] = r.astype(BF16)
        else:
            o_ref[:, j * 512:(j + 1) * 512] = acc.astype(BF16)


def _in_proj_call(x, w_main, w_ff, cosf, sina, sinb, tm):
    T, D = x.shape
    const = lambda i: (0, 0)
    row = lambda i: (i, 0)
    return pl.pallas_call(
        _in_proj_kernel,
        grid=(T // tm,),
        in_specs=[pl.BlockSpec((tm, D), row),
                  pl.BlockSpec((D, N_MAIN), const),
                  pl.BlockSpec((D, LANES), const),
                  pl.BlockSpec((tm, LANES), row),
                  pl.BlockSpec((tm, LANES), row),
                  pl.BlockSpec((tm, LANES), row)],
        out_specs=[pl.BlockSpec((tm, N_MAIN), row),
                   pl.BlockSpec((tm, LANES), row)],
        out_shape=[jax.ShapeDtypeStruct((T, N_MAIN), BF16),
                   jax.ShapeDtypeStruct((T, LANES), F32)],
        compiler_params=_params(("parallel",)),
        name="in_proj",
    )(x, w_main, w_ff, cosf, sina, sinb)


def _hgrn_chunk(z, ql, v, gl, lb, ng, state_t, b_scr, k_scr, v_scr, consts):
    C = HG_CHUNK
    tri, levels, sub_rows = consts
    log_f = _log_sigmoid(z) + jnp.log1p(lb * jnp.exp(jnp.minimum(-z, EXP_CLAMP)))
    k = (1.0 - lb) * jax.nn.sigmoid(-z)
    q = ql * jax.nn.sigmoid(ql)
    b = _cumsum_rows(tri, log_f)
    b_scr[...] = b
    k_scr[...] = k
    v_scr[...] = v
    vb = v.astype(BF16)
    b_last = b_scr[C - 1:C, :]

    o = _dot_nt((q * jnp.exp(b)).astype(BF16), state_t.astype(BF16))

    scores = jnp.zeros((C, C), F32)
    for m, is_query, pair in levels:
        pieces = [jnp.broadcast_to(b_scr[p * 2 * m + m - 1:p * 2 * m + m, :], (2 * m, LANES))
                  for p in range(C // (2 * m))]
        b_ref_rows = pieces[0] if len(pieces) == 1 else jnp.concatenate(pieces, axis=0)
        decay = jnp.exp(-jnp.abs(b - b_ref_rows))
        qd = jnp.where(is_query, q * decay, 0.0).astype(BF16)
        kd = jnp.where(is_query, 0.0, k * decay).astype(BF16)
        scores = scores + jnp.where(pair, _dot_nt(qd, kd), 0.0)
    o = o + _dot(scores.astype(BF16), vb)

    diag = []
    for blk in range(C // HG_SUB):
        lo = blk * HG_SUB
        bb = b[lo:lo + HG_SUB]
        qq = q[lo:lo + HG_SUB]
        acc = jnp.zeros((HG_SUB, LANES), F32)
        for s in range(HG_SUB):
            bs = b_scr[lo + s:lo + s + 1, :]
            ks = k_scr[lo + s:lo + s + 1, :]
            vs = v_scr[lo + s:lo + s + 1, :]
            w = qq * ks * jnp.exp(jnp.minimum(bb - bs, 0.0))
            a_col = jnp.sum(w, axis=-1, keepdims=True)
            acc = acc + jnp.where(sub_rows >= s, a_col, 0.0) * vs
        diag.append(acc)
    o = o + jnp.concatenate(diag, axis=0)

    k_dec = (k * jnp.exp(b_last - b)).astype(BF16)
    state_t = state_t * jnp.exp(b_last) + _dot(v.T.astype(BF16), k_dec)

    ms = jnp.mean(o * o, axis=-1, keepdims=True)
    y = o * lax.rsqrt(ms + RMS_EPS) * ng * (gl * jax.nn.sigmoid(gl))
    return y, state_t


def _hgrn_kernel(q_ref, f_ref, i_ref, g_ref, lb_ref, ng_ref, o_ref, b_scr, k_scr, v_scr, *, nchunks):
    C = HG_CHUNK
    ng = ng_ref[...]
    ri = lax.broadcasted_iota(jnp.int32, (C, C), 0)
    ci = lax.broadcasted_iota(jnp.int32, (C, C), 1)
    tri = jnp.where(ri >= ci, 1.0, 0.0).astype(BF16)
    rows = lax.broadcasted_iota(jnp.int32, (C, LANES), 0)
    sub_rows = lax.broadcasted_iota(jnp.int32, (HG_SUB, LANES), 0)
    levels = []
    m = HG_SUB
    while m < C:
        shift = int(math.log2(2 * m))
        is_query = (rows & (2 * m - 1)) >= m
        pair = ((ri >> shift) == (ci >> shift)) & ((ri & (2 * m - 1)) >= m) & ((ci & (2 * m - 1)) < m)
        levels.append((m, is_query, pair))
        m *= 2
    consts = (tri, levels, sub_rows)

    def body(n, states):
        r0 = pl.multiple_of(n * C, C)
        new_states = []
        for h in range(HG_HEADS):
            cols = slice(h * LANES, (h + 1) * LANES)
            y, st = _hgrn_chunk(f_ref[pl.ds(r0, C), cols].astype(F32), q_ref[pl.ds(r0, C), cols].astype(F32),
                                i_ref[pl.ds(r0, C), cols].astype(F32), g_ref[pl.ds(r0, C), cols].astype(F32),
                                lb_ref[:, cols], ng, states[h], b_scr.at[h], k_scr.at[h], v_scr.at[h], consts)
            o_ref[pl.ds(r0, C), cols] = y.astype(o_ref.dtype)
            new_states.append(st)
        return tuple(new_states)

    lax.fori_loop(0, nchunks, body, tuple(jnp.zeros((HG_D, HG_D), F32) for _ in range(HG_HEADS)))


def _hgrn_call(proj, lb, norm_g, B, S):
    T = B * S
    W = HG_HEADS * HG_D
    blk = lambda off: pl.BlockSpec((S, W), lambda b, off=off: (b, off))
    return pl.pallas_call(
        functools.partial(_hgrn_kernel, nchunks=S // HG_CHUNK),
        grid=(B,),
        in_specs=[blk(COL_HQ // 4), blk(COL_HF // 4), blk(COL_HI // 4), blk(COL_HG // 4),
                  pl.BlockSpec((1, W), lambda b: (0, 0)),
                  pl.BlockSpec((1, LANES), lambda b: (0, 0))],
        out_specs=pl.BlockSpec((S, W), lambda b: (b, 0)),
        out_shape=jax.ShapeDtypeStruct((T, BRANCH_WIDTH), BF16),
        scratch_shapes=[pltpu.VMEM((HG_HEADS, HG_CHUNK, LANES), F32)] * 3,
        compiler_params=_params(("parallel",)),
        name="hgrn",
    )(proj, proj, proj, proj, lb.reshape(1, W), norm_g.reshape(1, HG_D))


def _fox_aug_tables():
    sel_q = np.zeros((3, LANES, BRANCH_WIDTH), np.float32)
    sel_k = np.zeros((3, LANES, BRANCH_WIDTH), np.float32)
    ones_q = np.zeros((1, BRANCH_WIDTH), np.float32)
    ones_k = np.zeros((1, BRANCH_WIDTH), np.float32)
    for col in range(BRANCH_WIDTH):
        pair, within = divmod(col, LANES)
        half, slot = divmod(within, FOX_DH)
        head = 2 * pair + 1 - half
        if slot < 3:
            sel_q[slot, head, col] = 1.0
            ones_k[0, col] = 1.0
        elif slot < 6:
            sel_k[slot - 3, head, col] = -1.0
            ones_q[0, col] = 1.0
    return sel_q, sel_k, ones_q, ones_k


def _fox_gate_kernel(ff_ref, bias_ref, selq_ref, selk_ref, oq_ref, ok_ref, qa_ref, ka_ref, *, nchunks):
    CH = GATE_CHUNK
    ri = lax.broadcasted_iota(jnp.int32, (CH, CH), 0)
    ci = lax.broadcasted_iota(jnp.int32, (CH, CH), 1)
    tri = jnp.where(ri >= ci, 1.0, 0.0).astype(BF16)
    bias = bias_ref[...]

    def body(n, carry):
        r0 = pl.multiple_of(n * CH, CH)
        c = carry + _cumsum_rows(tri, _log_sigmoid(ff_ref[pl.ds(r0, CH), :] + bias))
        parts = _split3(c * LOG2E)
        qa = oq_ref[...] + _dot(parts[0], selq_ref[0]) + _dot(parts[1], selq_ref[1]) + _dot(parts[2], selq_ref[2])
        ka = ok_ref[...] + _dot(parts[0], selk_ref[0]) + _dot(parts[1], selk_ref[1]) + _dot(parts[2], selk_ref[2])
        qa_ref[pl.ds(r0, CH), :] = qa.astype(BF16)
        ka_ref[pl.ds(r0, CH), :] = ka.astype(BF16)
        return c[CH - 1:CH, :]

    lax.fori_loop(0, nchunks, body, jnp.zeros((1, LANES), F32))


def _fox_gate_call(ff, bias, B, S):
    T = B * S
    bias_row = jnp.zeros((1, LANES), F32).at[0, :FOX_HEADS].set(bias.astype(F32))
    sel_q, sel_k, ones_q, ones_k = _fox_aug_tables()
    const2 = lambda b: (0, 0)
    const3 = lambda b: (0, 0, 0)
    return pl.pallas_call(
        functools.partial(_fox_gate_kernel, nchunks=S // GATE_CHUNK),
        grid=(B,),
        in_specs=[pl.BlockSpec((S, LANES), lambda b: (b, 0)),
                  pl.BlockSpec((1, LANES), const2),
                  pl.BlockSpec((3, LANES, BRANCH_WIDTH), const3),
                  pl.BlockSpec((3, LANES, BRANCH_WIDTH), const3),
                  pl.BlockSpec((1, BRANCH_WIDTH), const2),
                  pl.BlockSpec((1, BRANCH_WIDTH), const2)],
        out_specs=[pl.BlockSpec((S, BRANCH_WIDTH), lambda b: (b, 0)),
                   pl.BlockSpec((S, BRANCH_WIDTH), lambda b: (b, 0))],
        out_shape=[jax.ShapeDtypeStruct((T, BRANCH_WIDTH), BF16),
                   jax.ShapeDtypeStruct((T, BRANCH_WIDTH), BF16)],
        compiler_params=_params(("parallel",)),
        name="fox_gate",
    )(ff, bias_row, jnp.asarray(sel_q, BF16), jnp.asarray(sel_k, BF16), jnp.asarray(ones_q), jnp.asarray(ones_k))


def _flash_t(qs, k_fns, vt_fns, n_full):
    tq = tk = ATTN_BLOCK
    krow = lax.broadcasted_iota(jnp.int32, (tk, tq), 0)
    qcol = lax.broadcasted_iota(jnp.int32, (tk, tq), 1)

    def step(j, carry, causal):
        k0 = pl.multiple_of(j * tk, tk)
        out = []
        for qa, k_fn, vt_fn, (m, l, acc) in zip(qs, k_fns, vt_fns, carry):
            s = _dot_nt(k_fn(k0), qa)
            if causal:
                s = jnp.where(krow <= qcol, s, NEG_INF)
            m_new = jnp.maximum(m, jnp.max(s, axis=0, keepdims=True))
            alpha = jnp.exp2(m - m_new)
            p = jnp.exp2(s - m_new)
            l = alpha * l + jnp.sum(p, axis=0, keepdims=True)
            acc = alpha * acc + _dot(vt_fn(k0), p.astype(BF16))
            out.append((m_new, l, acc))
        return tuple(out)

    init = []
    for vt_fn in vt_fns:
        dv = vt_fn(0).shape[0]
        init.append((jnp.full((1, tq), NEG_INF, F32), jnp.zeros((1, tq), F32), jnp.zeros((dv, tq), F32)))
    carry = lax.fori_loop(0, n_full, lambda j, c: step(j, c, False), tuple(init))
    carry = step(n_full, carry, True)
    return [(acc, l) for (_, l, acc) in carry]


def _fill_vt(v_ref, vt_scr):
    S = v_ref.shape[0]
    for c in range(S // ATTN_BLOCK):
        sl = slice(c * ATTN_BLOCK, (c + 1) * ATTN_BLOCK)
        vt_scr[:, sl] = v_ref[sl, :].astype(F32).T.astype(BF16)


def _fox_kernel(q_ref, qa_ref, k_ref, ka_ref, v_ref, o_ref, vt_scr):
    qi = pl.program_id(2)
    tk = ATTN_BLOCK

    @pl.when(qi == 0)
    def _():
        _fill_vt(v_ref, vt_scr)

    lo = lax.broadcasted_iota(jnp.int32, (1, LANES), 1) < FOX_DH
    qf = q_ref[...].astype(F32) * (FOX_DH ** -0.5 * LOG2E)
    qa = qa_ref[...].astype(F32)
    qs = [jnp.where(lo, qf, qa).astype(BF16), jnp.where(lo, qa, qf).astype(BF16)]
    k_fns = [lambda k0: jnp.where(lo, k_ref[pl.ds(k0, tk), :], ka_ref[pl.ds(k0, tk), :]),
             lambda k0: jnp.where(lo, ka_ref[pl.ds(k0, tk), :], k_ref[pl.ds(k0, tk), :])]
    vt_fns = [lambda k0: vt_scr[0:FOX_DH, pl.ds(k0, tk)],
              lambda k0: vt_scr[FOX_DH:2 * FOX_DH, pl.ds(k0, tk)]]
    (acc0, l0), (acc1, l1) = _flash_t(qs, k_fns, vt_fns, qi)
    o_t = jnp.concatenate([acc0 / l0, acc1 / l1], axis=0)
    o_ref[...] = o_t.T.astype(o_ref.dtype)


def _fox_call(proj, qa, ka, B, S):
    T = B * S
    tq = ATTN_BLOCK
    nq = S // tq
    return pl.pallas_call(
        _fox_kernel,
        grid=(B, FOX_HEADS // 2, nq),
        in_specs=[pl.BlockSpec((tq, LANES), lambda b, p, i: (b * nq + i, COL_FQ + p)),
                  pl.BlockSpec((tq, LANES), lambda b, p, i: (b * nq + i, p)),
                  pl.BlockSpec((S, LANES), lambda b, p, i: (b, COL_FK + p)),
                  pl.BlockSpec((S, LANES), lambda b, p, i: (b, p)),
                  pl.BlockSpec((S, LANES), lambda b, p, i: (b, COL_FV + p))],
        out_specs=pl.BlockSpec((tq, LANES), lambda b, p, i: (b * nq + i, p)),
        out_shape=jax.ShapeDtypeStruct((T, BRANCH_WIDTH), BF16),
        scratch_shapes=[pltpu.VMEM((LANES, S), BF16)],
        compiler_params=_params(("parallel", "parallel", "arbitrary")),
        name="fox_attn",
    )(proj, qa, proj, ka, proj)


def _diff_kernel(q_ref, k_ref, v_ref, lam_ref, ng_ref, o_ref, vt_scr):
    qi = pl.program_id(2)
    tk = ATTN_BLOCK

    @pl.when(qi == 0)
    def _():
        _fill_vt(v_ref, vt_scr)

    lo = lax.broadcasted_iota(jnp.int32, (1, LANES), 1) < DIFF_DH
    qf = q_ref[...].astype(F32) * (DIFF_DH ** -0.5 * LOG2E)
    qs = [jnp.where(lo, qf, 0.0).astype(BF16), jnp.where(lo, 0.0, qf).astype(BF16)]
    k_fns = [lambda k0: k_ref[pl.ds(k0, tk), :]] * 2
    vt_fns = [lambda k0: vt_scr[:, pl.ds(k0, tk)]] * 2
    (acc0, l0), (acc1, l1) = _flash_t(qs, k_fns, vt_fns, qi)
    o = (acc0 / l0 - lam_ref[0:1, 0:1] * (acc1 / l1)).T
    ms = jnp.mean(o * o, axis=-1, keepdims=True)
    o_ref[...] = (o * lax.rsqrt(ms + RMS_EPS) * ng_ref[...]).astype(o_ref.dtype)


def _diff_call(proj, lam_row, norm_row, B, S):
    T = B * S
    tq = ATTN_BLOCK
    nq = S // tq
    return pl.pallas_call(
        _diff_kernel,
        grid=(B, DIFF_HEADS, nq),
        in_specs=[pl.BlockSpec((tq, LANES), lambda b, h, i: (b * nq + i, COL_DQ + h)),
                  pl.BlockSpec((S, LANES), lambda b, h, i: (b, COL_DK + h)),
                  pl.BlockSpec((S, LANES), lambda b, h, i: (b, COL_DV + h)),
                  pl.BlockSpec((1, LANES), lambda b, h, i: (0, 0)),
                  pl.BlockSpec((1, LANES), lambda b, h, i: (0, 0))],
        out_specs=pl.BlockSpec((tq, LANES), lambda b, h, i: (b * nq + i, h)),
        out_shape=jax.ShapeDtypeStruct((T, BRANCH_WIDTH), BF16),
        scratch_shapes=[pltpu.VMEM((LANES, S), BF16)],
        compiler_params=_params(("parallel", "parallel", "arbitrary")),
        name="diff_attn",
    )(proj, proj, proj, lam_row, norm_row)


def _merge_kernel(x_ref, ya_ref, yb_ref, yc_ref, wg_ref, wb_ref, wo_ref, g1_ref, b1_ref, wr_ref, br_ref,
                  x1_ref, ids_ref, wts_ref, *, alpha):
    x = x_ref[...]
    xb = x.astype(BF16)
    merged = None
    for r, y_ref in enumerate((ya_ref, yb_ref, yc_ref)):
        gate = jax.nn.sigmoid(_dot(xb, wg_ref[:, r * D_MODEL:(r + 1) * D_MODEL]))
        term = gate * _dot(y_ref[...], wb_ref[r])
        merged = term if merged is None else merged + term
    h = _dot(merged.astype(BF16), wo_ref[...])
    x1 = _layer_norm(alpha * x + h, g1_ref[...], b1_ref[...])
    x1_ref[...] = x1

    logits = _dot(x1.astype(BF16), wr_ref[...]) + br_ref[...]
    lane = lax.broadcasted_iota(jnp.int32, logits.shape, 1)
    lane_f = lane.astype(F32)
    is_group = lane < N_GROUPS
    gl = jnp.where(is_group, logits, NEG_INF)
    gmax = jnp.max(gl, axis=-1, keepdims=True)
    gsum = jnp.sum(jnp.where(is_group, jnp.exp(gl - gmax), 0.0), axis=-1, keepdims=True)
    g_p = 1.0 / gsum
    g_idx = jnp.min(jnp.where(gl == gmax, lane_f, float(LANES)), axis=-1, keepdims=True)
    lo = N_GROUPS + EXPERTS_PER_GROUP * g_idx
    in_group = (lane_f >= lo) & (lane_f < lo + EXPERTS_PER_GROUP)
    el = jnp.where(in_group, logits, NEG_INF)
    v1 = jnp.max(el, axis=-1, keepdims=True)
    i1 = jnp.min(jnp.where(el == v1, lane_f, float(LANES)), axis=-1, keepdims=True)
    el2 = jnp.where(lane_f == i1, NEG_INF, el)
    v2 = jnp.max(el2, axis=-1, keepdims=True)
    i2 = jnp.min(jnp.where(el2 == v2, lane_f, float(LANES)), axis=-1, keepdims=True)
    t = jnp.exp(v2 - v1)
    w1 = g_p / (1.0 + t)
    w2 = g_p * t / (1.0 + t)
    ids = jnp.where(lane == 0, i1 - N_GROUPS, jnp.where(lane == 1, i2 - N_GROUPS, 0.0))
    ids_ref[...] = ids.astype(jnp.int32)
    wts_ref[...] = jnp.where(lane == 0, w1, jnp.where(lane == 1, w2, 0.0))


def _merge_call(x, ya, yb, yc, w_gates, w_branch, w_out, ln_g, ln_b, w_router, b_router, alpha, tm):
    T, D = x.shape
    row = lambda i: (i, 0)
    const2 = lambda i: (0, 0)
    return pl.pallas_call(
        functools.partial(_merge_kernel, alpha=alpha),
        grid=(T // tm,),
        in_specs=[pl.BlockSpec((tm, D), row),
                  pl.BlockSpec((tm, BRANCH_WIDTH), row),
                  pl.BlockSpec((tm, BRANCH_WIDTH), row),
                  pl.BlockSpec((tm, BRANCH_WIDTH), row),
                  pl.BlockSpec((D, N_BRANCHES * D), const2),
                  pl.BlockSpec((N_BRANCHES, BRANCH_WIDTH, D), lambda i: (0, 0, 0)),
                  pl.BlockSpec((D, D), const2),
                  pl.BlockSpec((1, D), const2),
                  pl.BlockSpec((1, D), const2),
                  pl.BlockSpec((D, LANES), const2),
                  pl.BlockSpec((1, LANES), const2)],
        out_specs=[pl.BlockSpec((tm, D), row),
                   pl.BlockSpec((tm, LANES), row),
                   pl.BlockSpec((tm, LANES), row)],
        out_shape=[jax.ShapeDtypeStruct((T, D), F32),
                   jax.ShapeDtypeStruct((T, LANES), jnp.int32),
                   jax.ShapeDtypeStruct((T, LANES), F32)],
        compiler_params=_params(("parallel",)),
        name="merge_ln1_router",
    )(x, ya, yb, yc, w_gates, w_branch, w_out, ln_g.reshape(1, D), ln_b.reshape(1, D), w_router, b_router)


def _row_copy(src, src_row, dst, dst_row, sem):
    return pltpu.make_async_copy(src.at[pl.ds(src_row, 1), :], dst.at[pl.ds(dst_row, 1), :], sem)


def _dispatch_kernel(pos_ref, pad_ref, x_ref, xs_hbm, zero_scr, sem, *, tm, npad, pad_steps):
    def issue(r, carry):
        _row_copy(x_ref, r, xs_hbm, pos_ref[2 * r], sem).start()
        _row_copy(x_ref, r, xs_hbm, pos_ref[2 * r + 1], sem).start()
        return carry

    lax.fori_loop(0, tm, issue, 0, unroll=8)

    @pl.when(pl.program_id(0) < pad_steps)
    def _():
        zero_scr[...] = jnp.zeros_like(zero_scr)

        def issue_pad(r, carry):
            _row_copy(zero_scr, r, xs_hbm, pad_ref[r], sem).start()
            return carry

        lax.fori_loop(0, npad, issue_pad, 0, unroll=8)
        pltpu.make_async_copy(zero_scr, xs_hbm.at[pl.ds(0, npad), :], sem).wait()

    for _ in range(2):
        pltpu.make_async_copy(x_ref, xs_hbm.at[pl.ds(0, tm), :], sem).wait()


def _dispatch_call(pos, pad_rows, x1, n_rows, tm):
    T, D = x1.shape
    nsteps = T // tm
    npad = max(LANES, pad_rows.shape[0] // nsteps)
    pad_steps = pad_rows.shape[0] // npad
    return pl.pallas_call(
        functools.partial(_dispatch_kernel, tm=tm, npad=npad, pad_steps=pad_steps),
        grid=(nsteps,),
        in_specs=[pl.BlockSpec((2 * tm,), lambda i: (i,), memory_space=pltpu.SMEM),
                  pl.BlockSpec((npad,), lambda i: (jnp.minimum(i, pad_steps - 1),), memory_space=pltpu.SMEM),
                  pl.BlockSpec((tm, D), lambda i: (i, 0))],
        out_specs=pl.BlockSpec(memory_space=pl.ANY),
        out_shape=jax.ShapeDtypeStruct((n_rows, D), F32),
        scratch_shapes=[pltpu.VMEM((npad, D), F32), pltpu.SemaphoreType.DMA(())],
        compiler_params=_params(("arbitrary",)),
        name="moe_dispatch",
    )(pos, pad_rows, x1)


def _expert_kernel(te_ref, nu_ref, xs_ref, wg_ref, wu_ref, wd_ref, o_ref):
    t = pl.program_id(0)

    @pl.when(t < nu_ref[0])
    def _():
        xb = xs_ref[...].astype(BF16)
        g = _dot(xb, wg_ref[0])
        u = _dot(xb, wu_ref[0])
        h = (g * jax.nn.sigmoid(g)) * u
        o_ref[...] = _dot(h.astype(BF16), wd_ref[0])

    @pl.when(t >= nu_ref[0])
    def _():
        o_ref[...] = jnp.zeros_like(o_ref)


def _expert_call(tile_expert, n_used, xs, n_tiles, w_gate, w_up, w_down):
    D = xs.shape[1]
    tm = EXPERT_TILE
    used = lambda t, te, nu: (jnp.minimum(t, nu[0] - 1), 0)
    grid_spec = pltpu.PrefetchScalarGridSpec(
        num_scalar_prefetch=2,
        grid=(n_tiles,),
        in_specs=[pl.BlockSpec((tm, D), used),
                  pl.BlockSpec((1, D, D_EXPERT), lambda t, te, nu: (te[t], 0, 0)),
                  pl.BlockSpec((1, D, D_EXPERT), lambda t, te, nu: (te[t], 0, 0)),
                  pl.BlockSpec((1, D_EXPERT, D), lambda t, te, nu: (te[t], 0, 0))],
        out_specs=pl.BlockSpec((tm, D), lambda t, te, nu: (t, 0)),
    )
    return pl.pallas_call(
        _expert_kernel,
        grid_spec=grid_spec,
        out_shape=jax.ShapeDtypeStruct((n_tiles * tm, D), F32),
        compiler_params=_params(("arbitrary",)),
        name="experts",
    )(tile_expert, n_used, xs, w_gate, w_up, w_down)


def _combine_kernel(pos_ref, x_ref, w_ref, g_ref, b_ref, y_hbm, o_ref, ybuf, sem, *, alpha, tm):
    def issue(r, carry):
        _row_copy(y_hbm, pos_ref[2 * r], ybuf.at[0], r, sem).start()
        _row_copy(y_hbm, pos_ref[2 * r + 1], ybuf.at[1], r, sem).start()
        return carry

    lax.fori_loop(0, tm, issue, 0, unroll=8)
    for k in range(2):
        pltpu.make_async_copy(y_hbm.at[pl.ds(0, tm), :], ybuf.at[k], sem).wait()
    w = w_ref[...]
    u = alpha * x_ref[...] + w[:, 0:1] * ybuf[0] + w[:, 1:2] * ybuf[1]
    o_ref[...] = _layer_norm(u, g_ref[...], b_ref[...])


def _combine_call(pos, x1, wts, ln_g, ln_b, y, alpha, tm):
    T, D = x1.shape
    row = lambda i: (i, 0)
    const2 = lambda i: (0, 0)
    return pl.pallas_call(
        functools.partial(_combine_kernel, alpha=alpha, tm=tm),
        grid=(T // tm,),
        in_specs=[pl.BlockSpec((2 * tm,), lambda i: (i,), memory_space=pltpu.SMEM),
                  pl.BlockSpec((tm, D), row),
                  pl.BlockSpec((tm, LANES), row),
                  pl.BlockSpec((1, D), const2), pl.BlockSpec((1, D), const2),
                  pl.BlockSpec(memory_space=pl.ANY)],
        out_specs=pl.BlockSpec((tm, D), row),
        out_shape=jax.ShapeDtypeStruct((T, D), F32),
        scratch_shapes=[pltpu.VMEM((2, tm, D), F32), pltpu.SemaphoreType.DMA(())],
        compiler_params=_params(("arbitrary",)),
        name="combine_ln2",
    )(pos, x1, wts, ln_g.reshape(1, D), ln_b.reshape(1, D), y)


def _rope_tables(positions):
    half = ROPE_DIM // 2
    inv_freq = ROPE_THETA ** (-jnp.arange(0, ROPE_DIM, 2, dtype=F32) / ROPE_DIM)
    ang = positions.astype(F32).reshape(-1, 1) * inv_freq[None, :]
    cos, sin = jnp.cos(ang), jnp.sin(ang)
    lane = jnp.arange(LANES)
    in_head = lane % DIFF_DH
    freq = in_head % half
    first = in_head < half
    second = (in_head >= half) & (in_head < ROPE_DIM)
    cosf = jnp.where((first | second)[None, :], cos[:, freq], 1.0)
    sin_a = jnp.where(second[None, :], sin[:, freq], 0.0)
    sin_b = jnp.where(first[None, :], -sin[:, freq], 0.0)
    return cosf, sin_a, sin_b


def _dispatch_plan(ids, T):
    tm = EXPERT_TILE
    flat = ids.reshape(-1)
    onehot = (flat[:, None] == jnp.arange(N_EXPERTS, dtype=jnp.int32)[None, :]).astype(jnp.int32)
    csum = jnp.cumsum(onehot, axis=0)
    counts = csum[-1]
    rank = jnp.take_along_axis(csum, flat[:, None], axis=1)[:, 0] - 1
    padded = ((counts + tm - 1) // tm) * tm
    ends = jnp.cumsum(padded)
    starts = ends - padded
    pos = (starts[flat] + rank).astype(jnp.int32)
    n_slab = 2 * T + N_EXPERTS * tm
    n_tiles = n_slab // tm
    tile_start = jnp.arange(n_tiles, dtype=jnp.int32) * tm
    tile_expert = jnp.minimum(jnp.sum((tile_start[:, None] >= ends[None, :]).astype(jnp.int32), axis=1),
                              N_EXPERTS - 1).astype(jnp.int32)
    n_used = (ends[-1] // tm).astype(jnp.int32).reshape(1)
    gap = padded - counts
    gap_end = jnp.cumsum(gap)
    gap_start = gap_end - gap
    j = jnp.arange(N_EXPERTS * tm, dtype=jnp.int32)
    e = jnp.minimum(jnp.sum((j[:, None] >= gap_end[None, :]).astype(jnp.int32), axis=1), N_EXPERTS - 1)
    in_tile = starts[e] + counts[e] + (j - gap_start[e])
    pad_rows = jnp.where(j < gap_end[-1], in_tile, ends[-1] + (j - gap_end[-1]))
    return pos, tile_expert, n_used, pad_rows.astype(jnp.int32), n_tiles


def _layer(x, cosf, sina, sinb, B, S, lb, p, alpha, lam_init):
    T = B * S
    tm = min(512, T)
    w_in = p["w_in"]
    w_main = jnp.concatenate([w_in[:, :3584], w_in[:, 3592:5128]], axis=1).astype(BF16)
    w_ff = jnp.zeros((D_MODEL, LANES), BF16).at[:, :FOX_HEADS].set(w_in[:, 3584:3592].astype(BF16))
    w_gates = w_in[:, 5128:].astype(BF16)

    proj, ff = _in_proj_call(x, w_main, w_ff, cosf, sina, sinb, tm)
    ya = _hgrn_call(proj, lb, p["hgrn_norm_g"], B, S)
    qa, ka = _fox_gate_call(ff, p["fox_f_bias"], B, S)
    yb = _fox_call(proj, qa, ka, B, S)
    lv = p["diff_lambda"].astype(F32)
    lam = jnp.exp(jnp.sum(lv[0] * lv[1])) - jnp.exp(jnp.sum(lv[2] * lv[3])) + lam_init
    lam_row = jnp.full((1, LANES), lam, F32)
    norm_row = (p["diff_norm_g"].astype(F32) * (1.0 - lam_init)).reshape(1, DIFF_DV)
    yc = _diff_call(proj, lam_row, norm_row, B, S)

    w_router = jnp.zeros((D_MODEL, LANES), F32)
    w_router = w_router.at[:, :N_GROUPS].set(p["router_g_w"]).at[:, N_GROUPS:N_GROUPS + N_EXPERTS].set(p["router_e_w"])
    b_router = jnp.zeros((1, LANES), F32)
    b_router = b_router.at[0, :N_GROUPS].set(p["router_g_b"]).at[0, N_GROUPS:N_GROUPS + N_EXPERTS].set(
        p["router_e_b"].reshape(-1))
    x1, ids, wts = _merge_call(x, ya, yb, yc, w_gates, p["w_branch"].astype(BF16), p["w_out"].astype(BF16),
                               p["ln1_g"], p["ln1_b"], w_router.astype(BF16), b_router, alpha, tm)

    pos, tile_expert, n_used, pad_rows, n_tiles = _dispatch_plan(ids[:, :2], T)
    xs = _dispatch_call(pos, pad_rows, x1, n_tiles * EXPERT_TILE, tm)
    y = _expert_call(tile_expert, n_used, xs, n_tiles, p["expert_w_gate"].astype(BF16),
                     p["expert_w_up"].astype(BF16), p["expert_w_down"].astype(BF16))
    return _combine_call(pos, x1, wts, p["ln2_g"], p["ln2_b"], y, alpha, tm)


def kernel(x, positions, ln_in_g, ln_in_b, w_in, hgrn_lb_logits, hgrn_norm_g, fox_f_bias, diff_lambda,
           diff_norm_g, w_branch, w_out, ln1_g, ln1_b, router_g_w, router_g_b, router_e_w, router_e_b,
           expert_w_gate, expert_w_up, expert_w_down, ln2_g, ln2_b):
    B, S, D = x.shape
    T = B * S
    depth = w_in.shape[0]
    alpha = (2 * depth) ** 0.25
    cosf, sina, sinb = _rope_tables(positions)
    lb_soft = jax.nn.softmax(hgrn_lb_logits.astype(F32), axis=0)
    lower_bounds = jnp.maximum(jnp.cumsum(lb_soft, axis=0) - lb_soft[0], 0.0)

    h = _ln_call(x.reshape(T, D), ln_in_g, ln_in_b, min(512, T))
    for l in range(depth):
        p = dict(w_in=w_in[l], hgrn_norm_g=hgrn_norm_g[l], fox_f_bias=fox_f_bias[l], diff_lambda=diff_lambda[l],
                 diff_norm_g=diff_norm_g[l], w_branch=w_branch[l], w_out=w_out[l], ln1_g=ln1_g[l], ln1_b=ln1_b[l],
                 router_g_w=router_g_w[l], router_g_b=router_g_b[l], router_e_w=router_e_w[l],
                 router_e_b=router_e_b[l], expert_w_gate=expert_w_gate[l], expert_w_up=expert_w_up[l],
                 expert_w_down=expert_w_down[l], ln2_g=ln2_g[l], ln2_b=ln2_b[l])
        lam_init = 0.8 - 0.6 * float(math.exp(-0.3 * l))
        h = _layer(h, cosf, sina, sinb, B, S, lower_bounds[l], p, alpha, lam_init)
    return h.reshape(B, S, D)
```

```python
import functools
import math

import numpy as np
import jax
import jax.numpy as jnp
from jax import lax
from jax.experimental import pallas as pl
from jax.experimental.pallas import tpu as pltpu

F32 = jnp.float32
BF16 = jnp.bfloat16

D_MODEL = 1024
HG_HEADS, HG_D = 4, 128
FOX_HEADS, FOX_DH = 8, 64
DIFF_HEADS, DIFF_DH, DIFF_DV = 4, 64, 128
BRANCH_WIDTH = 512
N_BRANCHES = 3
ROPE_THETA = 500000.0
ROPE_DIM = DIFF_DH // 4
N_GROUPS, EXPERTS_PER_GROUP = 4, 8
N_EXPERTS = N_GROUPS * EXPERTS_PER_GROUP
D_EXPERT = 512
LN_EPS = 1e-5
RMS_EPS = 1e-6
NEG_INF = -1e30
EXP_CLAMP = 60.0
LOG2E = 1.4426950408889634

LANES = 128
N_MAIN = 5120
COL_HQ, COL_HF, COL_HI, COL_HG = 0, 4, 8, 12
COL_FQ, COL_FK, COL_FV = 16, 20, 24
COL_DQ, COL_DK, COL_DV = 28, 32, 36
ROPE_TILES = (7, 8)

HG_CHUNK = 64
HG_SUB = 8
HG_BATCH = 2
GATE_CHUNK = 256
ATTN_BLOCK = 512
EXPERT_TILE = 256
VMEM_LIMIT = 56 * 1024 * 1024


def _dot(a, b):
    return jnp.dot(a, b, preferred_element_type=F32)


def _dot_nt(a, b):
    return lax.dot_general(a, b, (((1,), (1,)), ((), ())), preferred_element_type=F32)


def _log_sigmoid(z):
    return jnp.minimum(z, 0.0) - jnp.log1p(jnp.exp(-jnp.abs(z)))


def _split3(x):
    h1 = x.astype(BF16)
    r1 = x - h1.astype(F32)
    h2 = r1.astype(BF16)
    h3 = (r1 - h2.astype(F32)).astype(BF16)
    return h1, h2, h3


def _cumsum_rows(tri, x):
    h1, h2, h3 = _split3(x)
    return _dot(tri, h1) + _dot(tri, h2) + _dot(tri, h3)


def _layer_norm(u, g, b):
    mu = jnp.mean(u, axis=-1, keepdims=True)
    d = u - mu
    var = jnp.mean(d * d, axis=-1, keepdims=True)
    return d * lax.rsqrt(var + LN_EPS) * g + b


def _params(sem):
    return pltpu.CompilerParams(dimension_semantics=sem, vmem_limit_bytes=VMEM_LIMIT)


def _ln_kernel(x_ref, g_ref, b_ref, o_ref):
    o_ref[...] = _layer_norm(x_ref[...], g_ref[...], b_ref[...])


def _ln_call(x, g, b, tm):
    T, D = x.shape
    return pl.pallas_call(
        _ln_kernel,
        grid=(T // tm,),
        in_specs=[pl.BlockSpec((tm, D), lambda i: (i, 0)),
                  pl.BlockSpec((1, D), lambda i: (0, 0)),
                  pl.BlockSpec((1, D), lambda i: (0, 0))],
        out_specs=pl.BlockSpec((tm, D), lambda i: (i, 0)),
        out_shape=jax.ShapeDtypeStruct((T, D), F32),
        compiler_params=_params(("parallel",)),
        name="ln_in",
    )(x, g.reshape(1, D), b.reshape(1, D))


def _in_proj_kernel(x_ref, w_ref, wff_ref, cos_ref, sa_ref, sb_ref, o_ref, ff_ref):
    xb = x_ref[...].astype(BF16)
    ff_ref[...] = _dot(xb, wff_ref[...])
    for j in range(N_MAIN // 512):
        acc = _dot(xb, w_ref[:, j * 512:(j + 1) * 512])
        if j in ROPE_TILES:
            cosf, sa, sb = cos_ref[...], sa_ref[...], sb_ref[...]
            for g in range(4):
                t = acc[:, g * LANES:(g + 1) * LANES]
                r = t * cosf + pltpu.roll(t, 8, 1) * sa + pltpu.roll(t, LANES - 8, 1) * sb
                o_ref[:, j * 512 + g * LANES:j * 512 + (g + 1) * LANES] = r.astype(BF16)
        else:
            o_ref[:, j * 512:(j + 1) * 512] = acc.astype(BF16)


def _in_proj_call(x, w_main, w_ff, cosf, sina, sinb, tm):
    T, D = x.shape
    const = lambda i: (0, 0)
    row = lambda i: (i, 0)
    return pl.pallas_call(
        _in_proj_kernel,
        grid=(T // tm,),
        in_specs=[pl.BlockSpec((tm, D), row),
                  pl.BlockSpec((D, N_MAIN), const),
                  pl.BlockSpec((D, LANES), const),
                  pl.BlockSpec((tm, LANES), row),
                  pl.BlockSpec((tm, LANES), row),
                  pl.BlockSpec((tm, LANES), row)],
        out_specs=[pl.BlockSpec((tm, N_MAIN), row),
                   pl.BlockSpec((tm, LANES), row)],
        out_shape=[jax.ShapeDtypeStruct((T, N_MAIN), BF16),
                   jax.ShapeDtypeStruct((T, LANES), F32)],
        compiler_params=_params(("parallel",)),
        name="in_proj",
    )(x, w_main, w_ff, cosf, sina, sinb)


def _hgrn_chunk(z, ql, v, gl, lb, ng, state_t, b_scr, g_scr, consts):
    C = HG_CHUNK
    tri, levels, diag_masks, lane_c = consts
    u = jnp.exp(-jnp.abs(z))
    log1pu = jnp.log(1.0 + u)
    log_f = (jnp.minimum(z, 0.0) - log1pu) + jnp.log(1.0 + lb * jnp.exp(jnp.minimum(-z, EXP_CLAMP)))
    log2k = (jnp.minimum(-z, 0.0) - log1pu) * LOG2E + jnp.log2(1.0 - lb)
    k = jnp.exp2(log2k)
    q = ql * jax.nn.sigmoid(ql)
    b2 = _cumsum_rows(tri, log_f) * LOG2E
    b_scr[...] = b2
    g_scr[...] = log2k - b2
    vb = v.astype(BF16)
    b_last = b_scr[C - 1:C, :]

    o = _dot_nt((q * jnp.exp2(b2)).astype(BF16), state_t.astype(BF16))

    scores = jnp.zeros((C, C), F32)
    for m, is_query, pair in levels:
        pieces = [jnp.broadcast_to(b_scr[p * 2 * m + m - 1:p * 2 * m + m, :], (2 * m, LANES))
                  for p in range(C // (2 * m))]
        b_ref_rows = pieces[0] if len(pieces) == 1 else jnp.concatenate(pieces, axis=0)
        decay = jnp.exp2(-jnp.abs(b2 - b_ref_rows))
        qd = jnp.where(is_query, q * decay, 0.0).astype(BF16)
        kd = jnp.where(is_query, 0.0, k * decay).astype(BF16)
        scores = scores + jnp.where(pair, _dot_nt(qd, kd), 0.0)

    diag = []
    for blk in range(C // HG_SUB):
        lo = blk * HG_SUB
        bb = b2[lo:lo + HG_SUB]
        qq = q[lo:lo + HG_SUB]
        blk_scores = jnp.zeros((HG_SUB, C), F32)
        for s in range(HG_SUB):
            w = qq * jnp.exp2(bb + g_scr[lo + s:lo + s + 1, :])
            blk_scores = jnp.where(lane_c == lo + s, jnp.sum(w, axis=-1, keepdims=True), blk_scores)
        diag.append(jnp.where(diag_masks[blk], blk_scores, 0.0))
    scores = scores + jnp.concatenate(diag, axis=0)
    o = o + _dot(scores.astype(BF16), vb)

    k_dec = (k * jnp.exp2(b_last - b2)).astype(BF16)
    state_t = state_t * jnp.exp2(b_last) + _dot(v.T.astype(BF16), k_dec)

    ms = jnp.mean(o * o, axis=-1, keepdims=True)
    y = o * lax.rsqrt(ms + RMS_EPS) * ng * (gl * jax.nn.sigmoid(gl))
    return y, state_t


def _hgrn_kernel(q_ref, f_ref, i_ref, g_ref, lb_ref, ng_ref, o_ref, b_scr, g_scr, *, nchunks):
    C = HG_CHUNK
    ng = ng_ref[...]
    ri = lax.broadcasted_iota(jnp.int32, (C, C), 0)
    ci = lax.broadcasted_iota(jnp.int32, (C, C), 1)
    tri = jnp.where(ri >= ci, 1.0, 0.0).astype(BF16)
    rows = lax.broadcasted_iota(jnp.int32, (C, LANES), 0)
    sub_rows = lax.broadcasted_iota(jnp.int32, (HG_SUB, C), 0)
    lane_c = lax.broadcasted_iota(jnp.int32, (HG_SUB, C), 1)
    diag_masks = [sub_rows + blk * HG_SUB >= lane_c for blk in range(C // HG_SUB)]
    levels = []
    m = HG_SUB
    while m < C:
        shift = int(math.log2(2 * m))
        is_query = (rows & (2 * m - 1)) >= m
        pair = ((ri >> shift) == (ci >> shift)) & ((ri & (2 * m - 1)) >= m) & ((ci & (2 * m - 1)) < m)
        levels.append((m, is_query, pair))
        m *= 2
    consts = (tri, levels, diag_masks, lane_c)

    nb = q_ref.shape[0]

    def body(n, states):
        r0 = pl.multiple_of(n * C, C)
        new_states = []
        for bi in range(nb):
            for h in range(HG_HEADS):
                cols = slice(h * LANES, (h + 1) * LANES)
                y, st = _hgrn_chunk(f_ref[bi, pl.ds(r0, C), cols].astype(F32), q_ref[bi, pl.ds(r0, C), cols].astype(F32),
                                    i_ref[bi, pl.ds(r0, C), cols].astype(F32), g_ref[bi, pl.ds(r0, C), cols].astype(F32),
                                    lb_ref[:, cols], ng, states[bi * HG_HEADS + h],
                                    b_scr.at[bi * HG_HEADS + h], g_scr.at[bi * HG_HEADS + h], consts)
                o_ref[bi, pl.ds(r0, C), cols] = y.astype(o_ref.dtype)
                new_states.append(st)
        return tuple(new_states)

    lax.fori_loop(0, nchunks, body, tuple(jnp.zeros((HG_D, HG_D), F32) for _ in range(nb * HG_HEADS)))


def _hgrn_call(proj, lb, norm_g, B, S):
    W = HG_HEADS * HG_D
    nb = HG_BATCH if B % HG_BATCH == 0 else 1
    proj3 = proj.reshape(B, S, N_MAIN)
    blk = lambda off: pl.BlockSpec((nb, S, W), lambda b, off=off: (b, 0, off))
    out = pl.pallas_call(
        functools.partial(_hgrn_kernel, nchunks=S // HG_CHUNK),
        grid=(B // nb,),
        in_specs=[blk(COL_HQ // 4), blk(COL_HF // 4), blk(COL_HI // 4), blk(COL_HG // 4),
                  pl.BlockSpec((1, W), lambda b: (0, 0)),
                  pl.BlockSpec((1, LANES), lambda b: (0, 0))],
        out_specs=pl.BlockSpec((nb, S, W), lambda b: (b, 0, 0)),
        out_shape=jax.ShapeDtypeStruct((B, S, BRANCH_WIDTH), BF16),
        scratch_shapes=[pltpu.VMEM((nb * HG_HEADS, HG_CHUNK, LANES), F32)] * 2,
        compiler_params=_params(("parallel",)),
        name="hgrn",
    )(proj3, proj3, proj3, proj3, lb.reshape(1, W), norm_g.reshape(1, HG_D))
    return out.reshape(B * S, BRANCH_WIDTH)


def _fox_aug_tables():
    sel_q = np.zeros((3, LANES, BRANCH_WIDTH), np.float32)
    sel_k = np.zeros((3, LANES, BRANCH_WIDTH), np.float32)
    ones_q = np.zeros((1, BRANCH_WIDTH), np.float32)
    ones_k = np.zeros((1, BRANCH_WIDTH), np.float32)
    for col in range(BRANCH_WIDTH):
        pair, within = divmod(col, LANES)
        half, slot = divmod(within, FOX_DH)
        head = 2 * pair + 1 - half
        if slot < 3:
            sel_q[slot, head, col] = 1.0
            ones_k[0, col] = 1.0
        elif slot < 6:
            sel_k[slot - 3, head, col] = -1.0
            ones_q[0, col] = 1.0
    return sel_q, sel_k, ones_q, ones_k


def _fox_gate_kernel(ff_ref, bias_ref, selq_ref, selk_ref, oq_ref, ok_ref, qa_ref, ka_ref, *, nchunks):
    CH = GATE_CHUNK
    ri = lax.broadcasted_iota(jnp.int32, (CH, CH), 0)
    ci = lax.broadcasted_iota(jnp.int32, (CH, CH), 1)
    tri = jnp.where(ri >= ci, 1.0, 0.0).astype(BF16)
    bias = bias_ref[...]

    def body(n, carry):
        r0 = pl.multiple_of(n * CH, CH)
        c = carry + _cumsum_rows(tri, _log_sigmoid(ff_ref[pl.ds(r0, CH), :] + bias))
        parts = _split3(c * LOG2E)
        qa = oq_ref[...] + _dot(parts[0], selq_ref[0]) + _dot(parts[1], selq_ref[1]) + _dot(parts[2], selq_ref[2])
        ka = ok_ref[...] + _dot(parts[0], selk_ref[0]) + _dot(parts[1], selk_ref[1]) + _dot(parts[2], selk_ref[2])
        qa_ref[pl.ds(r0, CH), :] = qa.astype(BF16)
        ka_ref[pl.ds(r0, CH), :] = ka.astype(BF16)
        return c[CH - 1:CH, :]

    lax.fori_loop(0, nchunks, body, jnp.zeros((1, LANES), F32))


def _fox_gate_call(ff, bias, B, S):
    T = B * S
    bias_row = jnp.zeros((1, LANES), F32).at[0, :FOX_HEADS].set(bias.astype(F32))
    sel_q, sel_k, ones_q, ones_k = _fox_aug_tables()
    const2 = lambda b: (0, 0)
    const3 = lambda b: (0, 0, 0)
    return pl.pallas_call(
        functools.partial(_fox_gate_kernel, nchunks=S // GATE_CHUNK),
        grid=(B,),
        in_specs=[pl.BlockSpec((S, LANES), lambda b: (b, 0)),
                  pl.BlockSpec((1, LANES), const2),
                  pl.BlockSpec((3, LANES, BRANCH_WIDTH), const3),
                  pl.BlockSpec((3, LANES, BRANCH_WIDTH), const3),
                  pl.BlockSpec((1, BRANCH_WIDTH), const2),
                  pl.BlockSpec((1, BRANCH_WIDTH), const2)],
        out_specs=[pl.BlockSpec((S, BRANCH_WIDTH), lambda b: (b, 0)),
                   pl.BlockSpec((S, BRANCH_WIDTH), lambda b: (b, 0))],
        out_shape=[jax.ShapeDtypeStruct((T, BRANCH_WIDTH), BF16),
                   jax.ShapeDtypeStruct((T, BRANCH_WIDTH), BF16)],
        compiler_params=_params(("parallel",)),
        name="fox_gate",
    )(ff, bias_row, jnp.asarray(sel_q, BF16), jnp.asarray(sel_k, BF16), jnp.asarray(ones_q), jnp.asarray(ones_k))


def _flash_t(qs, k_fns, vt_fns, n_full, n_blocks, s_a, s_b):
    tq = tk = ATTN_BLOCK
    nmap = len(qs)
    krow = lax.broadcasted_iota(jnp.int32, (tk, tq), 0)
    qcol = lax.broadcasted_iota(jnp.int32, (tk, tq), 1)

    def produce(dst, j):
        for i in range(nmap):
            dst[i] = _dot_nt(k_fns[i](j * tk), qs[i])

    def consume(src, stats, j, causal):
        out = []
        for i in range(nmap):
            m, l, acc = stats[i]
            s = src[i]
            if causal:
                s = jnp.where(krow <= qcol, s, NEG_INF)
            m_new = jnp.maximum(m, jnp.max(s, axis=0, keepdims=True))
            alpha = jnp.exp2(m - m_new)
            p = jnp.exp2(s - m_new)
            l = alpha * l + jnp.sum(p, axis=0, keepdims=True)
            acc = alpha * acc + _dot(vt_fns[i](j * tk), p.astype(BF16))
            out.append((m_new, l, acc))
        return tuple(out)

    stats = []
    for vt_fn in vt_fns:
        dv = vt_fn(0).shape[0]
        stats.append((jnp.full((1, tq), NEG_INF, F32), jnp.zeros((1, tq), F32), jnp.zeros((dv, tq), F32)))

    def branch(n):
        def run(stats):
            bufs = (s_a, s_b)
            produce(bufs[0], 0)
            for j in range(n + 1):
                if j < n:
                    produce(bufs[(j + 1) % 2], j + 1)
                stats = consume(bufs[j % 2], stats, j, j == n)
            return stats
        return run

    stats = lax.switch(n_full, [branch(n) for n in range(n_blocks)], tuple(stats))
    return [(acc, l) for (_, l, acc) in stats]


def _fill_vt(v_ref, vt_scr):
    S = v_ref.shape[0]
    for c in range(S // ATTN_BLOCK):
        sl = slice(c * ATTN_BLOCK, (c + 1) * ATTN_BLOCK)
        vt_scr[:, sl] = v_ref[sl, :].astype(F32).T.astype(BF16)


def _fox_kernel(q_ref, qa_ref, k_ref, ka_ref, v_ref, o_ref, vt_scr, s_a, s_b):
    qi = pl.program_id(2)
    tk = ATTN_BLOCK

    @pl.when(qi == 0)
    def _():
        _fill_vt(v_ref, vt_scr)

    lo = lax.broadcasted_iota(jnp.int32, (1, LANES), 1) < FOX_DH
    qf = q_ref[...].astype(F32) * (FOX_DH ** -0.5 * LOG2E)
    qa = qa_ref[...].astype(F32)
    qs = [jnp.where(lo, qf, qa).astype(BF16), jnp.where(lo, qa, qf).astype(BF16)]
    k_fns = [lambda k0: jnp.where(lo, k_ref[pl.ds(k0, tk), :], ka_ref[pl.ds(k0, tk), :]),
             lambda k0: jnp.where(lo, ka_ref[pl.ds(k0, tk), :], k_ref[pl.ds(k0, tk), :])]
    vt_fns = [lambda k0: vt_scr[0:FOX_DH, pl.ds(k0, tk)],
              lambda k0: vt_scr[FOX_DH:2 * FOX_DH, pl.ds(k0, tk)]]
    (acc0, l0), (acc1, l1) = _flash_t(qs, k_fns, vt_fns, qi, k_ref.shape[0] // tk, s_a, s_b)
    o_t = jnp.concatenate([acc0 / l0, acc1 / l1], axis=0)
    o_ref[...] = o_t.T.astype(o_ref.dtype)


def _fox_call(proj, qa, ka, B, S):
    T = B * S
    tq = ATTN_BLOCK
    nq = S // tq
    return pl.pallas_call(
        _fox_kernel,
        grid=(B, FOX_HEADS // 2, nq),
        in_specs=[pl.BlockSpec((tq, LANES), lambda b, p, i: (b * nq + i, COL_FQ + p)),
                  pl.BlockSpec((tq, LANES), lambda b, p, i: (b * nq + i, p)),
                  pl.BlockSpec((S, LANES), lambda b, p, i: (b, COL_FK + p)),
                  pl.BlockSpec((S, LANES), lambda b, p, i: (b, p)),
                  pl.BlockSpec((S, LANES), lambda b, p, i: (b, COL_FV + p))],
        out_specs=pl.BlockSpec((tq, LANES), lambda b, p, i: (b * nq + i, p)),
        out_shape=jax.ShapeDtypeStruct((T, BRANCH_WIDTH), BF16),
        scratch_shapes=[pltpu.VMEM((LANES, S), BF16)] + [pltpu.VMEM((2, tq, tq), F32)] * 2,
        compiler_params=_params(("parallel", "parallel", "arbitrary")),
        name="fox_attn",
    )(proj, qa, proj, ka, proj)


def _diff_kernel(q_ref, k_ref, v_ref, lam_ref, ng_ref, o_ref, vt_scr, s_a, s_b):
    qi = pl.program_id(2)
    tk = ATTN_BLOCK

    @pl.when(qi == 0)
    def _():
        _fill_vt(v_ref, vt_scr)

    lo = lax.broadcasted_iota(jnp.int32, (1, LANES), 1) < DIFF_DH
    qf = q_ref[...].astype(F32) * (DIFF_DH ** -0.5 * LOG2E)
    qs = [jnp.where(lo, qf, 0.0).astype(BF16), jnp.where(lo, 0.0, qf).astype(BF16)]
    k_fns = [lambda k0: k_ref[pl.ds(k0, tk), :]] * 2
    vt_fns = [lambda k0: vt_scr[:, pl.ds(k0, tk)]] * 2
    (acc0, l0), (acc1, l1) = _flash_t(qs, k_fns, vt_fns, qi, k_ref.shape[0] // tk, s_a, s_b)
    o = (acc0 / l0 - lam_ref[0:1, 0:1] * (acc1 / l1)).T
    ms = jnp.mean(o * o, axis=-1, keepdims=True)
    o_ref[...] = (o * lax.rsqrt(ms + RMS_EPS) * ng_ref[...]).astype(o_ref.dtype)


def _diff_call(proj, lam_row, norm_row, B, S):
    T = B * S
    tq = ATTN_BLOCK
    nq = S // tq
    return pl.pallas_call(
        _diff_kernel,
        grid=(B, DIFF_HEADS, nq),
        in_specs=[pl.BlockSpec((tq, LANES), lambda b, h, i: (b * nq + i, COL_DQ + h)),
                  pl.BlockSpec((S, LANES), lambda b, h, i: (b, COL_DK + h)),
                  pl.BlockSpec((S, LANES), lambda b, h, i: (b, COL_DV + h)),
                  pl.BlockSpec((1, LANES), lambda b, h, i: (0, 0)),
                  pl.BlockSpec((1, LANES), lambda b, h, i: (0, 0))],
        out_specs=pl.BlockSpec((tq, LANES), lambda b, h, i: (b * nq + i, h)),
        out_shape=jax.ShapeDtypeStruct((T, BRANCH_WIDTH), BF16),
        scratch_shapes=[pltpu.VMEM((LANES, S), BF16)] + [pltpu.VMEM((2, tq, tq), F32)] * 2,
        compiler_params=_params(("parallel", "parallel", "arbitrary")),
        name="diff_attn",
    )(proj, proj, proj, lam_row, norm_row)


def _merge_kernel(x_ref, ya_ref, yb_ref, yc_ref, wg_ref, wb_ref, wo_ref, g1_ref, b1_ref, wr_ref, br_ref,
                  x1_ref, ids_ref, wts_ref, *, alpha):
    x = x_ref[...]
    xb = x.astype(BF16)
    merged = None
    for r, y_ref in enumerate((ya_ref, yb_ref, yc_ref)):
        gate = jax.nn.sigmoid(_dot(xb, wg_ref[:, r * D_MODEL:(r + 1) * D_MODEL]))
        term = gate * _dot(y_ref[...], wb_ref[r])
        merged = term if merged is None else merged + term
    h = _dot(merged.astype(BF16), wo_ref[...])
    x1 = _layer_norm(alpha * x + h, g1_ref[...], b1_ref[...])
    x1_ref[...] = x1

    logits = _dot(x1.astype(BF16), wr_ref[...]) + br_ref[...]
    lane = lax.broadcasted_iota(jnp.int32, logits.shape, 1)
    lane_f = lane.astype(F32)
    is_group = lane < N_GROUPS
    gl = jnp.where(is_group, logits, NEG_INF)
    gmax = jnp.max(gl, axis=-1, keepdims=True)
    gsum = jnp.sum(jnp.where(is_group, jnp.exp(gl - gmax), 0.0), axis=-1, keepdims=True)
    g_p = 1.0 / gsum
    g_idx = jnp.min(jnp.where(gl == gmax, lane_f, float(LANES)), axis=-1, keepdims=True)
    lo = N_GROUPS + EXPERTS_PER_GROUP * g_idx
    in_group = (lane_f >= lo) & (lane_f < lo + EXPERTS_PER_GROUP)
    el = jnp.where(in_group, logits, NEG_INF)
    v1 = jnp.max(el, axis=-1, keepdims=True)
    i1 = jnp.min(jnp.where(el == v1, lane_f, float(LANES)), axis=-1, keepdims=True)
    el2 = jnp.where(lane_f == i1, NEG_INF, el)
    v2 = jnp.max(el2, axis=-1, keepdims=True)
    i2 = jnp.min(jnp.where(el2 == v2, lane_f, float(LANES)), axis=-1, keepdims=True)
    t = jnp.exp(v2 - v1)
    w1 = g_p / (1.0 + t)
    w2 = g_p * t / (1.0 + t)
    ids = jnp.where(lane == 0, i1 - N_GROUPS, jnp.where(lane == 1, i2 - N_GROUPS, 0.0))
    ids_ref[...] = ids.astype(jnp.int32)
    wts_ref[...] = jnp.where(lane == 0, w1, jnp.where(lane == 1, w2, 0.0))


def _merge_call(x, ya, yb, yc, w_gates, w_branch, w_out, ln_g, ln_b, w_router, b_router, alpha, tm):
    T, D = x.shape
    row = lambda i: (i, 0)
    const2 = lambda i: (0, 0)
    return pl.pallas_call(
        functools.partial(_merge_kernel, alpha=alpha),
        grid=(T // tm,),
        in_specs=[pl.BlockSpec((tm, D), row),
                  pl.BlockSpec((tm, BRANCH_WIDTH), row),
                  pl.BlockSpec((tm, BRANCH_WIDTH), row),
                  pl.BlockSpec((tm, BRANCH_WIDTH), row),
                  pl.BlockSpec((D, N_BRANCHES * D), const2),
                  pl.BlockSpec((N_BRANCHES, BRANCH_WIDTH, D), lambda i: (0, 0, 0)),
                  pl.BlockSpec((D, D), const2),
                  pl.BlockSpec((1, D), const2),
                  pl.BlockSpec((1, D), const2),
                  pl.BlockSpec((D, LANES), const2),
                  pl.BlockSpec((1, LANES), const2)],
        out_specs=[pl.BlockSpec((tm, D), row),
                   pl.BlockSpec((tm, LANES), row),
                   pl.BlockSpec((tm, LANES), row)],
        out_shape=[jax.ShapeDtypeStruct((T, D), F32),
                   jax.ShapeDtypeStruct((T, LANES), jnp.int32),
                   jax.ShapeDtypeStruct((T, LANES), F32)],
        compiler_params=_params(("parallel",)),
        name="merge_ln1_router",
    )(x, ya, yb, yc, w_gates, w_branch, w_out, ln_g.reshape(1, D), ln_b.reshape(1, D), w_router, b_router)


def _row_copy(src, src_row, dst, dst_row, sem):
    return pltpu.make_async_copy(src.at[pl.ds(src_row, 1), :], dst.at[pl.ds(dst_row, 1), :], sem)


def _dispatch_kernel(pos_ref, pad_ref, x_ref, xs_hbm, zero_scr, sem, *, tm, npad, pad_steps):
    def issue(r, carry):
        _row_copy(x_ref, r, xs_hbm, pos_ref[2 * r], sem).start()
        _row_copy(x_ref, r, xs_hbm, pos_ref[2 * r + 1], sem).start()
        return carry

    lax.fori_loop(0, tm, issue, 0, unroll=8)

    @pl.when(pl.program_id(0) < pad_steps)
    def _():
        zero_scr[...] = jnp.zeros_like(zero_scr)

        def issue_pad(r, carry):
            _row_copy(zero_scr, r, xs_hbm, pad_ref[r], sem).start()
            return carry

        lax.fori_loop(0, npad, issue_pad, 0, unroll=8)
        pltpu.make_async_copy(zero_scr, xs_hbm.at[pl.ds(0, npad), :], sem).wait()

    for _ in range(2):
        pltpu.make_async_copy(x_ref, xs_hbm.at[pl.ds(0, tm), :], sem).wait()


def _dispatch_call(pos, pad_rows, x1, n_rows, tm):
    T, D = x1.shape
    nsteps = T // tm
    npad = max(LANES, pad_rows.shape[0] // nsteps)
    pad_steps = pad_rows.shape[0] // npad
    return pl.pallas_call(
        functools.partial(_dispatch_kernel, tm=tm, npad=npad, pad_steps=pad_steps),
        grid=(nsteps,),
        in_specs=[pl.BlockSpec((2 * tm,), lambda i: (i,), memory_space=pltpu.SMEM),
                  pl.BlockSpec((npad,), lambda i: (jnp.minimum(i, pad_steps - 1),), memory_space=pltpu.SMEM),
                  pl.BlockSpec((tm, D), lambda i: (i, 0))],
        out_specs=pl.BlockSpec(memory_space=pl.ANY),
        out_shape=jax.ShapeDtypeStruct((n_rows, D), F32),
        scratch_shapes=[pltpu.VMEM((npad, D), F32), pltpu.SemaphoreType.DMA(())],
        compiler_params=_params(("arbitrary",)),
        name="moe_dispatch",
    )(pos, pad_rows, x1)


def _expert_kernel(te_ref, nu_ref, xs_ref, wg_ref, wu_ref, wd_ref, o_ref):
    t = pl.program_id(0)

    @pl.when(t < nu_ref[0])
    def _():
        xb = xs_ref[...].astype(BF16)
        g = _dot(xb, wg_ref[0])
        u = _dot(xb, wu_ref[0])
        h = (g * jax.nn.sigmoid(g)) * u
        o_ref[...] = _dot(h.astype(BF16), wd_ref[0])

    @pl.when(t >= nu_ref[0])
    def _():
        o_ref[...] = jnp.zeros_like(o_ref)


def _expert_call(tile_expert, n_used, xs, n_tiles, w_gate, w_up, w_down):
    D = xs.shape[1]
    tm = EXPERT_TILE
    used = lambda t, te, nu: (jnp.minimum(t, nu[0] - 1), 0)
    grid_spec = pltpu.PrefetchScalarGridSpec(
        num_scalar_prefetch=2,
        grid=(n_tiles,),
        in_specs=[pl.BlockSpec((tm, D), used),
                  pl.BlockSpec((1, D, D_EXPERT), lambda t, te, nu: (te[t], 0, 0)),
                  pl.BlockSpec((1, D, D_EXPERT), lambda t, te, nu: (te[t], 0, 0)),
                  pl.BlockSpec((1, D_EXPERT, D), lambda t, te, nu: (te[t], 0, 0))],
        out_specs=pl.BlockSpec((tm, D), lambda t, te, nu: (t, 0)),
    )
    return pl.pallas_call(
        _expert_kernel,
        grid_spec=grid_spec,
        out_shape=jax.ShapeDtypeStruct((n_tiles * tm, D), F32),
        compiler_params=_params(("arbitrary",)),
        name="experts",
    )(tile_expert, n_used, xs, w_gate, w_up, w_down)


def _combine_kernel(pos_ref, x_ref, w_ref, g_ref, b_ref, y_hbm, o_ref, ybuf, sem, *, alpha, tm):
    def issue(r, carry):
        _row_copy(y_hbm, pos_ref[2 * r], ybuf.at[0], r, sem).start()
        _row_copy(y_hbm, pos_ref[2 * r + 1], ybuf.at[1], r, sem).start()
        return carry

    lax.fori_loop(0, tm, issue, 0, unroll=8)
    for k in range(2):
        pltpu.make_async_copy(y_hbm.at[pl.ds(0, tm), :], ybuf.at[k], sem).wait()
    w = w_ref[...]
    u = alpha * x_ref[...] + w[:, 0:1] * ybuf[0] + w[:, 1:2] * ybuf[1]
    o_ref[...] = _layer_norm(u, g_ref[...], b_ref[...])


def _combine_call(pos, x1, wts, ln_g, ln_b, y, alpha, tm):
    T, D = x1.shape
    row = lambda i: (i, 0)
    const2 = lambda i: (0, 0)
    return pl.pallas_call(
        functools.partial(_combine_kernel, alpha=alpha, tm=tm),
        grid=(T // tm,),
        in_specs=[pl.BlockSpec((2 * tm,), lambda i: (i,), memory_space=pltpu.SMEM),
                  pl.BlockSpec((tm, D), row),
                  pl.BlockSpec((tm, LANES), row),
                  pl.BlockSpec((1, D), const2), pl.BlockSpec((1, D), const2),
                  pl.BlockSpec(memory_space=pl.ANY)],
        out_specs=pl.BlockSpec((tm, D), row),
        out_shape=jax.ShapeDtypeStruct((T, D), F32),
        scratch_shapes=[pltpu.VMEM((2, tm, D), F32), pltpu.SemaphoreType.DMA(())],
        compiler_params=_params(("arbitrary",)),
        name="combine_ln2",
    )(pos, x1, wts, ln_g.reshape(1, D), ln_b.reshape(1, D), y)


def _rope_tables(positions):
    half = ROPE_DIM // 2
    inv_freq = ROPE_THETA ** (-jnp.arange(0, ROPE_DIM, 2, dtype=F32) / ROPE_DIM)
    ang = positions.astype(F32).reshape(-1, 1) * inv_freq[None, :]
    cos, sin = jnp.cos(ang), jnp.sin(ang)
    lane = jnp.arange(LANES)
    in_head = lane % DIFF_DH
    freq = in_head % half
    first = in_head < half
    second = (in_head >= half) & (in_head < ROPE_DIM)
    cosf = jnp.where((first | second)[None, :], cos[:, freq], 1.0)
    sin_a = jnp.where(second[None, :], sin[:, freq], 0.0)
    sin_b = jnp.where(first[None, :], -sin[:, freq], 0.0)
    return cosf, sin_a, sin_b


def _dispatch_plan(ids, T):
    tm = EXPERT_TILE
    flat = ids.reshape(-1)
    onehot = (flat[:, None] == jnp.arange(N_EXPERTS, dtype=jnp.int32)[None, :]).astype(jnp.int32)
    csum = jnp.cumsum(onehot, axis=0)
    counts = csum[-1]
    rank = jnp.take_along_axis(csum, flat[:, None], axis=1)[:, 0] - 1
    padded = ((counts + tm - 1) // tm) * tm
    ends = jnp.cumsum(padded)
    starts = ends - padded
    pos = (starts[flat] + rank).astype(jnp.int32)
    n_slab = 2 * T + N_EXPERTS * tm
    n_tiles = n_slab // tm
    tile_start = jnp.arange(n_tiles, dtype=jnp.int32) * tm
    tile_expert = jnp.minimum(jnp.sum((tile_start[:, None] >= ends[None, :]).astype(jnp.int32), axis=1),
                              N_EXPERTS - 1).astype(jnp.int32)
    n_used = (ends[-1] // tm).astype(jnp.int32).reshape(1)
    gap = padded - counts
    gap_end = jnp.cumsum(gap)
    gap_start = gap_end - gap
    j = jnp.arange(N_EXPERTS * tm, dtype=jnp.int32)
    e = jnp.minimum(jnp.sum((j[:, None] >= gap_end[None, :]).astype(jnp.int32), axis=1), N_EXPERTS - 1)
    in_tile = starts[e] + counts[e] + (j - gap_start[e])
    pad_rows = jnp.where(j < gap_end[-1], in_tile, ends[-1] + (j - gap_end[-1]))
    return pos, tile_expert, n_used, pad_rows.astype(jnp.int32), n_tiles


def _layer(x, cosf, sina, sinb, B, S, lb, p, alpha, lam_init):
    T = B * S
    tm = min(512, T)
    w_in = p["w_in"]
    w_main = jnp.concatenate([w_in[:, :3584], w_in[:, 3592:5128]], axis=1).astype(BF16)
    w_ff = jnp.zeros((D_MODEL, LANES), BF16).at[:, :FOX_HEADS].set(w_in[:, 3584:3592].astype(BF16))
    w_gates = w_in[:, 5128:].astype(BF16)

    proj, ff = _in_proj_call(x, w_main, w_ff, cosf, sina, sinb, tm)
    ya = _hgrn_call(proj, lb, p["hgrn_norm_g"], B, S)
    qa, ka = _fox_gate_call(ff, p["fox_f_bias"], B, S)
    yb = _fox_call(proj, qa, ka, B, S)
    lv = p["diff_lambda"].astype(F32)
    lam = jnp.exp(jnp.sum(lv[0] * lv[1])) - jnp.exp(jnp.sum(lv[2] * lv[3])) + lam_init
    lam_row = jnp.full((1, LANES), lam, F32)
    norm_row = (p["diff_norm_g"].astype(F32) * (1.0 - lam_init)).reshape(1, DIFF_DV)
    yc = _diff_call(proj, lam_row, norm_row, B, S)

    w_router = jnp.zeros((D_MODEL, LANES), F32)
    w_router = w_router.at[:, :N_GROUPS].set(p["router_g_w"]).at[:, N_GROUPS:N_GROUPS + N_EXPERTS].set(p["router_e_w"])
    b_router = jnp.zeros((1, LANES), F32)
    b_router = b_router.at[0, :N_GROUPS].set(p["router_g_b"]).at[0, N_GROUPS:N_GROUPS + N_EXPERTS].set(
        p["router_e_b"].reshape(-1))
    x1, ids, wts = _merge_call(x, ya, yb, yc, w_gates, p["w_branch"].astype(BF16), p["w_out"].astype(BF16),
                               p["ln1_g"], p["ln1_b"], w_router.astype(BF16), b_router, alpha, tm)

    pos, tile_expert, n_used, pad_rows, n_tiles = _dispatch_plan(ids[:, :2], T)
    xs = _dispatch_call(pos, pad_rows, x1, n_tiles * EXPERT_TILE, tm)
    y = _expert_call(tile_expert, n_used, xs, n_tiles, p["expert_w_gate"].astype(BF16),
                     p["expert_w_up"].astype(BF16), p["expert_w_down"].astype(BF16))
    return _combine_call(pos, x1, wts, p["ln2_g"], p["ln2_b"], y, alpha, tm)


def kernel(x, positions, ln_in_g, ln_in_b, w_in, hgrn_lb_logits, hgrn_norm_g, fox_f_bias, diff_lambda,
           diff_norm_g, w_branch, w_out, ln1_g, ln1_b, router_g_w, router_g_b, router_e_w, router_e_b,
           expert_w_gate, expert_w_up, expert_w_down, ln2_g, ln2_b):
    B, S, D = x.shape
    T = B * S
    depth = w_in.shape[0]
    alpha = (2 * depth) ** 0.25
    cosf, sina, sinb = _rope_tables(positions)
    lb_soft = jax.nn.softmax(hgrn_lb_logits.astype(F32), axis=0)
    lower_bounds = jnp.maximum(jnp.cumsum(lb_soft, axis=0) - lb_soft[0], 0.0)

    h = _ln_call(x.reshape(T, D), ln_in_g, ln_in_b, min(512, T))
    for l in range(depth):
        p = dict(w_in=w_in[l], hgrn_norm_g=hgrn_norm_g[l], fox_f_bias=fox_f_bias[l], diff_lambda=diff_lambda[l],
                 diff_norm_g=diff_norm_g[l], w_branch=w_branch[l], w_out=w_out[l], ln1_g=ln1_g[l], ln1_b=ln1_b[l],
                 router_g_w=router_g_w[l], router_g_b=router_g_b[l], router_e_w=router_e_w[l],
                 router_e_b=router_e_b[l], expert_w_gate=expert_w_gate[l], expert_w_up=expert_w_up[l],
                 expert_w_down=expert_w_down[l], ln2_g=ln2_g[l], ln2_b=ln2_b[l])
        lam_init = 0.8 - 0.6 * float(math.exp(-0.3 * l))
        h = _layer(h, cosf, sina, sinb, B, S, lower_bounds[l], p, alpha, lam_init)
    return h.reshape(B, S, D)
```

```python
import functools
import math

import numpy as np
import jax
import jax.numpy as jnp
from jax import lax
from jax.experimental import pallas as pl
from jax.experimental.pallas import tpu as pltpu

F32 = jnp.float32
BF16 = jnp.bfloat16

D_MODEL = 1024
HG_HEADS, HG_D = 4, 128
FOX_HEADS, FOX_DH = 8, 64
DIFF_HEADS, DIFF_DH, DIFF_DV = 4, 64, 128
BRANCH_WIDTH = 512
N_BRANCHES = 3
ROPE_THETA = 500000.0
ROPE_DIM = DIFF_DH // 4
N_GROUPS, EXPERTS_PER_GROUP = 4, 8
N_EXPERTS = N_GROUPS * EXPERTS_PER_GROUP
D_EXPERT = 512
LN_EPS = 1e-5
RMS_EPS = 1e-6
NEG_INF = -1e30
EXP_CLAMP = 60.0
LOG2E = 1.4426950408889634

LANES = 128
N_MAIN = 5120
COL_HQ, COL_HF, COL_HI, COL_HG = 0, 4, 8, 12
COL_FQ, COL_FK, COL_FV = 16, 20, 24
COL_DQ, COL_DK, COL_DV = 28, 32, 36
ROPE_TILES = (7, 8)

HG_CHUNK = 64
HG_SUB = 8
HG_BATCH = 2
GATE_CHUNK = 256
ATTN_BLOCK = 512
EXPERT_TILE = 256
VMEM_LIMIT = 56 * 1024 * 1024


def _dot(a, b):
    return jnp.dot(a, b, preferred_element_type=F32)


def _dot_nt(a, b):
    return lax.dot_general(a, b, (((1,), (1,)), ((), ())), preferred_element_type=F32)


def _log_sigmoid(z):
    return jnp.minimum(z, 0.0) - jnp.log1p(jnp.exp(-jnp.abs(z)))


def _split3(x):
    h1 = x.astype(BF16)
    r1 = x - h1.astype(F32)
    h2 = r1.astype(BF16)
    h3 = (r1 - h2.astype(F32)).astype(BF16)
    return h1, h2, h3


def _cumsum_rows(tri, x):
    h1, h2, h3 = _split3(x)
    return _dot(tri, h1) + _dot(tri, h2) + _dot(tri, h3)


def _layer_norm(u, g, b):
    mu = jnp.mean(u, axis=-1, keepdims=True)
    d = u - mu
    var = jnp.mean(d * d, axis=-1, keepdims=True)
    return d * lax.rsqrt(var + LN_EPS) * g + b


def _params(sem):
    return pltpu.CompilerParams(dimension_semantics=sem, vmem_limit_bytes=VMEM_LIMIT)


def _ln_kernel(x_ref, g_ref, b_ref, o_ref):
    o_ref[...] = _layer_norm(x_ref[...], g_ref[...], b_ref[...])


def _ln_call(x, g, b, tm):
    T, D = x.shape
    return pl.pallas_call(
        _ln_kernel,
        grid=(T // tm,),
        in_specs=[pl.BlockSpec((tm, D), lambda i: (i, 0)),
                  pl.BlockSpec((1, D), lambda i: (0, 0)),
                  pl.BlockSpec((1, D), lambda i: (0, 0))],
        out_specs=pl.BlockSpec((tm, D), lambda i: (i, 0)),
        out_shape=jax.ShapeDtypeStruct((T, D), F32),
        compiler_params=_params(("parallel",)),
        name="ln_in",
    )(x, g.reshape(1, D), b.reshape(1, D))


def _in_proj_kernel(x_ref, w_ref, wff_ref, cos_ref, sa_ref, sb_ref, o_ref, ff_ref):
    xb = x_ref[...].astype(BF16)
    ff_ref[...] = _dot(xb, wff_ref[...])
    for j in range(N_MAIN // 512):
        acc = _dot(xb, w_ref[:, j * 512:(j + 1) * 512])
        if j in ROPE_TILES:
            cosf, sa, sb = cos_ref[...], sa_ref[...], sb_ref[...]
            for g in range(4):
                t = acc[:, g * LANES:(g + 1) * LANES]
                r = t * cosf + pltpu.roll(t, 8, 1) * sa + pltpu.roll(t, LANES - 8, 1) * sb
                o_ref[:, j * 512 + g * LANES:j * 512 + (g + 1) * LANES] = r.astype(BF16)
        else:
            o_ref[:, j * 512:(j + 1) * 512] = acc.astype(BF16)


def _in_proj_call(x, w_main, w_ff, cosf, sina, sinb, tm):
    T, D = x.shape
    const = lambda i: (0, 0)
    row = lambda i: (i, 0)
    return pl.pallas_call(
        _in_proj_kernel,
        grid=(T // tm,),
        in_specs=[pl.BlockSpec((tm, D), row),
                  pl.BlockSpec((D, N_MAIN), const),
                  pl.BlockSpec((D, LANES), const),
                  pl.BlockSpec((tm, LANES), row),
                  pl.BlockSpec((tm, LANES), row),
                  pl.BlockSpec((tm, LANES), row)],
        out_specs=[pl.BlockSpec((tm, N_MAIN), row),
                   pl.BlockSpec((tm, LANES), row)],
        out_shape=[jax.ShapeDtypeStruct((T, N_MAIN), BF16),
                   jax.ShapeDtypeStruct((T, LANES), F32)],
        compiler_params=_params(("parallel",)),
        name="in_proj",
    )(x, w_main, w_ff, cosf, sina, sinb)


def _hgrn_chunk(z, ql, v, gl, lb, ng, state_t, b_scr, g_scr, consts):
    C = HG_CHUNK
    tri, levels, diag_masks, lane_c = consts
    u = jnp.exp(-jnp.abs(z))
    log1pu = jnp.log(1.0 + u)
    log_f = (jnp.minimum(z, 0.0) - log1pu) + jnp.log(1.0 + lb * jnp.exp(jnp.minimum(-z, EXP_CLAMP)))
    log2k = (jnp.minimum(-z, 0.0) - log1pu) * LOG2E + jnp.log2(1.0 - lb)
    k = jnp.exp2(log2k)
    q = ql * jax.nn.sigmoid(ql)
    b2 = _cumsum_rows(tri, log_f) * LOG2E
    b_scr[...] = b2
    g_scr[...] = log2k - b2
    vb = v.astype(BF16)
    b_last = b_scr[C - 1:C, :]

    o = _dot_nt((q * jnp.exp2(b2)).astype(BF16), state_t.astype(BF16))

    scores = jnp.zeros((C, C), F32)
    for m, is_query, pair in levels:
        pieces = [jnp.broadcast_to(b_scr[p * 2 * m + m - 1:p * 2 * m + m, :], (2 * m, LANES))
                  for p in range(C // (2 * m))]
        b_ref_rows = pieces[0] if len(pieces) == 1 else jnp.concatenate(pieces, axis=0)
        decay = jnp.exp2(-jnp.abs(b2 - b_ref_rows))
        qd = jnp.where(is_query, q * decay, 0.0).astype(BF16)
        kd = jnp.where(is_query, 0.0, k * decay).astype(BF16)
        scores = scores + jnp.where(pair, _dot_nt(qd, kd), 0.0)

    diag = []
    for blk in range(C // HG_SUB):
        lo = blk * HG_SUB
        bb = b2[lo:lo + HG_SUB]
        qq = q[lo:lo + HG_SUB]
        blk_scores = jnp.zeros((HG_SUB, C), F32)
        for s in range(HG_SUB):
            w = qq * jnp.exp2(bb + g_scr[lo + s:lo + s + 1, :])
            blk_scores = jnp.where(lane_c == lo + s, jnp.sum(w, axis=-1, keepdims=True), blk_scores)
        diag.append(jnp.where(diag_masks[blk], blk_scores, 0.0))
    scores = scores + jnp.concatenate(diag, axis=0)
    o = o + _dot(scores.astype(BF16), vb)

    k_dec = (k * jnp.exp2(b_last - b2)).astype(BF16)
    state_t = state_t * jnp.exp2(b_last) + _dot(v.T.astype(BF16), k_dec)

    ms = jnp.mean(o * o, axis=-1, keepdims=True)
    y = o * lax.rsqrt(ms + RMS_EPS) * ng * (gl * jax.nn.sigmoid(gl))
    return y, state_t


def _hgrn_kernel(q_ref, f_ref, i_ref, g_ref, lb_ref, ng_ref, o_ref, b_scr, g_scr, *, nchunks):
    C = HG_CHUNK
    ng = ng_ref[...]
    ri = lax.broadcasted_iota(jnp.int32, (C, C), 0)
    ci = lax.broadcasted_iota(jnp.int32, (C, C), 1)
    tri = jnp.where(ri >= ci, 1.0, 0.0).astype(BF16)
    rows = lax.broadcasted_iota(jnp.int32, (C, LANES), 0)
    sub_rows = lax.broadcasted_iota(jnp.int32, (HG_SUB, C), 0)
    lane_c = lax.broadcasted_iota(jnp.int32, (HG_SUB, C), 1)
    diag_masks = [sub_rows + blk * HG_SUB >= lane_c for blk in range(C // HG_SUB)]
    levels = []
    m = HG_SUB
    while m < C:
        shift = int(math.log2(2 * m))
        is_query = (rows & (2 * m - 1)) >= m
        pair = ((ri >> shift) == (ci >> shift)) & ((ri & (2 * m - 1)) >= m) & ((ci & (2 * m - 1)) < m)
        levels.append((m, is_query, pair))
        m *= 2
    consts = (tri, levels, diag_masks, lane_c)

    nb = q_ref.shape[0]

    def body(n, states):
        r0 = pl.multiple_of(n * C, C)
        new_states = []
        for bi in range(nb):
            for h in range(HG_HEADS):
                cols = slice(h * LANES, (h + 1) * LANES)
                y, st = _hgrn_chunk(f_ref[bi, pl.ds(r0, C), cols].astype(F32), q_ref[bi, pl.ds(r0, C), cols].astype(F32),
                                    i_ref[bi, pl.ds(r0, C), cols].astype(F32), g_ref[bi, pl.ds(r0, C), cols].astype(F32),
                                    lb_ref[:, cols], ng, states[bi * HG_HEADS + h],
                                    b_scr.at[bi * HG_HEADS + h], g_scr.at[bi * HG_HEADS + h], consts)
                o_ref[bi, pl.ds(r0, C), cols] = y.astype(o_ref.dtype)
                new_states.append(st)
        return tuple(new_states)

    lax.fori_loop(0, nchunks, body, tuple(jnp.zeros((HG_D, HG_D), F32) for _ in range(nb * HG_HEADS)))


def _hgrn_call(proj, lb, norm_g, B, S):
    W = HG_HEADS * HG_D
    nb = HG_BATCH if B % HG_BATCH == 0 else 1
    proj3 = proj.reshape(B, S, N_MAIN)
    blk = lambda off: pl.BlockSpec((nb, S, W), lambda b, off=off: (b, 0, off))
    out = pl.pallas_call(
        functools.partial(_hgrn_kernel, nchunks=S // HG_CHUNK),
        grid=(B // nb,),
        in_specs=[blk(COL_HQ // 4), blk(COL_HF // 4), blk(COL_HI // 4), blk(COL_HG // 4),
                  pl.BlockSpec((1, W), lambda b: (0, 0)),
                  pl.BlockSpec((1, LANES), lambda b: (0, 0))],
        out_specs=pl.BlockSpec((nb, S, W), lambda b: (b, 0, 0)),
        out_shape=jax.ShapeDtypeStruct((B, S, BRANCH_WIDTH), BF16),
        scratch_shapes=[pltpu.VMEM((nb * HG_HEADS, HG_CHUNK, LANES), F32)] * 2,
        compiler_params=_params(("parallel",)),
        name="hgrn",
    )(proj3, proj3, proj3, proj3, lb.reshape(1, W), norm_g.reshape(1, HG_D))
    return out.reshape(B * S, BRANCH_WIDTH)


def _fox_aug_tables():
    sel_q = np.zeros((3, LANES, BRANCH_WIDTH), np.float32)
    sel_k = np.zeros((3, LANES, BRANCH_WIDTH), np.float32)
    ones_q = np.zeros((1, BRANCH_WIDTH), np.float32)
    ones_k = np.zeros((1, BRANCH_WIDTH), np.float32)
    for col in range(BRANCH_WIDTH):
        pair, within = divmod(col, LANES)
        half, slot = divmod(within, FOX_DH)
        head = 2 * pair + 1 - half
        if slot < 3:
            sel_q[slot, head, col] = 1.0
            ones_k[0, col] = 1.0
        elif slot < 6:
            sel_k[slot - 3, head, col] = -1.0
            ones_q[0, col] = 1.0
    return sel_q, sel_k, ones_q, ones_k


def _fox_gate_kernel(ff_ref, bias_ref, selq_ref, selk_ref, oq_ref, ok_ref, qa_ref, ka_ref, *, nchunks):
    CH = GATE_CHUNK
    ri = lax.broadcasted_iota(jnp.int32, (CH, CH), 0)
    ci = lax.broadcasted_iota(jnp.int32, (CH, CH), 1)
    tri = jnp.where(ri >= ci, 1.0, 0.0).astype(BF16)
    bias = bias_ref[...]

    def body(n, carry):
        r0 = pl.multiple_of(n * CH, CH)
        c = carry + _cumsum_rows(tri, _log_sigmoid(ff_ref[pl.ds(r0, CH), :] + bias))
        parts = _split3(c * LOG2E)
        qa = oq_ref[...] + _dot(parts[0], selq_ref[0]) + _dot(parts[1], selq_ref[1]) + _dot(parts[2], selq_ref[2])
        ka = ok_ref[...] + _dot(parts[0], selk_ref[0]) + _dot(parts[1], selk_ref[1]) + _dot(parts[2], selk_ref[2])
        qa_ref[pl.ds(r0, CH), :] = qa.astype(BF16)
        ka_ref[pl.ds(r0, CH), :] = ka.astype(BF16)
        return c[CH - 1:CH, :]

    lax.fori_loop(0, nchunks, body, jnp.zeros((1, LANES), F32))


def _fox_gate_call(ff, bias, B, S):
    T = B * S
    bias_row = jnp.zeros((1, LANES), F32).at[0, :FOX_HEADS].set(bias.astype(F32))
    sel_q, sel_k, ones_q, ones_k = _fox_aug_tables()
    const2 = lambda b: (0, 0)
    const3 = lambda b: (0, 0, 0)
    return pl.pallas_call(
        functools.partial(_fox_gate_kernel, nchunks=S // GATE_CHUNK),
        grid=(B,),
        in_specs=[pl.BlockSpec((S, LANES), lambda b: (b, 0)),
                  pl.BlockSpec((1, LANES), const2),
                  pl.BlockSpec((3, LANES, BRANCH_WIDTH), const3),
                  pl.BlockSpec((3, LANES, BRANCH_WIDTH), const3),
                  pl.BlockSpec((1, BRANCH_WIDTH), const2),
                  pl.BlockSpec((1, BRANCH_WIDTH), const2)],
        out_specs=[pl.BlockSpec((S, BRANCH_WIDTH), lambda b: (b, 0)),
                   pl.BlockSpec((S, BRANCH_WIDTH), lambda b: (b, 0))],
        out_shape=[jax.ShapeDtypeStruct((T, BRANCH_WIDTH), BF16),
                   jax.ShapeDtypeStruct((T, BRANCH_WIDTH), BF16)],
        compiler_params=_params(("parallel",)),
        name="fox_gate",
    )(ff, bias_row, jnp.asarray(sel_q, BF16), jnp.asarray(sel_k, BF16), jnp.asarray(ones_q), jnp.asarray(ones_k))


def _flash_t(qs, k_fns, vt_fns, n_full, n_blocks, s_a, s_b):
    tq = tk = ATTN_BLOCK
    nmap = len(qs)

    def produce(dst, j):
        for i in range(nmap):
            dst[i] = _dot_nt(k_fns[i](j * tk, tk), qs[i])

    def update(s, m, l, acc, vt):
        m_new = jnp.maximum(m, jnp.max(s, axis=0, keepdims=True))
        alpha = jnp.exp2(m - m_new)
        p = jnp.exp2(s - m_new)
        return m_new, alpha * l + jnp.sum(p, axis=0, keepdims=True), alpha * acc + _dot(vt, p.astype(BF16))

    def consume(src, stats, j):
        return tuple(update(src[i], *stats[i], vt_fns[i](j * tk, tk)) for i in range(nmap))

    h = tk // 2
    mask_a = lax.broadcasted_iota(jnp.int32, (h, tq), 0) <= lax.broadcasted_iota(jnp.int32, (h, tq), 1)
    mask_b = lax.broadcasted_iota(jnp.int32, (h, h), 0) <= lax.broadcasted_iota(jnp.int32, (h, h), 1)

    def produce_diag(dst, j):
        for i in range(nmap):
            dst[i, 0:h, :] = _dot_nt(k_fns[i](j * tk, h), qs[i])
            dst[i, h:tk, h:tq] = _dot_nt(k_fns[i](j * tk + h, h), qs[i][h:tq])

    def consume_diag(src, stats, j):
        out = []
        for i in range(nmap):
            m, l, acc = update(jnp.where(mask_a, src[i, 0:h, :], NEG_INF), *stats[i], vt_fns[i](j * tk, h))
            m_b, l_b, acc_b = update(jnp.where(mask_b, src[i, h:tk, h:tq], NEG_INF), m[:, h:], l[:, h:], acc[:, h:],
                                     vt_fns[i](j * tk + h, h))
            out.append((jnp.concatenate([m[:, :h], m_b], axis=1), jnp.concatenate([l[:, :h], l_b], axis=1),
                        jnp.concatenate([acc[:, :h], acc_b], axis=1)))
        return tuple(out)

    stats = []
    for vt_fn in vt_fns:
        dv = vt_fn(0, tk).shape[0]
        stats.append((jnp.full((1, tq), NEG_INF, F32), jnp.zeros((1, tq), F32), jnp.zeros((dv, tq), F32)))

    def branch(n):
        def run(stats):
            bufs = (s_a, s_b)
            (produce if n > 0 else produce_diag)(bufs[0], 0)
            for j in range(n):
                (produce if j + 1 < n else produce_diag)(bufs[(j + 1) % 2], j + 1)
                stats = consume(bufs[j % 2], stats, j)
            return consume_diag(bufs[n % 2], stats, n)
        return run

    stats = lax.switch(n_full, [branch(n) for n in range(n_blocks)], tuple(stats))
    return [(acc, l) for (_, l, acc) in stats]


def _fill_vt(v_ref, vt_scr):
    S = v_ref.shape[0]
    for c in range(S // ATTN_BLOCK):
        sl = slice(c * ATTN_BLOCK, (c + 1) * ATTN_BLOCK)
        vt_scr[:, sl] = v_ref[sl, :].astype(F32).T.astype(BF16)


def _fox_kernel(q_ref, qa_ref, k_ref, ka_ref, v_ref, o_ref, vt_scr, s_a, s_b):
    qi = pl.program_id(2)
    tk = ATTN_BLOCK

    @pl.when(qi == 0)
    def _():
        _fill_vt(v_ref, vt_scr)

    lo = lax.broadcasted_iota(jnp.int32, (1, LANES), 1) < FOX_DH
    qf = q_ref[...].astype(F32) * (FOX_DH ** -0.5 * LOG2E)
    qa = qa_ref[...].astype(F32)
    qs = [jnp.where(lo, qf, qa).astype(BF16), jnp.where(lo, qa, qf).astype(BF16)]
    k_fns = [lambda k0, n: jnp.where(lo, k_ref[pl.ds(k0, n), :], ka_ref[pl.ds(k0, n), :]),
             lambda k0, n: jnp.where(lo, ka_ref[pl.ds(k0, n), :], k_ref[pl.ds(k0, n), :])]
    vt_fns = [lambda k0, n: vt_scr[0:FOX_DH, pl.ds(k0, n)],
              lambda k0, n: vt_scr[FOX_DH:2 * FOX_DH, pl.ds(k0, n)]]
    (acc0, l0), (acc1, l1) = _flash_t(qs, k_fns, vt_fns, qi, k_ref.shape[0] // tk, s_a, s_b)
    o_t = jnp.concatenate([acc0 / l0, acc1 / l1], axis=0)
    o_ref[...] = o_t.T.astype(o_ref.dtype)


def _fox_call(proj, qa, ka, B, S):
    T = B * S
    tq = ATTN_BLOCK
    nq = S // tq
    return pl.pallas_call(
        _fox_kernel,
        grid=(B, FOX_HEADS // 2, nq),
        in_specs=[pl.BlockSpec((tq, LANES), lambda b, p, i: (b * nq + i, COL_FQ + p)),
                  pl.BlockSpec((tq, LANES), lambda b, p, i: (b * nq + i, p)),
                  pl.BlockSpec((S, LANES), lambda b, p, i: (b, COL_FK + p)),
                  pl.BlockSpec((S, LANES), lambda b, p, i: (b, p)),
                  pl.BlockSpec((S, LANES), lambda b, p, i: (b, COL_FV + p))],
        out_specs=pl.BlockSpec((tq, LANES), lambda b, p, i: (b * nq + i, p)),
        out_shape=jax.ShapeDtypeStruct((T, BRANCH_WIDTH), BF16),
        scratch_shapes=[pltpu.VMEM((LANES, S), BF16)] + [pltpu.VMEM((2, tq, tq), F32)] * 2,
        compiler_params=_params(("parallel", "parallel", "arbitrary")),
        name="fox_attn",
    )(proj, qa, proj, ka, proj)


def _diff_kernel(q_ref, k_ref, v_ref, lam_ref, ng_ref, o_ref, vt_scr, s_a, s_b):
    qi = pl.program_id(2)
    tk = ATTN_BLOCK

    @pl.when(qi == 0)
    def _():
        _fill_vt(v_ref, vt_scr)

    lo = lax.broadcasted_iota(jnp.int32, (1, LANES), 1) < DIFF_DH
    qf = q_ref[...].astype(F32) * (DIFF_DH ** -0.5 * LOG2E)
    qs = [jnp.where(lo, qf, 0.0).astype(BF16), jnp.where(lo, 0.0, qf).astype(BF16)]
    k_fns = [lambda k0, n: k_ref[pl.ds(k0, n), :]] * 2
    vt_fns = [lambda k0, n: vt_scr[:, pl.ds(k0, n)]] * 2
    (acc0, l0), (acc1, l1) = _flash_t(qs, k_fns, vt_fns, qi, k_ref.shape[0] // tk, s_a, s_b)
    o = (acc0 / l0 - lam_ref[0:1, 0:1] * (acc1 / l1)).T
    ms = jnp.mean(o * o, axis=-1, keepdims=True)
    o_ref[...] = (o * lax.rsqrt(ms + RMS_EPS) * ng_ref[...]).astype(o_ref.dtype)


def _diff_call(proj, lam_row, norm_row, B, S):
    T = B * S
    tq = ATTN_BLOCK
    nq = S // tq
    return pl.pallas_call(
        _diff_kernel,
        grid=(B, DIFF_HEADS, nq),
        in_specs=[pl.BlockSpec((tq, LANES), lambda b, h, i: (b * nq + i, COL_DQ + h)),
                  pl.BlockSpec((S, LANES), lambda b, h, i: (b, COL_DK + h)),
                  pl.BlockSpec((S, LANES), lambda b, h, i: (b, COL_DV + h)),
                  pl.BlockSpec((1, LANES), lambda b, h, i: (0, 0)),
                  pl.BlockSpec((1, LANES), lambda b, h, i: (0, 0))],
        out_specs=pl.BlockSpec((tq, LANES), lambda b, h, i: (b * nq + i, h)),
        out_shape=jax.ShapeDtypeStruct((T, BRANCH_WIDTH), BF16),
        scratch_shapes=[pltpu.VMEM((LANES, S), BF16)] + [pltpu.VMEM((2, tq, tq), F32)] * 2,
        compiler_params=_params(("parallel", "parallel", "arbitrary")),
        name="diff_attn",
    )(proj, proj, proj, lam_row, norm_row)


def _merge_kernel(x_ref, ya_ref, yb_ref, yc_ref, wg_ref, wb_ref, wo_ref, g1_ref, b1_ref, wr_ref, br_ref,
                  x1_ref, ids_ref, wts_ref, *, alpha):
    x = x_ref[...]
    xb = x.astype(BF16)
    merged = None
    for r, y_ref in enumerate((ya_ref, yb_ref, yc_ref)):
        gate = jax.nn.sigmoid(_dot(xb, wg_ref[:, r * D_MODEL:(r + 1) * D_MODEL]))
        term = gate * _dot(y_ref[...], wb_ref[r])
        merged = term if merged is None else merged + term
    h = _dot(merged.astype(BF16), wo_ref[...])
    x1 = _layer_norm(alpha * x + h, g1_ref[...], b1_ref[...])
    x1_ref[...] = x1

    logits = _dot(x1.astype(BF16), wr_ref[...]) + br_ref[...]
    lane = lax.broadcasted_iota(jnp.int32, logits.shape, 1)
    lane_f = lane.astype(F32)
    is_group = lane < N_GROUPS
    gl = jnp.where(is_group, logits, NEG_INF)
    gmax = jnp.max(gl, axis=-1, keepdims=True)
    gsum = jnp.sum(jnp.where(is_group, jnp.exp(gl - gmax), 0.0), axis=-1, keepdims=True)
    g_p = 1.0 / gsum
    g_idx = jnp.min(jnp.where(gl == gmax, lane_f, float(LANES)), axis=-1, keepdims=True)
    lo = N_GROUPS + EXPERTS_PER_GROUP * g_idx
    in_group = (lane_f >= lo) & (lane_f < lo + EXPERTS_PER_GROUP)
    el = jnp.where(in_group, logits, NEG_INF)
    v1 = jnp.max(el, axis=-1, keepdims=True)
    i1 = jnp.min(jnp.where(el == v1, lane_f, float(LANES)), axis=-1, keepdims=True)
    el2 = jnp.where(lane_f == i1, NEG_INF, el)
    v2 = jnp.max(el2, axis=-1, keepdims=True)
    i2 = jnp.min(jnp.where(el2 == v2, lane_f, float(LANES)), axis=-1, keepdims=True)
    t = jnp.exp(v2 - v1)
    w1 = g_p / (1.0 + t)
    w2 = g_p * t / (1.0 + t)
    ids = jnp.where(lane == 0, i1 - N_GROUPS, jnp.where(lane == 1, i2 - N_GROUPS, 0.0))
    ids_ref[...] = ids.astype(jnp.int32)
    wts_ref[...] = jnp.where(lane == 0, w1, jnp.where(lane == 1, w2, 0.0))


def _merge_call(x, ya, yb, yc, w_gates, w_branch, w_out, ln_g, ln_b, w_router, b_router, alpha, tm):
    T, D = x.shape
    row = lambda i: (i, 0)
    const2 = lambda i: (0, 0)
    return pl.pallas_call(
        functools.partial(_merge_kernel, alpha=alpha),
        grid=(T // tm,),
        in_specs=[pl.BlockSpec((tm, D), row),
                  pl.BlockSpec((tm, BRANCH_WIDTH), row),
                  pl.BlockSpec((tm, BRANCH_WIDTH), row),
                  pl.BlockSpec((tm, BRANCH_WIDTH), row),
                  pl.BlockSpec((D, N_BRANCHES * D), const2),
                  pl.BlockSpec((N_BRANCHES, BRANCH_WIDTH, D), lambda i: (0, 0, 0)),
                  pl.BlockSpec((D, D), const2),
                  pl.BlockSpec((1, D), const2),
                  pl.BlockSpec((1, D), const2),
                  pl.BlockSpec((D, LANES), const2),
                  pl.BlockSpec((1, LANES), const2)],
        out_specs=[pl.BlockSpec((tm, D), row),
                   pl.BlockSpec((tm, LANES), row),
                   pl.BlockSpec((tm, LANES), row)],
        out_shape=[jax.ShapeDtypeStruct((T, D), F32),
                   jax.ShapeDtypeStruct((T, LANES), jnp.int32),
                   jax.ShapeDtypeStruct((T, LANES), F32)],
        compiler_params=_params(("parallel",)),
        name="merge_ln1_router",
    )(x, ya, yb, yc, w_gates, w_branch, w_out, ln_g.reshape(1, D), ln_b.reshape(1, D), w_router, b_router)


def _row_copy(src, src_row, dst, dst_row, sem):
    return pltpu.make_async_copy(src.at[pl.ds(src_row, 1), :], dst.at[pl.ds(dst_row, 1), :], sem)


def _dispatch_kernel(pos_ref, pad_ref, x_hbm, xs_hbm, zero_scr, sem, *, tm, npad, pad_steps, nsteps):
    i = pl.program_id(0)

    @pl.when(i == 0)
    def _():
        zero_scr[...] = jnp.zeros_like(zero_scr)

    def issue(r, carry):
        _row_copy(x_hbm, i * tm + r, xs_hbm, pos_ref[2 * r], sem).start()
        _row_copy(x_hbm, i * tm + r, xs_hbm, pos_ref[2 * r + 1], sem).start()
        return carry

    lax.fori_loop(0, tm, issue, 0, unroll=8)

    @pl.when(i < pad_steps)
    def _():
        def issue_pad(r, carry):
            _row_copy(zero_scr, r, xs_hbm, pad_ref[r], sem).start()
            return carry

        lax.fori_loop(0, npad, issue_pad, 0, unroll=8)

    def retire_rows():
        for _ in range(2):
            pltpu.make_async_copy(x_hbm.at[pl.ds(0, tm), :], xs_hbm.at[pl.ds(0, tm), :], sem).wait()

    def retire_pad():
        pltpu.make_async_copy(zero_scr, xs_hbm.at[pl.ds(0, npad), :], sem).wait()

    @pl.when(i > 0)
    def _():
        retire_rows()

    @pl.when((i > 0) & (i <= pad_steps))
    def _():
        retire_pad()

    @pl.when(i == nsteps - 1)
    def _():
        retire_rows()
        if pad_steps == nsteps:
            retire_pad()


def _dispatch_call(pos, pad_rows, x1, n_rows, tm):
    T, D = x1.shape
    nsteps = T // tm
    npad = max(LANES, pad_rows.shape[0] // nsteps)
    pad_steps = pad_rows.shape[0] // npad
    return pl.pallas_call(
        functools.partial(_dispatch_kernel, tm=tm, npad=npad, pad_steps=pad_steps, nsteps=nsteps),
        grid=(nsteps,),
        in_specs=[pl.BlockSpec((2 * tm,), lambda i: (i,), memory_space=pltpu.SMEM),
                  pl.BlockSpec((npad,), lambda i: (jnp.minimum(i, pad_steps - 1),), memory_space=pltpu.SMEM),
                  pl.BlockSpec(memory_space=pl.ANY)],
        out_specs=pl.BlockSpec(memory_space=pl.ANY),
        out_shape=jax.ShapeDtypeStruct((n_rows, D), F32),
        scratch_shapes=[pltpu.VMEM((npad, D), F32), pltpu.SemaphoreType.DMA(())],
        compiler_params=_params(("arbitrary",)),
        name="moe_dispatch",
    )(pos, pad_rows, x1)


def _expert_kernel(te_ref, nu_ref, xs_ref, wg_ref, wu_ref, wd_ref, o_ref, wgu_scr, wd_scr):
    t = pl.program_id(0)
    used = t < nu_ref[0]
    new_expert = (t == 0) | (te_ref[t] != te_ref[jnp.maximum(t - 1, 0)])

    @pl.when(used & new_expert)
    def _():
        wgu_scr[:, 0:D_EXPERT] = wg_ref[0].astype(BF16)
        wgu_scr[:, D_EXPERT:2 * D_EXPERT] = wu_ref[0].astype(BF16)
        wd_scr[...] = wd_ref[0].astype(BF16)

    @pl.when(used)
    def _():
        gu = _dot(xs_ref[...].astype(BF16), wgu_scr[...])
        g = gu[:, 0:D_EXPERT]
        h = (g * jax.nn.sigmoid(g)) * gu[:, D_EXPERT:2 * D_EXPERT]
        o_ref[...] = _dot(h.astype(BF16), wd_scr[...])

    @pl.when(jnp.logical_not(used))
    def _():
        o_ref[...] = jnp.zeros_like(o_ref)


def _expert_call(tile_expert, n_used, xs, n_tiles, w_gate, w_up, w_down):
    D = xs.shape[1]
    tm = EXPERT_TILE
    used = lambda t, te, nu: (jnp.minimum(t, nu[0] - 1), 0)
    grid_spec = pltpu.PrefetchScalarGridSpec(
        num_scalar_prefetch=2,
        grid=(n_tiles,),
        in_specs=[pl.BlockSpec((tm, D), used),
                  pl.BlockSpec((1, D, D_EXPERT), lambda t, te, nu: (te[t], 0, 0)),
                  pl.BlockSpec((1, D, D_EXPERT), lambda t, te, nu: (te[t], 0, 0)),
                  pl.BlockSpec((1, D_EXPERT, D), lambda t, te, nu: (te[t], 0, 0))],
        out_specs=pl.BlockSpec((tm, D), lambda t, te, nu: (t, 0)),
        scratch_shapes=[pltpu.VMEM((D, 2 * D_EXPERT), BF16), pltpu.VMEM((D_EXPERT, D), BF16)],
    )
    return pl.pallas_call(
        _expert_kernel,
        grid_spec=grid_spec,
        out_shape=jax.ShapeDtypeStruct((n_tiles * tm, D), F32),
        compiler_params=_params(("arbitrary",)),
        name="experts",
    )(tile_expert, n_used, xs, w_gate, w_up, w_down)


def _combine_kernel(pos_ref, posn_ref, x_ref, w_ref, g_ref, b_ref, y_hbm, o_ref, ybuf, sems, *, alpha, tm, nsteps):
    i = pl.program_id(0)
    slot = lax.rem(i, 2)
    other = 1 - slot

    def issue_tile(idx_ref, s):
        def issue(r, carry):
            _row_copy(y_hbm, idx_ref[2 * r], ybuf.at[s, 0], r, sems.at[s]).start()
            _row_copy(y_hbm, idx_ref[2 * r + 1], ybuf.at[s, 1], r, sems.at[s]).start()
            return carry

        lax.fori_loop(0, tm, issue, 0, unroll=8)

    def wait_tile(s):
        for k in range(2):
            pltpu.make_async_copy(y_hbm.at[pl.ds(0, tm), :], ybuf.at[s, k], sems.at[s]).wait()

    @pl.when(i == 0)
    def _():
        issue_tile(pos_ref, 0)

    wait_tile(slot)
    w = w_ref[...]
    u = alpha * x_ref[...] + w[:, 0:1] * ybuf[slot, 0] + w[:, 1:2] * ybuf[slot, 1]
    o_ref[...] = _layer_norm(u, g_ref[...], b_ref[...])
    issue_tile(posn_ref, other)

    @pl.when(i == nsteps - 1)
    def _():
        wait_tile(other)


def _combine_call(pos, x1, wts, ln_g, ln_b, y, alpha, tm):
    T, D = x1.shape
    nsteps = T // tm
    row = lambda i: (i, 0)
    const2 = lambda i: (0, 0)
    return pl.pallas_call(
        functools.partial(_combine_kernel, alpha=alpha, tm=tm, nsteps=nsteps),
        grid=(nsteps,),
        in_specs=[pl.BlockSpec((2 * tm,), lambda i: (i,), memory_space=pltpu.SMEM),
                  pl.BlockSpec((2 * tm,), lambda i: (jnp.minimum(i + 1, nsteps - 1),), memory_space=pltpu.SMEM),
                  pl.BlockSpec((tm, D), row),
                  pl.BlockSpec((tm, LANES), row),
                  pl.BlockSpec((1, D), const2), pl.BlockSpec((1, D), const2),
                  pl.BlockSpec(memory_space=pl.ANY)],
        out_specs=pl.BlockSpec((tm, D), row),
        out_shape=jax.ShapeDtypeStruct((T, D), F32),
        scratch_shapes=[pltpu.VMEM((2, 2, tm, D), F32), pltpu.SemaphoreType.DMA((2,))],
        compiler_params=_params(("arbitrary",)),
        name="combine_ln2",
    )(pos, pos, x1, wts, ln_g.reshape(1, D), ln_b.reshape(1, D), y)


def _rope_tables(positions):
    half = ROPE_DIM // 2
    inv_freq = ROPE_THETA ** (-jnp.arange(0, ROPE_DIM, 2, dtype=F32) / ROPE_DIM)
    ang = positions.astype(F32).reshape(-1, 1) * inv_freq[None, :]
    cos, sin = jnp.cos(ang), jnp.sin(ang)
    lane = jnp.arange(LANES)
    in_head = lane % DIFF_DH
    freq = in_head % half
    first = in_head < half
    second = (in_head >= half) & (in_head < ROPE_DIM)
    cosf = jnp.where((first | second)[None, :], cos[:, freq], 1.0)
    sin_a = jnp.where(second[None, :], sin[:, freq], 0.0)
    sin_b = jnp.where(first[None, :], -sin[:, freq], 0.0)
    return cosf, sin_a, sin_b


def _dispatch_plan(ids, T):
    tm = EXPERT_TILE
    flat = ids.reshape(-1)
    onehot = (flat[:, None] == jnp.arange(N_EXPERTS, dtype=jnp.int32)[None, :]).astype(jnp.int32)
    csum = jnp.cumsum(onehot, axis=0)
    counts = csum[-1]
    rank = jnp.take_along_axis(csum, flat[:, None], axis=1)[:, 0] - 1
    padded = ((counts + tm - 1) // tm) * tm
    ends = jnp.cumsum(padded)
    starts = ends - padded
    pos = (starts[flat] + rank).astype(jnp.int32)
    n_slab = 2 * T + N_EXPERTS * tm
    n_tiles = n_slab // tm
    tile_start = jnp.arange(n_tiles, dtype=jnp.int32) * tm
    tile_expert = jnp.minimum(jnp.sum((tile_start[:, None] >= ends[None, :]).astype(jnp.int32), axis=1),
                              N_EXPERTS - 1).astype(jnp.int32)
    n_used = (ends[-1] // tm).astype(jnp.int32).reshape(1)
    gap = padded - counts
    gap_end = jnp.cumsum(gap)
    gap_start = gap_end - gap
    j = jnp.arange(N_EXPERTS * tm, dtype=jnp.int32)
    e = jnp.minimum(jnp.sum((j[:, None] >= gap_end[None, :]).astype(jnp.int32), axis=1), N_EXPERTS - 1)
    in_tile = starts[e] + counts[e] + (j - gap_start[e])
    pad_rows = jnp.where(j < gap_end[-1], in_tile, ends[-1] + (j - gap_end[-1]))
    return pos, tile_expert, n_used, pad_rows.astype(jnp.int32), n_tiles


def _layer(x, cosf, sina, sinb, B, S, lb, p, alpha, lam_init):
    T = B * S
    tm = min(512, T)
    w_in = p["w_in"]
    w_main = jnp.concatenate([w_in[:, :3584], w_in[:, 3592:5128]], axis=1).astype(BF16)
    w_ff = jnp.zeros((D_MODEL, LANES), BF16).at[:, :FOX_HEADS].set(w_in[:, 3584:3592].astype(BF16))
    w_gates = w_in[:, 5128:].astype(BF16)

    proj, ff = _in_proj_call(x, w_main, w_ff, cosf, sina, sinb, tm)
    ya = _hgrn_call(proj, lb, p["hgrn_norm_g"], B, S)
    qa, ka = _fox_gate_call(ff, p["fox_f_bias"], B, S)
    yb = _fox_call(proj, qa, ka, B, S)
    lv = p["diff_lambda"].astype(F32)
    lam = jnp.exp(jnp.sum(lv[0] * lv[1])) - jnp.exp(jnp.sum(lv[2] * lv[3])) + lam_init
    lam_row = jnp.full((1, LANES), lam, F32)
    norm_row = (p["diff_norm_g"].astype(F32) * (1.0 - lam_init)).reshape(1, DIFF_DV)
    yc = _diff_call(proj, lam_row, norm_row, B, S)

    w_router = jnp.zeros((D_MODEL, LANES), F32)
    w_router = w_router.at[:, :N_GROUPS].set(p["router_g_w"]).at[:, N_GROUPS:N_GROUPS + N_EXPERTS].set(p["router_e_w"])
    b_router = jnp.zeros((1, LANES), F32)
    b_router = b_router.at[0, :N_GROUPS].set(p["router_g_b"]).at[0, N_GROUPS:N_GROUPS + N_EXPERTS].set(
        p["router_e_b"].reshape(-1))
    x1, ids, wts = _merge_call(x, ya, yb, yc, w_gates, p["w_branch"].astype(BF16), p["w_out"].astype(BF16),
                               p["ln1_g"], p["ln1_b"], w_router.astype(BF16), b_router, alpha, tm)

    pos, tile_expert, n_used, pad_rows, n_tiles = _dispatch_plan(ids[:, :2], T)
    xs = _dispatch_call(pos, pad_rows, x1, n_tiles * EXPERT_TILE, tm)
    y = _expert_call(tile_expert, n_used, xs, n_tiles, p["expert_w_gate"], p["expert_w_up"], p["expert_w_down"])
    return _combine_call(pos, x1, wts, p["ln2_g"], p["ln2_b"], y, alpha, tm)


def kernel(x, positions, ln_in_g, ln_in_b, w_in, hgrn_lb_logits, hgrn_norm_g, fox_f_bias, diff_lambda,
           diff_norm_g, w_branch, w_out, ln1_g, ln1_b, router_g_w, router_g_b, router_e_w, router_e_b,
           expert_w_gate, expert_w_up, expert_w_down, ln2_g, ln2_b):
    B, S, D = x.shape
    T = B * S
    depth = w_in.shape[0]
    alpha = (2 * depth) ** 0.25
    cosf, sina, sinb = _rope_tables(positions)
    lb_soft = jax.nn.softmax(hgrn_lb_logits.astype(F32), axis=0)
    lower_bounds = jnp.maximum(jnp.cumsum(lb_soft, axis=0) - lb_soft[0], 0.0)

    h = _ln_call(x.reshape(T, D), ln_in_g, ln_in_b, min(512, T))
    for l in range(depth):
        p = dict(w_in=w_in[l], hgrn_norm_g=hgrn_norm_g[l], fox_f_bias=fox_f_bias[l], diff_lambda=diff_lambda[l],
                 diff_norm_g=diff_norm_g[l], w_branch=w_branch[l], w_out=w_out[l], ln1_g=ln1_g[l], ln1_b=ln1_b[l],
                 router_g_w=router_g_w[l], router_g_b=router_g_b[l], router_e_w=router_e_w[l],
                 router_e_b=router_e_b[l], expert_w_gate=expert_w_gate[l], expert_w_up=expert_w_up[l],
                 expert_w_down=expert_w_down[l], ln2_g=ln2_g[l], ln2_b=ln2_b[l])
        lam_init = 0.8 - 0.6 * float(math.exp(-0.3 * l))
        h = _layer(h, cosf, sina, sinb, B, S, lower_bounds[l], p, alpha, lam_init)
    return h.reshape(B, S, D)
```

```python
import functools
import math

import numpy as np
import jax
import jax.numpy as jnp
from jax import lax
from jax.experimental import pallas as pl
from jax.experimental.pallas import tpu as pltpu

F32 = jnp.float32
BF16 = jnp.bfloat16

D_MODEL = 1024
HG_HEADS, HG_D = 4, 128
FOX_HEADS, FOX_DH = 8, 64
DIFF_HEADS, DIFF_DH, DIFF_DV = 4, 64, 128
BRANCH_WIDTH = 512
N_BRANCHES = 3
ROPE_THETA = 500000.0
ROPE_DIM = DIFF_DH // 4
N_GROUPS, EXPERTS_PER_GROUP = 4, 8
N_EXPERTS = N_GROUPS * EXPERTS_PER_GROUP
D_EXPERT = 512
LN_EPS = 1e-5
RMS_EPS = 1e-6
NEG_INF = -1e30
EXP_CLAMP = 60.0
LOG2E = 1.4426950408889634

LANES = 128
N_MAIN = 5120
COL_HQ, COL_HF, COL_HI, COL_HG = 0, 4, 8, 12
COL_FQ, COL_FK, COL_FV = 16, 20, 24
COL_DQ, COL_DK, COL_DV = 28, 32, 36
ROPE_TILES = (7, 8)

HG_CHUNK = 64
HG_SUB = 8
HG_BATCH = 2
GATE_CHUNK = 256
ATTN_BLOCK = 512
EXPERT_TILE = 256
VMEM_LIMIT = 56 * 1024 * 1024


def _dot(a, b):
    return jnp.dot(a, b, preferred_element_type=F32)


def _dot_nt(a, b):
    return lax.dot_general(a, b, (((1,), (1,)), ((), ())), preferred_element_type=F32)


def _log_sigmoid(z):
    return jnp.minimum(z, 0.0) - jnp.log1p(jnp.exp(-jnp.abs(z)))


def _split3(x):
    h1 = x.astype(BF16)
    r1 = x - h1.astype(F32)
    h2 = r1.astype(BF16)
    h3 = (r1 - h2.astype(F32)).astype(BF16)
    return h1, h2, h3


def _cumsum_rows(tri, x):
    h1, h2, h3 = _split3(x)
    return _dot(tri, h1) + _dot(tri, h2) + _dot(tri, h3)


def _layer_norm(u, g, b):
    mu = jnp.mean(u, axis=-1, keepdims=True)
    d = u - mu
    var = jnp.mean(d * d, axis=-1, keepdims=True)
    return d * lax.rsqrt(var + LN_EPS) * g + b


def _params(sem):
    return pltpu.CompilerParams(dimension_semantics=sem, vmem_limit_bytes=VMEM_LIMIT)


def _ln_kernel(x_ref, g_ref, b_ref, o_ref):
    o_ref[...] = _layer_norm(x_ref[...], g_ref[...], b_ref[...])


def _ln_call(x, g, b, tm):
    T, D = x.shape
    return pl.pallas_call(
        _ln_kernel,
        grid=(T // tm,),
        in_specs=[pl.BlockSpec((tm, D), lambda i: (i, 0)),
                  pl.BlockSpec((1, D), lambda i: (0, 0)),
                  pl.BlockSpec((1, D), lambda i: (0, 0))],
        out_specs=pl.BlockSpec((tm, D), lambda i: (i, 0)),
        out_shape=jax.ShapeDtypeStruct((T, D), F32),
        compiler_params=_params(("parallel",)),
        name="ln_in",
    )(x, g.reshape(1, D), b.reshape(1, D))


def _in_proj_kernel(x_ref, w_ref, wff_ref, cos_ref, sa_ref, sb_ref, o_ref, ff_ref):
    xb = x_ref[...].astype(BF16)
    ff_ref[...] = _dot(xb, wff_ref[...])
    for j in range(N_MAIN // 512):
        acc = _dot(xb, w_ref[:, j * 512:(j + 1) * 512])
        if j in ROPE_TILES:
            cosf, sa, sb = cos_ref[...], sa_ref[...], sb_ref[...]
            for g in range(4):
                t = acc[:, g * LANES:(g + 1) * LANES]
                r = t * cosf + pltpu.roll(t, 8, 1) * sa + pltpu.roll(t, LANES - 8, 1) * sb
                o_ref[:, j * 512 + g * LANES:j * 512 + (g + 1) * LANES] = r.astype(BF16)
        else:
            o_ref[:, j * 512:(j + 1) * 512] = acc.astype(BF16)


def _in_proj_call(x, w_main, w_ff, cosf, sina, sinb, tm):
    T, D = x.shape
    const = lambda i: (0, 0)
    row = lambda i: (i, 0)
    return pl.pallas_call(
        _in_proj_kernel,
        grid=(T // tm,),
        in_specs=[pl.BlockSpec((tm, D), row),
                  pl.BlockSpec((D, N_MAIN), const),
                  pl.BlockSpec((D, LANES), const),
                  pl.BlockSpec((tm, LANES), row),
                  pl.BlockSpec((tm, LANES), row),
                  pl.BlockSpec((tm, LANES), row)],
        out_specs=[pl.BlockSpec((tm, N_MAIN), row),
                   pl.BlockSpec((tm, LANES), row)],
        out_shape=[jax.ShapeDtypeStruct((T, N_MAIN), BF16),
                   jax.ShapeDtypeStruct((T, LANES), F32)],
        compiler_params=_params(("parallel",)),
        name="in_proj",
    )(x, w_main, w_ff, cosf, sina, sinb)


def _hgrn_chunk(z, ql, v, gl, lb, ng, state_t, b_scr, g_scr, consts):
    C = HG_CHUNK
    tri, levels, diag_masks, lane_c = consts
    u = jnp.exp(-jnp.abs(z))
    log1pu = jnp.log(1.0 + u)
    log_f = (jnp.minimum(z, 0.0) - log1pu) + jnp.log(1.0 + lb * jnp.exp(jnp.minimum(-z, EXP_CLAMP)))
    log2k = (jnp.minimum(-z, 0.0) - log1pu) * LOG2E + jnp.log2(1.0 - lb)
    k = jnp.exp2(log2k)
    q = ql * jax.nn.sigmoid(ql)
    b2 = _cumsum_rows(tri, log_f) * LOG2E
    b_scr[...] = b2
    g_scr[...] = log2k - b2
    vb = v.astype(BF16)
    b_last = b_scr[C - 1:C, :]

    o = _dot_nt((q * jnp.exp2(b2)).astype(BF16), state_t.astype(BF16))

    scores = jnp.zeros((C, C), F32)
    for m, is_query, pair in levels:
        pieces = [jnp.broadcast_to(b_scr[p * 2 * m + m - 1:p * 2 * m + m, :], (2 * m, LANES))
                  for p in range(C // (2 * m))]
        b_ref_rows = pieces[0] if len(pieces) == 1 else jnp.concatenate(pieces, axis=0)
        decay = jnp.exp2(-jnp.abs(b2 - b_ref_rows))
        qd = jnp.where(is_query, q * decay, 0.0).astype(BF16)
        kd = jnp.where(is_query, 0.0, k * decay).astype(BF16)
        scores = scores + jnp.where(pair, _dot_nt(qd, kd), 0.0)

    diag = []
    for blk in range(C // HG_SUB):
        lo = blk * HG_SUB
        bb = b2[lo:lo + HG_SUB]
        qq = q[lo:lo + HG_SUB]
        blk_scores = jnp.zeros((HG_SUB, C), F32)
        for s in range(HG_SUB):
            w = qq * jnp.exp2(bb + g_scr[lo + s:lo + s + 1, :])
            blk_scores = jnp.where(lane_c == lo + s, jnp.sum(w, axis=-1, keepdims=True), blk_scores)
        diag.append(jnp.where(diag_masks[blk], blk_scores, 0.0))
    scores = scores + jnp.concatenate(diag, axis=0)
    o = o + _dot(scores.astype(BF16), vb)

    k_dec = (k * jnp.exp2(b_last - b2)).astype(BF16)
    state_t = state_t * jnp.exp2(b_last) + _dot(v.T.astype(BF16), k_dec)

    ms = jnp.mean(o * o, axis=-1, keepdims=True)
    y = o * lax.rsqrt(ms + RMS_EPS) * ng * (gl * jax.nn.sigmoid(gl))
    return y, state_t


def _hgrn_kernel(q_ref, f_ref, i_ref, g_ref, lb_ref, ng_ref, o_ref, b_scr, g_scr, *, nchunks):
    C = HG_CHUNK
    ng = ng_ref[...]
    ri = lax.broadcasted_iota(jnp.int32, (C, C), 0)
    ci = lax.broadcasted_iota(jnp.int32, (C, C), 1)
    tri = jnp.where(ri >= ci, 1.0, 0.0).astype(BF16)
    rows = lax.broadcasted_iota(jnp.int32, (C, LANES), 0)
    sub_rows = lax.broadcasted_iota(jnp.int32, (HG_SUB, C), 0)
    lane_c = lax.broadcasted_iota(jnp.int32, (HG_SUB, C), 1)
    diag_masks = [sub_rows + blk * HG_SUB >= lane_c for blk in range(C // HG_SUB)]
    levels = []
    m = HG_SUB
    while m < C:
        shift = int(math.log2(2 * m))
        is_query = (rows & (2 * m - 1)) >= m
        pair = ((ri >> shift) == (ci >> shift)) & ((ri & (2 * m - 1)) >= m) & ((ci & (2 * m - 1)) < m)
        levels.append((m, is_query, pair))
        m *= 2
    consts = (tri, levels, diag_masks, lane_c)

    nb = q_ref.shape[0]

    def body(n, states):
        r0 = pl.multiple_of(n * C, C)
        new_states = []
        for bi in range(nb):
            for h in range(HG_HEADS):
                cols = slice(h * LANES, (h + 1) * LANES)
                y, st = _hgrn_chunk(f_ref[bi, pl.ds(r0, C), cols].astype(F32), q_ref[bi, pl.ds(r0, C), cols].astype(F32),
                                    i_ref[bi, pl.ds(r0, C), cols].astype(F32), g_ref[bi, pl.ds(r0, C), cols].astype(F32),
                                    lb_ref[:, cols], ng, states[bi * HG_HEADS + h],
                                    b_scr.at[bi * HG_HEADS + h], g_scr.at[bi * HG_HEADS + h], consts)
                o_ref[bi, pl.ds(r0, C), cols] = y.astype(o_ref.dtype)
                new_states.append(st)
        return tuple(new_states)

    lax.fori_loop(0, nchunks, body, tuple(jnp.zeros((HG_D, HG_D), F32) for _ in range(nb * HG_HEADS)))


def _hgrn_call(proj, lb, norm_g, B, S):
    W = HG_HEADS * HG_D
    nb = HG_BATCH if B % HG_BATCH == 0 else 1
    proj3 = proj.reshape(B, S, N_MAIN)
    blk = lambda off: pl.BlockSpec((nb, S, W), lambda b, off=off: (b, 0, off))
    out = pl.pallas_call(
        functools.partial(_hgrn_kernel, nchunks=S // HG_CHUNK),
        grid=(B // nb,),
        in_specs=[blk(COL_HQ // 4), blk(COL_HF // 4), blk(COL_HI // 4), blk(COL_HG // 4),
                  pl.BlockSpec((1, W), lambda b: (0, 0)),
                  pl.BlockSpec((1, LANES), lambda b: (0, 0))],
        out_specs=pl.BlockSpec((nb, S, W), lambda b: (b, 0, 0)),
        out_shape=jax.ShapeDtypeStruct((B, S, BRANCH_WIDTH), BF16),
        scratch_shapes=[pltpu.VMEM((nb * HG_HEADS, HG_CHUNK, LANES), F32)] * 2,
        compiler_params=_params(("parallel",)),
        name="hgrn",
    )(proj3, proj3, proj3, proj3, lb.reshape(1, W), norm_g.reshape(1, HG_D))
    return out.reshape(B * S, BRANCH_WIDTH)


def _fox_aug_tables():
    sel_q = np.zeros((3, LANES, BRANCH_WIDTH), np.float32)
    sel_k = np.zeros((3, LANES, BRANCH_WIDTH), np.float32)
    ones_q = np.zeros((1, BRANCH_WIDTH), np.float32)
    ones_k = np.zeros((1, BRANCH_WIDTH), np.float32)
    for col in range(BRANCH_WIDTH):
        pair, within = divmod(col, LANES)
        half, slot = divmod(within, FOX_DH)
        head = 2 * pair + 1 - half
        if slot < 3:
            sel_q[slot, head, col] = 1.0
            ones_k[0, col] = 1.0
        elif slot < 6:
            sel_k[slot - 3, head, col] = -1.0
            ones_q[0, col] = 1.0
    return sel_q, sel_k, ones_q, ones_k


def _fox_gate_kernel(ff_ref, bias_ref, selq_ref, selk_ref, oq_ref, ok_ref, qa_ref, ka_ref, *, nchunks):
    CH = GATE_CHUNK
    ri = lax.broadcasted_iota(jnp.int32, (CH, CH), 0)
    ci = lax.broadcasted_iota(jnp.int32, (CH, CH), 1)
    tri = jnp.where(ri >= ci, 1.0, 0.0).astype(BF16)
    bias = bias_ref[...]

    def body(n, carry):
        r0 = pl.multiple_of(n * CH, CH)
        c = carry + _cumsum_rows(tri, _log_sigmoid(ff_ref[pl.ds(r0, CH), :] + bias))
        parts = _split3(c * LOG2E)
        qa = oq_ref[...] + _dot(parts[0], selq_ref[0]) + _dot(parts[1], selq_ref[1]) + _dot(parts[2], selq_ref[2])
        ka = ok_ref[...] + _dot(parts[0], selk_ref[0]) + _dot(parts[1], selk_ref[1]) + _dot(parts[2], selk_ref[2])
        qa_ref[pl.ds(r0, CH), :] = qa.astype(BF16)
        ka_ref[pl.ds(r0, CH), :] = ka.astype(BF16)
        return c[CH - 1:CH, :]

    lax.fori_loop(0, nchunks, body, jnp.zeros((1, LANES), F32))


def _fox_gate_call(ff, bias, B, S):
    T = B * S
    bias_row = jnp.zeros((1, LANES), F32).at[0, :FOX_HEADS].set(bias.astype(F32))
    sel_q, sel_k, ones_q, ones_k = _fox_aug_tables()
    const2 = lambda b: (0, 0)
    const3 = lambda b: (0, 0, 0)
    return pl.pallas_call(
        functools.partial(_fox_gate_kernel, nchunks=S // GATE_CHUNK),
        grid=(B,),
        in_specs=[pl.BlockSpec((S, LANES), lambda b: (b, 0)),
                  pl.BlockSpec((1, LANES), const2),
                  pl.BlockSpec((3, LANES, BRANCH_WIDTH), const3),
                  pl.BlockSpec((3, LANES, BRANCH_WIDTH), const3),
                  pl.BlockSpec((1, BRANCH_WIDTH), const2),
                  pl.BlockSpec((1, BRANCH_WIDTH), const2)],
        out_specs=[pl.BlockSpec((S, BRANCH_WIDTH), lambda b: (b, 0)),
                   pl.BlockSpec((S, BRANCH_WIDTH), lambda b: (b, 0))],
        out_shape=[jax.ShapeDtypeStruct((T, BRANCH_WIDTH), BF16),
                   jax.ShapeDtypeStruct((T, BRANCH_WIDTH), BF16)],
        compiler_params=_params(("parallel",)),
        name="fox_gate",
    )(ff, bias_row, jnp.asarray(sel_q, BF16), jnp.asarray(sel_k, BF16), jnp.asarray(ones_q), jnp.asarray(ones_k))


def _flash_t(qs, k_fns, vt_fns, n_full, n_blocks, s_a, s_b):
    tq = tk = ATTN_BLOCK
    nmap = len(qs)

    def produce(dst, j):
        for i in range(nmap):
            dst[i] = _dot_nt(k_fns[i](j * tk, tk), qs[i])

    def update(s, m, l, acc, vt):
        m_new = jnp.maximum(m, jnp.max(s, axis=0, keepdims=True))
        alpha = jnp.exp2(m - m_new)
        p = jnp.exp2(s - m_new)
        return m_new, alpha * l + jnp.sum(p, axis=0, keepdims=True), alpha * acc + _dot(vt, p.astype(BF16))

    def consume(src, stats, j):
        return tuple(update(src[i], *stats[i], vt_fns[i](j * tk, tk)) for i in range(nmap))

    h = tk // 2
    mask_a = lax.broadcasted_iota(jnp.int32, (h, tq), 0) <= lax.broadcasted_iota(jnp.int32, (h, tq), 1)
    mask_b = lax.broadcasted_iota(jnp.int32, (h, h), 0) <= lax.broadcasted_iota(jnp.int32, (h, h), 1)

    def produce_diag(dst, j):
        for i in range(nmap):
            dst[i, 0:h, :] = _dot_nt(k_fns[i](j * tk, h), qs[i])
            dst[i, h:tk, h:tq] = _dot_nt(k_fns[i](j * tk + h, h), qs[i][h:tq])

    def consume_diag(src, stats, j):
        out = []
        for i in range(nmap):
            m, l, acc = update(jnp.where(mask_a, src[i, 0:h, :], NEG_INF), *stats[i], vt_fns[i](j * tk, h))
            m_b, l_b, acc_b = update(jnp.where(mask_b, src[i, h:tk, h:tq], NEG_INF), m[:, h:], l[:, h:], acc[:, h:],
                                     vt_fns[i](j * tk + h, h))
            out.append((jnp.concatenate([m[:, :h], m_b], axis=1), jnp.concatenate([l[:, :h], l_b], axis=1),
                        jnp.concatenate([acc[:, :h], acc_b], axis=1)))
        return tuple(out)

    stats = []
    for vt_fn in vt_fns:
        dv = vt_fn(0, tk).shape[0]
        stats.append((jnp.full((1, tq), NEG_INF, F32), jnp.zeros((1, tq), F32), jnp.zeros((dv, tq), F32)))

    def branch(n):
        def run(stats):
            bufs = (s_a, s_b)
            (produce if n > 0 else produce_diag)(bufs[0], 0)
            for j in range(n):
                (produce if j + 1 < n else produce_diag)(bufs[(j + 1) % 2], j + 1)
                stats = consume(bufs[j % 2], stats, j)
            return consume_diag(bufs[n % 2], stats, n)
        return run

    stats = lax.switch(n_full, [branch(n) for n in range(n_blocks)], tuple(stats))
    return [(acc, l) for (_, l, acc) in stats]


def _fill_vt(v_ref, vt_scr):
    S = v_ref.shape[0]
    for c in range(S // ATTN_BLOCK):
        sl = slice(c * ATTN_BLOCK, (c + 1) * ATTN_BLOCK)
        vt_scr[:, sl] = v_ref[sl, :].astype(F32).T.astype(BF16)


def _fox_kernel(q_ref, qa_ref, k_ref, ka_ref, v_ref, o_ref, vt_scr, s_a, s_b):
    qi = pl.program_id(2)
    tk = ATTN_BLOCK

    @pl.when(qi == 0)
    def _():
        _fill_vt(v_ref, vt_scr)

    lo = lax.broadcasted_iota(jnp.int32, (1, LANES), 1) < FOX_DH
    qf = q_ref[...].astype(F32) * (FOX_DH ** -0.5 * LOG2E)
    qa = qa_ref[...].astype(F32)
    qs = [jnp.where(lo, qf, qa).astype(BF16), jnp.where(lo, qa, qf).astype(BF16)]
    k_fns = [lambda k0, n: jnp.where(lo, k_ref[pl.ds(k0, n), :], ka_ref[pl.ds(k0, n), :]),
             lambda k0, n: jnp.where(lo, ka_ref[pl.ds(k0, n), :], k_ref[pl.ds(k0, n), :])]
    vt_fns = [lambda k0, n: vt_scr[0:FOX_DH, pl.ds(k0, n)],
              lambda k0, n: vt_scr[FOX_DH:2 * FOX_DH, pl.ds(k0, n)]]
    (acc0, l0), (acc1, l1) = _flash_t(qs, k_fns, vt_fns, qi, k_ref.shape[0] // tk, s_a, s_b)
    o_t = jnp.concatenate([acc0 / l0, acc1 / l1], axis=0)
    o_ref[...] = o_t.T.astype(o_ref.dtype)


def _fox_call(proj, qa, ka, B, S):
    T = B * S
    tq = ATTN_BLOCK
    nq = S // tq
    return pl.pallas_call(
        _fox_kernel,
        grid=(B, FOX_HEADS // 2, nq),
        in_specs=[pl.BlockSpec((tq, LANES), lambda b, p, i: (b * nq + i, COL_FQ + p)),
                  pl.BlockSpec((tq, LANES), lambda b, p, i: (b * nq + i, p)),
                  pl.BlockSpec((S, LANES), lambda b, p, i: (b, COL_FK + p)),
                  pl.BlockSpec((S, LANES), lambda b, p, i: (b, p)),
                  pl.BlockSpec((S, LANES), lambda b, p, i: (b, COL_FV + p))],
        out_specs=pl.BlockSpec((tq, LANES), lambda b, p, i: (b * nq + i, p)),
        out_shape=jax.ShapeDtypeStruct((T, BRANCH_WIDTH), BF16),
        scratch_shapes=[pltpu.VMEM((LANES, S), BF16)] + [pltpu.VMEM((2, tq, tq), F32)] * 2,
        compiler_params=_params(("parallel", "parallel", "arbitrary")),
        name="fox_attn",
    )(proj, qa, proj, ka, proj)


def _diff_kernel(q_ref, k_ref, v_ref, lam_ref, ng_ref, o_ref, vt_scr, s_a, s_b):
    qi = pl.program_id(2)
    tk = ATTN_BLOCK

    @pl.when(qi == 0)
    def _():
        _fill_vt(v_ref, vt_scr)

    lo = lax.broadcasted_iota(jnp.int32, (1, LANES), 1) < DIFF_DH
    qf = q_ref[...].astype(F32) * (DIFF_DH ** -0.5 * LOG2E)
    qs = [jnp.where(lo, qf, 0.0).astype(BF16), jnp.where(lo, 0.0, qf).astype(BF16)]
    k_fns = [lambda k0, n: k_ref[pl.ds(k0, n), :]] * 2
    vt_fns = [lambda k0, n: vt_scr[:, pl.ds(k0, n)]] * 2
    (acc0, l0), (acc1, l1) = _flash_t(qs, k_fns, vt_fns, qi, k_ref.shape[0] // tk, s_a, s_b)
    o = (acc0 / l0 - lam_ref[0:1, 0:1] * (acc1 / l1)).T
    ms = jnp.mean(o * o, axis=-1, keepdims=True)
    o_ref[...] = (o * lax.rsqrt(ms + RMS_EPS) * ng_ref[...]).astype(o_ref.dtype)


def _diff_call(proj, lam_row, norm_row, B, S):
    T = B * S
    tq = ATTN_BLOCK
    nq = S // tq
    return pl.pallas_call(
        _diff_kernel,
        grid=(B, DIFF_HEADS, nq),
        in_specs=[pl.BlockSpec((tq, LANES), lambda b, h, i: (b * nq + i, COL_DQ + h)),
                  pl.BlockSpec((S, LANES), lambda b, h, i: (b, COL_DK + h)),
                  pl.BlockSpec((S, LANES), lambda b, h, i: (b, COL_DV + h)),
                  pl.BlockSpec((1, LANES), lambda b, h, i: (0, 0)),
                  pl.BlockSpec((1, LANES), lambda b, h, i: (0, 0))],
        out_specs=pl.BlockSpec((tq, LANES), lambda b, h, i: (b * nq + i, h)),
        out_shape=jax.ShapeDtypeStruct((T, BRANCH_WIDTH), BF16),
        scratch_shapes=[pltpu.VMEM((LANES, S), BF16)] + [pltpu.VMEM((2, tq, tq), F32)] * 2,
        compiler_params=_params(("parallel", "parallel", "arbitrary")),
        name="diff_attn",
    )(proj, proj, proj, lam_row, norm_row)


def _merge_kernel(x_ref, ya_ref, yb_ref, yc_ref, wg_ref, wb_ref, wo_ref, g1_ref, b1_ref, wr_ref, br_ref,
                  x1_ref, ids_ref, wts_ref, *, alpha):
    x = x_ref[...]
    xb = x.astype(BF16)
    merged = None
    for r, y_ref in enumerate((ya_ref, yb_ref, yc_ref)):
        gate = jax.nn.sigmoid(_dot(xb, wg_ref[:, r * D_MODEL:(r + 1) * D_MODEL]))
        term = gate * _dot(y_ref[...], wb_ref[r])
        merged = term if merged is None else merged + term
    h = _dot(merged.astype(BF16), wo_ref[...])
    x1 = _layer_norm(alpha * x + h, g1_ref[...], b1_ref[...])
    x1_ref[...] = x1

    logits = _dot(x1.astype(BF16), wr_ref[...]) + br_ref[...]
    lane = lax.broadcasted_iota(jnp.int32, logits.shape, 1)
    lane_f = lane.astype(F32)
    is_group = lane < N_GROUPS
    gl = jnp.where(is_group, logits, NEG_INF)
    gmax = jnp.max(gl, axis=-1, keepdims=True)
    gsum = jnp.sum(jnp.where(is_group, jnp.exp(gl - gmax), 0.0), axis=-1, keepdims=True)
    g_p = 1.0 / gsum
    g_idx = jnp.min(jnp.where(gl == gmax, lane_f, float(LANES)), axis=-1, keepdims=True)
    lo = N_GROUPS + EXPERTS_PER_GROUP * g_idx
    in_group = (lane_f >= lo) & (lane_f < lo + EXPERTS_PER_GROUP)
    el = jnp.where(in_group, logits, NEG_INF)
    v1 = jnp.max(el, axis=-1, keepdims=True)
    i1 = jnp.min(jnp.where(el == v1, lane_f, float(LANES)), axis=-1, keepdims=True)
    el2 = jnp.where(lane_f == i1, NEG_INF, el)
    v2 = jnp.max(el2, axis=-1, keepdims=True)
    i2 = jnp.min(jnp.where(el2 == v2, lane_f, float(LANES)), axis=-1, keepdims=True)
    t = jnp.exp(v2 - v1)
    w1 = g_p / (1.0 + t)
    w2 = g_p * t / (1.0 + t)
    ids = jnp.where(lane == 0, i1 - N_GROUPS, jnp.where(lane == 1, i2 - N_GROUPS, 0.0))
    ids_ref[...] = ids.astype(jnp.int32)
    wts_ref[...] = jnp.where(lane == 0, w1, jnp.where(lane == 1, w2, 0.0))


def _merge_call(x, ya, yb, yc, w_gates, w_branch, w_out, ln_g, ln_b, w_router, b_router, alpha, tm):
    T, D = x.shape
    row = lambda i: (i, 0)
    const2 = lambda i: (0, 0)
    return pl.pallas_call(
        functools.partial(_merge_kernel, alpha=alpha),
        grid=(T // tm,),
        in_specs=[pl.BlockSpec((tm, D), row),
                  pl.BlockSpec((tm, BRANCH_WIDTH), row),
                  pl.BlockSpec((tm, BRANCH_WIDTH), row),
                  pl.BlockSpec((tm, BRANCH_WIDTH), row),
                  pl.BlockSpec((D, N_BRANCHES * D), const2),
                  pl.BlockSpec((N_BRANCHES, BRANCH_WIDTH, D), lambda i: (0, 0, 0)),
                  pl.BlockSpec((D, D), const2),
                  pl.BlockSpec((1, D), const2),
                  pl.BlockSpec((1, D), const2),
                  pl.BlockSpec((D, LANES), const2),
                  pl.BlockSpec((1, LANES), const2)],
        out_specs=[pl.BlockSpec((tm, D), row),
                   pl.BlockSpec((tm, LANES), row),
                   pl.BlockSpec((tm, LANES), row)],
        out_shape=[jax.ShapeDtypeStruct((T, D), F32),
                   jax.ShapeDtypeStruct((T, LANES), jnp.int32),
                   jax.ShapeDtypeStruct((T, LANES), F32)],
        compiler_params=_params(("parallel",)),
        name="merge_ln1_router",
    )(x, ya, yb, yc, w_gates, w_branch, w_out, ln_g.reshape(1, D), ln_b.reshape(1, D), w_router, b_router)


def _row_copy(src, src_row, dst, dst_row, sem):
    return pltpu.make_async_copy(src.at[pl.ds(src_row, 1), :], dst.at[pl.ds(dst_row, 1), :], sem)


def _dispatch_kernel(pos_ref, pad_ref, x_ref, xs_hbm, zero_scr, sem, *, tm, npad, pad_steps):
    def issue(r, carry):
        _row_copy(x_ref, r, xs_hbm, pos_ref[2 * r], sem).start()
        _row_copy(x_ref, r, xs_hbm, pos_ref[2 * r + 1], sem).start()
        return carry

    lax.fori_loop(0, tm, issue, 0, unroll=8)

    @pl.when(pl.program_id(0) < pad_steps)
    def _():
        zero_scr[...] = jnp.zeros_like(zero_scr)

        def issue_pad(r, carry):
            _row_copy(zero_scr, r, xs_hbm, pad_ref[r], sem).start()
            return carry

        lax.fori_loop(0, npad, issue_pad, 0, unroll=8)
        pltpu.make_async_copy(zero_scr, xs_hbm.at[pl.ds(0, npad), :], sem).wait()

    for _ in range(2):
        pltpu.make_async_copy(x_ref, xs_hbm.at[pl.ds(0, tm), :], sem).wait()


def _dispatch_call(pos, pad_rows, x1, n_rows, tm):
    T, D = x1.shape
    nsteps = T // tm
    npad = max(LANES, pad_rows.shape[0] // nsteps)
    pad_steps = pad_rows.shape[0] // npad
    return pl.pallas_call(
        functools.partial(_dispatch_kernel, tm=tm, npad=npad, pad_steps=pad_steps),
        grid=(nsteps,),
        in_specs=[pl.BlockSpec((2 * tm,), lambda i: (i,), memory_space=pltpu.SMEM),
                  pl.BlockSpec((npad,), lambda i: (jnp.minimum(i, pad_steps - 1),), memory_space=pltpu.SMEM),
                  pl.BlockSpec((tm, D), lambda i: (i, 0))],
        out_specs=pl.BlockSpec(memory_space=pl.ANY),
        out_shape=jax.ShapeDtypeStruct((n_rows, D), F32),
        scratch_shapes=[pltpu.VMEM((npad, D), F32), pltpu.SemaphoreType.DMA(())],
        compiler_params=_params(("arbitrary",)),
        name="moe_dispatch",
    )(pos, pad_rows, x1)


def _expert_kernel(te_ref, nu_ref, xs_ref, wg_ref, wu_ref, wd_ref, o_ref, wgu_scr, wd_scr):
    t = pl.program_id(0)
    used = t < nu_ref[0]
    new_expert = (t == 0) | (te_ref[t] != te_ref[jnp.maximum(t - 1, 0)])

    @pl.when(used & new_expert)
    def _():
        wgu_scr[:, 0:D_EXPERT] = wg_ref[0, 0].astype(BF16)
        wgu_scr[:, D_EXPERT:2 * D_EXPERT] = wu_ref[0, 0].astype(BF16)
        wd_scr[...] = wd_ref[0, 0].astype(BF16)

    @pl.when(used)
    def _():
        gu = _dot(xs_ref[...].astype(BF16), wgu_scr[...])
        g = gu[:, 0:D_EXPERT]
        h = (g * jax.nn.sigmoid(g)) * gu[:, D_EXPERT:2 * D_EXPERT]
        o_ref[...] = _dot(h.astype(BF16), wd_scr[...])

    @pl.when(jnp.logical_not(used))
    def _():
        o_ref[...] = jnp.zeros_like(o_ref)


def _expert_call(tile_expert, n_used, xs, n_tiles, w_gate, w_up, w_down, layer):
    D = xs.shape[1]
    tm = EXPERT_TILE
    used = lambda t, te, nu: (jnp.minimum(t, nu[0] - 1), 0)
    expert = lambda t, te, nu: (layer, te[t], 0, 0)
    grid_spec = pltpu.PrefetchScalarGridSpec(
        num_scalar_prefetch=2,
        grid=(n_tiles,),
        in_specs=[pl.BlockSpec((tm, D), used),
                  pl.BlockSpec((1, 1, D, D_EXPERT), expert),
                  pl.BlockSpec((1, 1, D, D_EXPERT), expert),
                  pl.BlockSpec((1, 1, D_EXPERT, D), expert)],
        out_specs=pl.BlockSpec((tm, D), lambda t, te, nu: (t, 0)),
        scratch_shapes=[pltpu.VMEM((D, 2 * D_EXPERT), BF16), pltpu.VMEM((D_EXPERT, D), BF16)],
    )
    return pl.pallas_call(
        _expert_kernel,
        grid_spec=grid_spec,
        out_shape=jax.ShapeDtypeStruct((n_tiles * tm, D), F32),
        compiler_params=_params(("arbitrary",)),
        name="experts",
    )(tile_expert, n_used, xs, w_gate, w_up, w_down)


def _combine_kernel(pos_ref, x_ref, w_ref, g_ref, b_ref, y_hbm, o_ref, ybuf, sem, *, alpha, tm):
    def issue(r, carry):
        _row_copy(y_hbm, pos_ref[2 * r], ybuf.at[0], r, sem).start()
        _row_copy(y_hbm, pos_ref[2 * r + 1], ybuf.at[1], r, sem).start()
        return carry

    lax.fori_loop(0, tm, issue, 0, unroll=8)
    for k in range(2):
        pltpu.make_async_copy(y_hbm.at[pl.ds(0, tm), :], ybuf.at[k], sem).wait()
    w = w_ref[...]
    u = alpha * x_ref[...] + w[:, 0:1] * ybuf[0] + w[:, 1:2] * ybuf[1]
    o_ref[...] = _layer_norm(u, g_ref[...], b_ref[...])


def _combine_call(pos, x1, wts, ln_g, ln_b, y, alpha, tm):
    T, D = x1.shape
    row = lambda i: (i, 0)
    const2 = lambda i: (0, 0)
    return pl.pallas_call(
        functools.partial(_combine_kernel, alpha=alpha, tm=tm),
        grid=(T // tm,),
        in_specs=[pl.BlockSpec((2 * tm,), lambda i: (i,), memory_space=pltpu.SMEM),
                  pl.BlockSpec((tm, D), row),
                  pl.BlockSpec((tm, LANES), row),
                  pl.BlockSpec((1, D), const2), pl.BlockSpec((1, D), const2),
                  pl.BlockSpec(memory_space=pl.ANY)],
        out_specs=pl.BlockSpec((tm, D), row),
        out_shape=jax.ShapeDtypeStruct((T, D), F32),
        scratch_shapes=[pltpu.VMEM((2, tm, D), F32), pltpu.SemaphoreType.DMA(())],
        compiler_params=_params(("arbitrary",)),
        name="combine_ln2",
    )(pos, x1, wts, ln_g.reshape(1, D), ln_b.reshape(1, D), y)


def _rope_tables(positions):
    half = ROPE_DIM // 2
    inv_freq = ROPE_THETA ** (-jnp.arange(0, ROPE_DIM, 2, dtype=F32) / ROPE_DIM)
    ang = positions.astype(F32).reshape(-1, 1) * inv_freq[None, :]
    cos, sin = jnp.cos(ang), jnp.sin(ang)
    lane = jnp.arange(LANES)
    in_head = lane % DIFF_DH
    freq = in_head % half
    first = in_head < half
    second = (in_head >= half) & (in_head < ROPE_DIM)
    cosf = jnp.where((first | second)[None, :], cos[:, freq], 1.0)
    sin_a = jnp.where(second[None, :], sin[:, freq], 0.0)
    sin_b = jnp.where(first[None, :], -sin[:, freq], 0.0)
    return cosf, sin_a, sin_b


def _dispatch_plan(ids, T):
    tm = EXPERT_TILE
    flat = ids.reshape(-1)
    onehot = (flat[:, None] == jnp.arange(N_EXPERTS, dtype=jnp.int32)[None, :]).astype(jnp.int32)
    csum = jnp.cumsum(onehot, axis=0)
    counts = csum[-1]
    rank = jnp.take_along_axis(csum, flat[:, None], axis=1)[:, 0] - 1
    padded = ((counts + tm - 1) // tm) * tm
    ends = jnp.cumsum(padded)
    starts = ends - padded
    pos = (starts[flat] + rank).astype(jnp.int32)
    n_slab = 2 * T + N_EXPERTS * tm
    n_tiles = n_slab // tm
    tile_start = jnp.arange(n_tiles, dtype=jnp.int32) * tm
    tile_expert = jnp.minimum(jnp.sum((tile_start[:, None] >= ends[None, :]).astype(jnp.int32), axis=1),
                              N_EXPERTS - 1).astype(jnp.int32)
    n_used = (ends[-1] // tm).astype(jnp.int32).reshape(1)
    gap = padded - counts
    gap_end = jnp.cumsum(gap)
    gap_start = gap_end - gap
    j = jnp.arange(N_EXPERTS * tm, dtype=jnp.int32)
    e = jnp.minimum(jnp.sum((j[:, None] >= gap_end[None, :]).astype(jnp.int32), axis=1), N_EXPERTS - 1)
    in_tile = starts[e] + counts[e] + (j - gap_start[e])
    pad_rows = jnp.where(j < gap_end[-1], in_tile, ends[-1] + (j - gap_end[-1]))
    return pos, tile_expert, n_used, pad_rows.astype(jnp.int32), n_tiles


def _layer(x, cosf, sina, sinb, B, S, lb, p, alpha, lam_init):
    T = B * S
    tm = min(512, T)
    w_in = p["w_in"]
    w_main = jnp.concatenate([w_in[:, :3584], w_in[:, 3592:5128]], axis=1).astype(BF16)
    w_ff = jnp.zeros((D_MODEL, LANES), BF16).at[:, :FOX_HEADS].set(w_in[:, 3584:3592].astype(BF16))
    w_gates = w_in[:, 5128:].astype(BF16)

    proj, ff = _in_proj_call(x, w_main, w_ff, cosf, sina, sinb, tm)
    ya = _hgrn_call(proj, lb, p["hgrn_norm_g"], B, S)
    qa, ka = _fox_gate_call(ff, p["fox_f_bias"], B, S)
    yb = _fox_call(proj, qa, ka, B, S)
    lv = p["diff_lambda"].astype(F32)
    lam = jnp.exp(jnp.sum(lv[0] * lv[1])) - jnp.exp(jnp.sum(lv[2] * lv[3])) + lam_init
    lam_row = jnp.full((1, LANES), lam, F32)
    norm_row = (p["diff_norm_g"].astype(F32) * (1.0 - lam_init)).reshape(1, DIFF_DV)
    yc = _diff_call(proj, lam_row, norm_row, B, S)

    w_router = jnp.zeros((D_MODEL, LANES), F32)
    w_router = w_router.at[:, :N_GROUPS].set(p["router_g_w"]).at[:, N_GROUPS:N_GROUPS + N_EXPERTS].set(p["router_e_w"])
    b_router = jnp.zeros((1, LANES), F32)
    b_router = b_router.at[0, :N_GROUPS].set(p["router_g_b"]).at[0, N_GROUPS:N_GROUPS + N_EXPERTS].set(
        p["router_e_b"].reshape(-1))
    x1, ids, wts = _merge_call(x, ya, yb, yc, w_gates, p["w_branch"].astype(BF16), p["w_out"].astype(BF16),
                               p["ln1_g"], p["ln1_b"], w_router.astype(BF16), b_router, alpha, tm)

    pos, tile_expert, n_used, pad_rows, n_tiles = _dispatch_plan(ids[:, :2], T)
    xs = _dispatch_call(pos, pad_rows, x1, n_tiles * EXPERT_TILE, tm)
    y = _expert_call(tile_expert, n_used, xs, n_tiles, p["expert_w_gate"], p["expert_w_up"], p["expert_w_down"],
                     p["layer"])
    return _combine_call(pos, x1, wts, p["ln2_g"], p["ln2_b"], y, alpha, tm)


def kernel(x, positions, ln_in_g, ln_in_b, w_in, hgrn_lb_logits, hgrn_norm_g, fox_f_bias, diff_lambda,
           diff_norm_g, w_branch, w_out, ln1_g, ln1_b, router_g_w, router_g_b, router_e_w, router_e_b,
           expert_w_gate, expert_w_up, expert_w_down, ln2_g, ln2_b):
    B, S, D = x.shape
    T = B * S
    depth = w_in.shape[0]
    alpha = (2 * depth) ** 0.25
    cosf, sina, sinb = _rope_tables(positions)
    lb_soft = jax.nn.softmax(hgrn_lb_logits.astype(F32), axis=0)
    lower_bounds = jnp.maximum(jnp.cumsum(lb_soft, axis=0) - lb_soft[0], 0.0)

    h = _ln_call(x.reshape(T, D), ln_in_g, ln_in_b, min(512, T))
    for l in range(depth):
        p = dict(w_in=w_in[l], hgrn_norm_g=hgrn_norm_g[l], fox_f_bias=fox_f_bias[l], diff_lambda=diff_lambda[l],
                 diff_norm_g=diff_norm_g[l], w_branch=w_branch[l], w_out=w_out[l], ln1_g=ln1_g[l], ln1_b=ln1_b[l],
                 router_g_w=router_g_w[l], router_g_b=router_g_b[l], router_e_w=router_e_w[l],
                 router_e_b=router_e_b[l], expert_w_gate=expert_w_gate, expert_w_up=expert_w_up,
                 expert_w_down=expert_w_down, layer=l, ln2_g=ln2_g[l], ln2_b=ln2_b[l])
        lam_init = 0.8 - 0.6 * float(math.exp(-0.3 * l))
        h = _layer(h, cosf, sina, sinb, B, S, lower_bounds[l], p, alpha, lam_init)
    return h.reshape(B, S, D)
```

```python
import functools
import math

import numpy as np
import jax
import jax.numpy as jnp
from jax import lax
from jax.experimental import pallas as pl
from jax.experimental.pallas import tpu as pltpu

F32 = jnp.float32
BF16 = jnp.bfloat16

D_MODEL = 1024
HG_HEADS, HG_D = 4, 128
FOX_HEADS, FOX_DH = 8, 64
DIFF_HEADS, DIFF_DH, DIFF_DV = 4, 64, 128
BRANCH_WIDTH = 512
N_BRANCHES = 3
ROPE_THETA = 500000.0
ROPE_DIM = DIFF_DH // 4
N_GROUPS, EXPERTS_PER_GROUP = 4, 8
N_EXPERTS = N_GROUPS * EXPERTS_PER_GROUP
D_EXPERT = 512
LN_EPS = 1e-5
RMS_EPS = 1e-6
NEG_INF = -1e30
EXP_CLAMP = 60.0
LOG2E = 1.4426950408889634

LANES = 128
N_MAIN = 5120
COL_HQ, COL_HF, COL_HI, COL_HG = 0, 4, 8, 12
COL_FQ, COL_FK, COL_FV = 16, 20, 24
COL_DQ, COL_DK, COL_DV = 28, 32, 36
ROPE_TILES = (7, 8)

HG_CHUNK = 128
HG_SUB = 8
HG_BATCH = 2
GATE_CHUNK = 256
ATTN_BLOCK = 512
EXPERT_TILE = 256
VMEM_LIMIT = 56 * 1024 * 1024


def _dot(a, b):
    return jnp.dot(a, b, preferred_element_type=F32)


def _dot_nt(a, b):
    return lax.dot_general(a, b, (((1,), (1,)), ((), ())), preferred_element_type=F32)


def _log_sigmoid(z):
    return jnp.minimum(z, 0.0) - jnp.log1p(jnp.exp(-jnp.abs(z)))


def _split3(x):
    h1 = x.astype(BF16)
    r1 = x - h1.astype(F32)
    h2 = r1.astype(BF16)
    h3 = (r1 - h2.astype(F32)).astype(BF16)
    return h1, h2, h3


def _cumsum_rows(tri, x):
    h1, h2, h3 = _split3(x)
    return _dot(tri, h1) + _dot(tri, h2) + _dot(tri, h3)


def _layer_norm(u, g, b):
    mu = jnp.mean(u, axis=-1, keepdims=True)
    d = u - mu
    var = jnp.mean(d * d, axis=-1, keepdims=True)
    return d * lax.rsqrt(var + LN_EPS) * g + b


def _params(sem):
    return pltpu.CompilerParams(dimension_semantics=sem, vmem_limit_bytes=VMEM_LIMIT)


def _ln_kernel(x_ref, g_ref, b_ref, o_ref):
    o_ref[...] = _layer_norm(x_ref[...], g_ref[...], b_ref[...])


def _ln_call(x, g, b, tm):
    T, D = x.shape
    return pl.pallas_call(
        _ln_kernel,
        grid=(T // tm,),
        in_specs=[pl.BlockSpec((tm, D), lambda i: (i, 0)),
                  pl.BlockSpec((1, D), lambda i: (0, 0)),
                  pl.BlockSpec((1, D), lambda i: (0, 0))],
        out_specs=pl.BlockSpec((tm, D), lambda i: (i, 0)),
        out_shape=jax.ShapeDtypeStruct((T, D), F32),
        compiler_params=_params(("parallel",)),
        name="ln_in",
    )(x, g.reshape(1, D), b.reshape(1, D))


def _in_proj_kernel(x_ref, w_ref, wff_ref, cos_ref, sa_ref, sb_ref, o_ref, ff_ref):
    xb = x_ref[...].astype(BF16)
    ff_ref[...] = _dot(xb, wff_ref[...])
    for j in range(N_MAIN // 512):
        acc = _dot(xb, w_ref[:, j * 512:(j + 1) * 512])
        if j in ROPE_TILES:
            cosf, sa, sb = cos_ref[...], sa_ref[...], sb_ref[...]
            for g in range(4):
                t = acc[:, g * LANES:(g + 1) * LANES]
                r = t * cosf + pltpu.roll(t, 8, 1) * sa + pltpu.roll(t, LANES - 8, 1) * sb
                o_ref[:, j * 512 + g * LANES:j * 512 + (g + 1) * LANES] = r.astype(BF16)
        else:
            o_ref[:, j * 512:(j + 1) * 512] = acc.astype(BF16)


def _in_proj_call(x, w_main, w_ff, cosf, sina, sinb, tm):
    T, D = x.shape
    const = lambda i: (0, 0)
    row = lambda i: (i, 0)
    return pl.pallas_call(
        _in_proj_kernel,
        grid=(T // tm,),
        in_specs=[pl.BlockSpec((tm, D), row),
                  pl.BlockSpec((D, N_MAIN), const),
                  pl.BlockSpec((D, LANES), const),
                  pl.BlockSpec((tm, LANES), row),
                  pl.BlockSpec((tm, LANES), row),
                  pl.BlockSpec((tm, LANES), row)],
        out_specs=[pl.BlockSpec((tm, N_MAIN), row),
                   pl.BlockSpec((tm, LANES), row)],
        out_shape=[jax.ShapeDtypeStruct((T, N_MAIN), BF16),
                   jax.ShapeDtypeStruct((T, LANES), F32)],
        compiler_params=_params(("parallel",)),
        name="in_proj",
    )(x, w_main, w_ff, cosf, sina, sinb)


def _hgrn_chunk(z, ql, v, gl, lb, ng, state_t, b_scr, g_scr, consts):
    C = HG_CHUNK
    tri, levels, diag_masks, lane_c = consts
    u = jnp.exp(-jnp.abs(z))
    log1pu = jnp.log(1.0 + u)
    log_f = (jnp.minimum(z, 0.0) - log1pu) + jnp.log(1.0 + lb * jnp.exp(jnp.minimum(-z, EXP_CLAMP)))
    log2k = (jnp.minimum(-z, 0.0) - log1pu) * LOG2E + jnp.log2(1.0 - lb)
    k = jnp.exp2(log2k)
    q = ql * jax.nn.sigmoid(ql)
    b2 = _cumsum_rows(tri, log_f) * LOG2E
    b_scr[...] = b2
    g_scr[...] = log2k - b2
    vb = v.astype(BF16)
    b_last = b_scr[C - 1:C, :]

    o = _dot_nt((q * jnp.exp2(b2)).astype(BF16), state_t.astype(BF16))

    scores = jnp.zeros((C, C), F32)
    for m, is_query, pair in levels:
        pieces = [jnp.broadcast_to(b_scr[p * 2 * m + m - 1:p * 2 * m + m, :], (2 * m, LANES))
                  for p in range(C // (2 * m))]
        b_ref_rows = pieces[0] if len(pieces) == 1 else jnp.concatenate(pieces, axis=0)
        decay = jnp.exp2(-jnp.abs(b2 - b_ref_rows))
        qd = jnp.where(is_query, q * decay, 0.0).astype(BF16)
        kd = jnp.where(is_query, 0.0, k * decay).astype(BF16)
        scores = scores + jnp.where(pair, _dot_nt(qd, kd), 0.0)

    diag = []
    for blk in range(C // HG_SUB):
        lo = blk * HG_SUB
        bb = b2[lo:lo + HG_SUB]
        qq = q[lo:lo + HG_SUB]
        blk_scores = jnp.zeros((HG_SUB, C), F32)
        for s in range(HG_SUB):
            w = qq * jnp.exp2(bb + g_scr[lo + s:lo + s + 1, :])
            blk_scores = jnp.where(lane_c == lo + s, jnp.sum(w, axis=-1, keepdims=True), blk_scores)
        diag.append(jnp.where(diag_masks[blk], blk_scores, 0.0))
    scores = scores + jnp.concatenate(diag, axis=0)
    o = o + _dot(scores.astype(BF16), vb)

    k_dec = (k * jnp.exp2(b_last - b2)).astype(BF16)
    state_t = state_t * jnp.exp2(b_last) + _dot(v.T.astype(BF16), k_dec)

    ms = jnp.mean(o * o, axis=-1, keepdims=True)
    y = o * lax.rsqrt(ms + RMS_EPS) * ng * (gl * jax.nn.sigmoid(gl))
    return y, state_t


def _hgrn_kernel(q_ref, f_ref, i_ref, g_ref, lb_ref, ng_ref, o_ref, b_scr, g_scr, *, nchunks):
    C = HG_CHUNK
    ng = ng_ref[...]
    ri = lax.broadcasted_iota(jnp.int32, (C, C), 0)
    ci = lax.broadcasted_iota(jnp.int32, (C, C), 1)
    tri = jnp.where(ri >= ci, 1.0, 0.0).astype(BF16)
    rows = lax.broadcasted_iota(jnp.int32, (C, LANES), 0)
    sub_rows = lax.broadcasted_iota(jnp.int32, (HG_SUB, C), 0)
    lane_c = lax.broadcasted_iota(jnp.int32, (HG_SUB, C), 1)
    diag_masks = [sub_rows + blk * HG_SUB >= lane_c for blk in range(C // HG_SUB)]
    levels = []
    m = HG_SUB
    while m < C:
        shift = int(math.log2(2 * m))
        is_query = (rows & (2 * m - 1)) >= m
        pair = ((ri >> shift) == (ci >> shift)) & ((ri & (2 * m - 1)) >= m) & ((ci & (2 * m - 1)) < m)
        levels.append((m, is_query, pair))
        m *= 2
    consts = (tri, levels, diag_masks, lane_c)

    nb = q_ref.shape[0]

    def body(n, states):
        r0 = pl.multiple_of(n * C, C)
        new_states = []
        for bi in range(nb):
            for h in range(HG_HEADS):
                cols = slice(h * LANES, (h + 1) * LANES)
                y, st = _hgrn_chunk(f_ref[bi, pl.ds(r0, C), cols].astype(F32), q_ref[bi, pl.ds(r0, C), cols].astype(F32),
                                    i_ref[bi, pl.ds(r0, C), cols].astype(F32), g_ref[bi, pl.ds(r0, C), cols].astype(F32),
                                    lb_ref[:, cols], ng, states[bi * HG_HEADS + h],
                                    b_scr.at[bi * HG_HEADS + h], g_scr.at[bi * HG_HEADS + h], consts)
                o_ref[bi, pl.ds(r0, C), cols] = y.astype(o_ref.dtype)
                new_states.append(st)
        return tuple(new_states)

    lax.fori_loop(0, nchunks, body, tuple(jnp.zeros((HG_D, HG_D), F32) for _ in range(nb * HG_HEADS)))


def _hgrn_call(proj, lb, norm_g, B, S):
    W = HG_HEADS * HG_D
    nb = HG_BATCH if B % HG_BATCH == 0 else 1
    proj3 = proj.reshape(B, S, N_MAIN)
    blk = lambda off: pl.BlockSpec((nb, S, W), lambda b, off=off: (b, 0, off))
    out = pl.pallas_call(
        functools.partial(_hgrn_kernel, nchunks=S // HG_CHUNK),
        grid=(B // nb,),
        in_specs=[blk(COL_HQ // 4), blk(COL_HF // 4), blk(COL_HI // 4), blk(COL_HG // 4),
                  pl.BlockSpec((1, W), lambda b: (0, 0)),
                  pl.BlockSpec((1, LANES), lambda b: (0, 0))],
        out_specs=pl.BlockSpec((nb, S, W), lambda b: (b, 0, 0)),
        out_shape=jax.ShapeDtypeStruct((B, S, BRANCH_WIDTH), BF16),
        scratch_shapes=[pltpu.VMEM((nb * HG_HEADS, HG_CHUNK, LANES), F32)] * 2,
        compiler_params=_params(("parallel",)),
        name="hgrn",
    )(proj3, proj3, proj3, proj3, lb.reshape(1, W), norm_g.reshape(1, HG_D))
    return out.reshape(B * S, BRANCH_WIDTH)


def _fox_aug_tables():
    sel_q = np.zeros((3, LANES, BRANCH_WIDTH), np.float32)
    sel_k = np.zeros((3, LANES, BRANCH_WIDTH), np.float32)
    ones_q = np.zeros((1, BRANCH_WIDTH), np.float32)
    ones_k = np.zeros((1, BRANCH_WIDTH), np.float32)
    for col in range(BRANCH_WIDTH):
        pair, within = divmod(col, LANES)
        half, slot = divmod(within, FOX_DH)
        head = 2 * pair + 1 - half
        if slot < 3:
            sel_q[slot, head, col] = 1.0
            ones_k[0, col] = 1.0
        elif slot < 6:
            sel_k[slot - 3, head, col] = -1.0
            ones_q[0, col] = 1.0
    return sel_q, sel_k, ones_q, ones_k


def _fox_gate_kernel(ff_ref, bias_ref, selq_ref, selk_ref, oq_ref, ok_ref, qa_ref, ka_ref, *, nchunks):
    CH = GATE_CHUNK
    ri = lax.broadcasted_iota(jnp.int32, (CH, CH), 0)
    ci = lax.broadcasted_iota(jnp.int32, (CH, CH), 1)
    tri = jnp.where(ri >= ci, 1.0, 0.0).astype(BF16)
    bias = bias_ref[...]

    def body(n, carry):
        r0 = pl.multiple_of(n * CH, CH)
        c = carry + _cumsum_rows(tri, _log_sigmoid(ff_ref[pl.ds(r0, CH), :] + bias))
        parts = _split3(c * LOG2E)
        qa = oq_ref[...] + _dot(parts[0], selq_ref[0]) + _dot(parts[1], selq_ref[1]) + _dot(parts[2], selq_ref[2])
        ka = ok_ref[...] + _dot(parts[0], selk_ref[0]) + _dot(parts[1], selk_ref[1]) + _dot(parts[2], selk_ref[2])
        qa_ref[pl.ds(r0, CH), :] = qa.astype(BF16)
        ka_ref[pl.ds(r0, CH), :] = ka.astype(BF16)
        return c[CH - 1:CH, :]

    lax.fori_loop(0, nchunks, body, jnp.zeros((1, LANES), F32))


def _fox_gate_call(ff, bias, B, S):
    T = B * S
    bias_row = jnp.zeros((1, LANES), F32).at[0, :FOX_HEADS].set(bias.astype(F32))
    sel_q, sel_k, ones_q, ones_k = _fox_aug_tables()
    const2 = lambda b: (0, 0)
    const3 = lambda b: (0, 0, 0)
    return pl.pallas_call(
        functools.partial(_fox_gate_kernel, nchunks=S // GATE_CHUNK),
        grid=(B,),
        in_specs=[pl.BlockSpec((S, LANES), lambda b: (b, 0)),
                  pl.BlockSpec((1, LANES), const2),
                  pl.BlockSpec((3, LANES, BRANCH_WIDTH), const3),
                  pl.BlockSpec((3, LANES, BRANCH_WIDTH), const3),
                  pl.BlockSpec((1, BRANCH_WIDTH), const2),
                  pl.BlockSpec((1, BRANCH_WIDTH), const2)],
        out_specs=[pl.BlockSpec((S, BRANCH_WIDTH), lambda b: (b, 0)),
                   pl.BlockSpec((S, BRANCH_WIDTH), lambda b: (b, 0))],
        out_shape=[jax.ShapeDtypeStruct((T, BRANCH_WIDTH), BF16),
                   jax.ShapeDtypeStruct((T, BRANCH_WIDTH), BF16)],
        compiler_params=_params(("parallel",)),
        name="fox_gate",
    )(ff, bias_row, jnp.asarray(sel_q, BF16), jnp.asarray(sel_k, BF16), jnp.asarray(ones_q), jnp.asarray(ones_k))


def _flash_t(qs, k_fns, vt_fns, n_full, n_blocks, s_a, s_b):
    tq = tk = ATTN_BLOCK
    nmap = len(qs)

    def produce(dst, j):
        for i in range(nmap):
            dst[i] = _dot_nt(k_fns[i](j * tk, tk), qs[i])

    def update(s, m, l, acc, vt):
        m_new = jnp.maximum(m, jnp.max(s, axis=0, keepdims=True))
        alpha = jnp.exp2(m - m_new)
        p = jnp.exp2(s - m_new)
        return m_new, alpha * l + jnp.sum(p, axis=0, keepdims=True), alpha * acc + _dot(vt, p.astype(BF16))

    def consume(src, stats, j):
        return tuple(update(src[i], *stats[i], vt_fns[i](j * tk, tk)) for i in range(nmap))

    h = tk // 2
    mask_a = lax.broadcasted_iota(jnp.int32, (h, tq), 0) <= lax.broadcasted_iota(jnp.int32, (h, tq), 1)
    mask_b = lax.broadcasted_iota(jnp.int32, (h, h), 0) <= lax.broadcasted_iota(jnp.int32, (h, h), 1)

    def produce_diag(dst, j):
        for i in range(nmap):
            dst[i, 0:h, :] = _dot_nt(k_fns[i](j * tk, h), qs[i])
            dst[i, h:tk, h:tq] = _dot_nt(k_fns[i](j * tk + h, h), qs[i][h:tq])

    def consume_diag(src, stats, j):
        out = []
        for i in range(nmap):
            m, l, acc = update(jnp.where(mask_a, src[i, 0:h, :], NEG_INF), *stats[i], vt_fns[i](j * tk, h))
            m_b, l_b, acc_b = update(jnp.where(mask_b, src[i, h:tk, h:tq], NEG_INF), m[:, h:], l[:, h:], acc[:, h:],
                                     vt_fns[i](j * tk + h, h))
            out.append((jnp.concatenate([m[:, :h], m_b], axis=1), jnp.concatenate([l[:, :h], l_b], axis=1),
                        jnp.concatenate([acc[:, :h], acc_b], axis=1)))
        return tuple(out)

    stats = []
    for vt_fn in vt_fns:
        dv = vt_fn(0, tk).shape[0]
        stats.append((jnp.full((1, tq), NEG_INF, F32), jnp.zeros((1, tq), F32), jnp.zeros((dv, tq), F32)))

    def branch(n):
        def run(stats):
            bufs = (s_a, s_b)
            (produce if n > 0 else produce_diag)(bufs[0], 0)
            for j in range(n):
                (produce if j + 1 < n else produce_diag)(bufs[(j + 1) % 2], j + 1)
                stats = consume(bufs[j % 2], stats, j)
            return consume_diag(bufs[n % 2], stats, n)
        return run

    stats = lax.switch(n_full, [branch(n) for n in range(n_blocks)], tuple(stats))
    return [(acc, l) for (_, l, acc) in stats]


def _fill_vt(v_ref, vt_scr):
    S = v_ref.shape[0]
    for c in range(S // ATTN_BLOCK):
        sl = slice(c * ATTN_BLOCK, (c + 1) * ATTN_BLOCK)
        vt_scr[:, sl] = v_ref[sl, :].astype(F32).T.astype(BF16)


def _fox_kernel(q_ref, qa_ref, k_ref, ka_ref, v_ref, o_ref, vt_scr, s_a, s_b):
    qi = pl.program_id(2)
    tk = ATTN_BLOCK

    @pl.when(qi == 0)
    def _():
        _fill_vt(v_ref, vt_scr)

    lo = lax.broadcasted_iota(jnp.int32, (1, LANES), 1) < FOX_DH
    qf = q_ref[...].astype(F32) * (FOX_DH ** -0.5 * LOG2E)
    qa = qa_ref[...].astype(F32)
    qs = [jnp.where(lo, qf, qa).astype(BF16), jnp.where(lo, qa, qf).astype(BF16)]
    k_fns = [lambda k0, n: jnp.where(lo, k_ref[pl.ds(k0, n), :], ka_ref[pl.ds(k0, n), :]),
             lambda k0, n: jnp.where(lo, ka_ref[pl.ds(k0, n), :], k_ref[pl.ds(k0, n), :])]
    vt_fns = [lambda k0, n: vt_scr[0:FOX_DH, pl.ds(k0, n)],
              lambda k0, n: vt_scr[FOX_DH:2 * FOX_DH, pl.ds(k0, n)]]
    (acc0, l0), (acc1, l1) = _flash_t(qs, k_fns, vt_fns, qi, k_ref.shape[0] // tk, s_a, s_b)
    o_t = jnp.concatenate([acc0 / l0, acc1 / l1], axis=0)
    o_ref[...] = o_t.T.astype(o_ref.dtype)


def _fox_call(proj, qa, ka, B, S):
    T = B * S
    tq = ATTN_BLOCK
    nq = S // tq
    return pl.pallas_call(
        _fox_kernel,
        grid=(B, FOX_HEADS // 2, nq),
        in_specs=[pl.BlockSpec((tq, LANES), lambda b, p, i: (b * nq + i, COL_FQ + p)),
                  pl.BlockSpec((tq, LANES), lambda b, p, i: (b * nq + i, p)),
                  pl.BlockSpec((S, LANES), lambda b, p, i: (b, COL_FK + p)),
                  pl.BlockSpec((S, LANES), lambda b, p, i: (b, p)),
                  pl.BlockSpec((S, LANES), lambda b, p, i: (b, COL_FV + p))],
        out_specs=pl.BlockSpec((tq, LANES), lambda b, p, i: (b * nq + i, p)),
        out_shape=jax.ShapeDtypeStruct((T, BRANCH_WIDTH), BF16),
        scratch_shapes=[pltpu.VMEM((LANES, S), BF16)] + [pltpu.VMEM((2, tq, tq), F32)] * 2,
        compiler_params=_params(("parallel", "parallel", "arbitrary")),
        name="fox_attn",
    )(proj, qa, proj, ka, proj)


def _diff_kernel(q_ref, k_ref, v_ref, lam_ref, ng_ref, o_ref, vt_scr, s_a, s_b):
    qi = pl.program_id(2)
    tk = ATTN_BLOCK

    @pl.when(qi == 0)
    def _():
        _fill_vt(v_ref, vt_scr)

    lo = lax.broadcasted_iota(jnp.int32, (1, LANES), 1) < DIFF_DH
    qf = q_ref[...].astype(F32) * (DIFF_DH ** -0.5 * LOG2E)
    qs = [jnp.where(lo, qf, 0.0).astype(BF16), jnp.where(lo, 0.0, qf).astype(BF16)]
    k_fns = [lambda k0, n: k_ref[pl.ds(k0, n), :]] * 2
    vt_fns = [lambda k0, n: vt_scr[:, pl.ds(k0, n)]] * 2
    (acc0, l0), (acc1, l1) = _flash_t(qs, k_fns, vt_fns, qi, k_ref.shape[0] // tk, s_a, s_b)
    o = (acc0 / l0 - lam_ref[0:1, 0:1] * (acc1 / l1)).T
    ms = jnp.mean(o * o, axis=-1, keepdims=True)
    o_ref[...] = (o * lax.rsqrt(ms + RMS_EPS) * ng_ref[...]).astype(o_ref.dtype)


def _diff_call(proj, lam_row, norm_row, B, S):
    T = B * S
    tq = ATTN_BLOCK
    nq = S // tq
    return pl.pallas_call(
        _diff_kernel,
        grid=(B, DIFF_HEADS, nq),
        in_specs=[pl.BlockSpec((tq, LANES), lambda b, h, i: (b * nq + i, COL_DQ + h)),
                  pl.BlockSpec((S, LANES), lambda b, h, i: (b, COL_DK + h)),
                  pl.BlockSpec((S, LANES), lambda b, h, i: (b, COL_DV + h)),
                  pl.BlockSpec((1, LANES), lambda b, h, i: (0, 0)),
                  pl.BlockSpec((1, LANES), lambda b, h, i: (0, 0))],
        out_specs=pl.BlockSpec((tq, LANES), lambda b, h, i: (b * nq + i, h)),
        out_shape=jax.ShapeDtypeStruct((T, BRANCH_WIDTH), BF16),
        scratch_shapes=[pltpu.VMEM((LANES, S), BF16)] + [pltpu.VMEM((2, tq, tq), F32)] * 2,
        compiler_params=_params(("parallel", "parallel", "arbitrary")),
        name="diff_attn",
    )(proj, proj, proj, lam_row, norm_row)


def _merge_kernel(x_ref, ya_ref, yb_ref, yc_ref, wg_ref, wb_ref, wo_ref, g1_ref, b1_ref, wr_ref, br_ref,
                  x1_ref, ids_ref, wts_ref, *, alpha):
    x = x_ref[...]
    xb = x.astype(BF16)
    merged = None
    for r, y_ref in enumerate((ya_ref, yb_ref, yc_ref)):
        gate = jax.nn.sigmoid(_dot(xb, wg_ref[:, r * D_MODEL:(r + 1) * D_MODEL]))
        term = gate * _dot(y_ref[...], wb_ref[r])
        merged = term if merged is None else merged + term
    h = _dot(merged.astype(BF16), wo_ref[...])
    x1 = _layer_norm(alpha * x + h, g1_ref[...], b1_ref[...])
    x1_ref[...] = x1

    logits = _dot(x1.astype(BF16), wr_ref[...]) + br_ref[...]
    lane = lax.broadcasted_iota(jnp.int32, logits.shape, 1)
    lane_f = lane.astype(F32)
    is_group = lane < N_GROUPS
    gl = jnp.where(is_group, logits, NEG_INF)
    gmax = jnp.max(gl, axis=-1, keepdims=True)
    gsum = jnp.sum(jnp.where(is_group, jnp.exp(gl - gmax), 0.0), axis=-1, keepdims=True)
    g_p = 1.0 / gsum
    g_idx = jnp.min(jnp.where(gl == gmax, lane_f, float(LANES)), axis=-1, keepdims=True)
    lo = N_GROUPS + EXPERTS_PER_GROUP * g_idx
    in_group = (lane_f >= lo) & (lane_f < lo + EXPERTS_PER_GROUP)
    el = jnp.where(in_group, logits, NEG_INF)
    v1 = jnp.max(el, axis=-1, keepdims=True)
    i1 = jnp.min(jnp.where(el == v1, lane_f, float(LANES)), axis=-1, keepdims=True)
    el2 = jnp.where(lane_f == i1, NEG_INF, el)
    v2 = jnp.max(el2, axis=-1, keepdims=True)
    i2 = jnp.min(jnp.where(el2 == v2, lane_f, float(LANES)), axis=-1, keepdims=True)
    t = jnp.exp(v2 - v1)
    w1 = g_p / (1.0 + t)
    w2 = g_p * t / (1.0 + t)
    ids = jnp.where(lane == 0, i1 - N_GROUPS, jnp.where(lane == 1, i2 - N_GROUPS, 0.0))
    ids_ref[...] = ids.astype(jnp.int32)
    wts_ref[...] = jnp.where(lane == 0, w1, jnp.where(lane == 1, w2, 0.0))


def _merge_call(x, ya, yb, yc, w_gates, w_branch, w_out, ln_g, ln_b, w_router, b_router, alpha, tm):
    T, D = x.shape
    row = lambda i: (i, 0)
    const2 = lambda i: (0, 0)
    return pl.pallas_call(
        functools.partial(_merge_kernel, alpha=alpha),
        grid=(T // tm,),
        in_specs=[pl.BlockSpec((tm, D), row),
                  pl.BlockSpec((tm, BRANCH_WIDTH), row),
                  pl.BlockSpec((tm, BRANCH_WIDTH), row),
                  pl.BlockSpec((tm, BRANCH_WIDTH), row),
                  pl.BlockSpec((D, N_BRANCHES * D), const2),
                  pl.BlockSpec((N_BRANCHES, BRANCH_WIDTH, D), lambda i: (0, 0, 0)),
                  pl.BlockSpec((D, D), const2),
                  pl.BlockSpec((1, D), const2),
                  pl.BlockSpec((1, D), const2),
                  pl.BlockSpec((D, LANES), const2),
                  pl.BlockSpec((1, LANES), const2)],
        out_specs=[pl.BlockSpec((tm, D), row),
                   pl.BlockSpec((tm, LANES), row),
                   pl.BlockSpec((tm, LANES), row)],
        out_shape=[jax.ShapeDtypeStruct((T, D), F32),
                   jax.ShapeDtypeStruct((T, LANES), jnp.int32),
                   jax.ShapeDtypeStruct((T, LANES), F32)],
        compiler_params=_params(("parallel",)),
        name="merge_ln1_router",
    )(x, ya, yb, yc, w_gates, w_branch, w_out, ln_g.reshape(1, D), ln_b.reshape(1, D), w_router, b_router)


def _row_copy(src, src_row, dst, dst_row, sem):
    return pltpu.make_async_copy(src.at[pl.ds(src_row, 1), :], dst.at[pl.ds(dst_row, 1), :], sem)


def _dispatch_kernel(pos_ref, pad_ref, x_ref, xs_hbm, zero_scr, sem, *, tm, npad, pad_steps):
    def issue(r, carry):
        _row_copy(x_ref, r, xs_hbm, pos_ref[2 * r], sem).start()
        _row_copy(x_ref, r, xs_hbm, pos_ref[2 * r + 1], sem).start()
        return carry

    lax.fori_loop(0, tm, issue, 0, unroll=8)

    @pl.when(pl.program_id(0) < pad_steps)
    def _():
        zero_scr[...] = jnp.zeros_like(zero_scr)

        def issue_pad(r, carry):
            _row_copy(zero_scr, r, xs_hbm, pad_ref[r], sem).start()
            return carry

        lax.fori_loop(0, npad, issue_pad, 0, unroll=8)
        pltpu.make_async_copy(zero_scr, xs_hbm.at[pl.ds(0, npad), :], sem).wait()

    for _ in range(2):
        pltpu.make_async_copy(x_ref, xs_hbm.at[pl.ds(0, tm), :], sem).wait()


def _dispatch_call(pos, pad_rows, x1, n_rows, tm):
    T, D = x1.shape
    nsteps = T // tm
    npad = max(LANES, pad_rows.shape[0] // nsteps)
    pad_steps = pad_rows.shape[0] // npad
    return pl.pallas_call(
        functools.partial(_dispatch_kernel, tm=tm, npad=npad, pad_steps=pad_steps),
        grid=(nsteps,),
        in_specs=[pl.BlockSpec((2 * tm,), lambda i: (i,), memory_space=pltpu.SMEM),
                  pl.BlockSpec((npad,), lambda i: (jnp.minimum(i, pad_steps - 1),), memory_space=pltpu.SMEM),
                  pl.BlockSpec((tm, D), lambda i: (i, 0))],
        out_specs=pl.BlockSpec(memory_space=pl.ANY),
        out_shape=jax.ShapeDtypeStruct((n_rows, D), F32),
        scratch_shapes=[pltpu.VMEM((npad, D), F32), pltpu.SemaphoreType.DMA(())],
        compiler_params=_params(("arbitrary",)),
        name="moe_dispatch",
    )(pos, pad_rows, x1)


def _expert_kernel(te_ref, nu_ref, xs_ref, wg_ref, wu_ref, wd_ref, o_ref, wgu_scr, wd_scr):
    t = pl.program_id(0)
    used = t < nu_ref[0]
    new_expert = (t == 0) | (te_ref[t] != te_ref[jnp.maximum(t - 1, 0)])

    @pl.when(used & new_expert)
    def _():
        wgu_scr[:, 0:D_EXPERT] = wg_ref[0, 0].astype(BF16)
        wgu_scr[:, D_EXPERT:2 * D_EXPERT] = wu_ref[0, 0].astype(BF16)
        wd_scr[...] = wd_ref[0, 0].astype(BF16)

    @pl.when(used)
    def _():
        gu = _dot(xs_ref[...].astype(BF16), wgu_scr[...])
        g = gu[:, 0:D_EXPERT]
        h = (g * jax.nn.sigmoid(g)) * gu[:, D_EXPERT:2 * D_EXPERT]
        o_ref[...] = _dot(h.astype(BF16), wd_scr[...])

    @pl.when(jnp.logical_not(used))
    def _():
        o_ref[...] = jnp.zeros_like(o_ref)


def _expert_call(tile_expert, n_used, xs, n_tiles, w_gate, w_up, w_down, layer):
    D = xs.shape[1]
    tm = EXPERT_TILE
    used = lambda t, te, nu: (jnp.minimum(t, nu[0] - 1), 0)
    expert = lambda t, te, nu: (layer, te[t], 0, 0)
    grid_spec = pltpu.PrefetchScalarGridSpec(
        num_scalar_prefetch=2,
        grid=(n_tiles,),
        in_specs=[pl.BlockSpec((tm, D), used),
                  pl.BlockSpec((1, 1, D, D_EXPERT), expert),
                  pl.BlockSpec((1, 1, D, D_EXPERT), expert),
                  pl.BlockSpec((1, 1, D_EXPERT, D), expert)],
        out_specs=pl.BlockSpec((tm, D), lambda t, te, nu: (t, 0)),
        scratch_shapes=[pltpu.VMEM((D, 2 * D_EXPERT), BF16), pltpu.VMEM((D_EXPERT, D), BF16)],
    )
    return pl.pallas_call(
        _expert_kernel,
        grid_spec=grid_spec,
        out_shape=jax.ShapeDtypeStruct((n_tiles * tm, D), F32),
        compiler_params=_params(("arbitrary",)),
        name="experts",
    )(tile_expert, n_used, xs, w_gate, w_up, w_down)


def _combine_kernel(pos_ref, x_ref, w_ref, g_ref, b_ref, y_hbm, o_ref, ybuf, sem, *, alpha, tm):
    def issue(r, carry):
        _row_copy(y_hbm, pos_ref[2 * r], ybuf.at[0], r, sem).start()
        _row_copy(y_hbm, pos_ref[2 * r + 1], ybuf.at[1], r, sem).start()
        return carry

    lax.fori_loop(0, tm, issue, 0, unroll=8)
    for k in range(2):
        pltpu.make_async_copy(y_hbm.at[pl.ds(0, tm), :], ybuf.at[k], sem).wait()
    w = w_ref[...]
    u = alpha * x_ref[...] + w[:, 0:1] * ybuf[0] + w[:, 1:2] * ybuf[1]
    o_ref[...] = _layer_norm(u, g_ref[...], b_ref[...])


def _combine_call(pos, x1, wts, ln_g, ln_b, y, alpha, tm):
    T, D = x1.shape
    row = lambda i: (i, 0)
    const2 = lambda i: (0, 0)
    return pl.pallas_call(
        functools.partial(_combine_kernel, alpha=alpha, tm=tm),
        grid=(T // tm,),
        in_specs=[pl.BlockSpec((2 * tm,), lambda i: (i,), memory_space=pltpu.SMEM),
                  pl.BlockSpec((tm, D), row),
                  pl.BlockSpec((tm, LANES), row),
                  pl.BlockSpec((1, D), const2), pl.BlockSpec((1, D), const2),
                  pl.BlockSpec(memory_space=pl.ANY)],
        out_specs=pl.BlockSpec((tm, D), row),
        out_shape=jax.ShapeDtypeStruct((T, D), F32),
        scratch_shapes=[pltpu.VMEM((2, tm, D), F32), pltpu.SemaphoreType.DMA(())],
        compiler_params=_params(("arbitrary",)),
        name="combine_ln2",
    )(pos, x1, wts, ln_g.reshape(1, D), ln_b.reshape(1, D), y)


def _rope_tables(positions):
    half = ROPE_DIM // 2
    inv_freq = ROPE_THETA ** (-jnp.arange(0, ROPE_DIM, 2, dtype=F32) / ROPE_DIM)
    ang = positions.astype(F32).reshape(-1, 1) * inv_freq[None, :]
    cos, sin = jnp.cos(ang), jnp.sin(ang)
    lane = jnp.arange(LANES)
    in_head = lane % DIFF_DH
    freq = in_head % half
    first = in_head < half
    second = (in_head >= half) & (in_head < ROPE_DIM)
    cosf = jnp.where((first | second)[None, :], cos[:, freq], 1.0)
    sin_a = jnp.where(second[None, :], sin[:, freq], 0.0)
    sin_b = jnp.where(first[None, :], -sin[:, freq], 0.0)
    return cosf, sin_a, sin_b


def _dispatch_plan(ids, T):
    tm = EXPERT_TILE
    flat = ids.reshape(-1)
    onehot = (flat[:, None] == jnp.arange(N_EXPERTS, dtype=jnp.int32)[None, :]).astype(jnp.int32)
    csum = jnp.cumsum(onehot, axis=0)
    counts = csum[-1]
    rank = jnp.take_along_axis(csum, flat[:, None], axis=1)[:, 0] - 1
    padded = ((counts + tm - 1) // tm) * tm
    ends = jnp.cumsum(padded)
    starts = ends - padded
    pos = (starts[flat] + rank).astype(jnp.int32)
    n_slab = 2 * T + N_EXPERTS * tm
    n_tiles = n_slab // tm
    tile_start = jnp.arange(n_tiles, dtype=jnp.int32) * tm
    tile_expert = jnp.minimum(jnp.sum((tile_start[:, None] >= ends[None, :]).astype(jnp.int32), axis=1),
                              N_EXPERTS - 1).astype(jnp.int32)
    n_used = (ends[-1] // tm).astype(jnp.int32).reshape(1)
    gap = padded - counts
    gap_end = jnp.cumsum(gap)
    gap_start = gap_end - gap
    j = jnp.arange(N_EXPERTS * tm, dtype=jnp.int32)
    e = jnp.minimum(jnp.sum((j[:, None] >= gap_end[None, :]).astype(jnp.int32), axis=1), N_EXPERTS - 1)
    in_tile = starts[e] + counts[e] + (j - gap_start[e])
    pad_rows = jnp.where(j < gap_end[-1], in_tile, ends[-1] + (j - gap_end[-1]))
    return pos, tile_expert, n_used, pad_rows.astype(jnp.int32), n_tiles


def _layer(x, cosf, sina, sinb, B, S, lb, p, alpha, lam_init):
    T = B * S
    tm = min(512, T)
    w_in = p["w_in"]
    w_main = jnp.concatenate([w_in[:, :3584], w_in[:, 3592:5128]], axis=1).astype(BF16)
    w_ff = jnp.zeros((D_MODEL, LANES), BF16).at[:, :FOX_HEADS].set(w_in[:, 3584:3592].astype(BF16))
    w_gates = w_in[:, 5128:].astype(BF16)

    proj, ff = _in_proj_call(x, w_main, w_ff, cosf, sina, sinb, tm)
    ya = _hgrn_call(proj, lb, p["hgrn_norm_g"], B, S)
    qa, ka = _fox_gate_call(ff, p["fox_f_bias"], B, S)
    yb = _fox_call(proj, qa, ka, B, S)
    lv = p["diff_lambda"].astype(F32)
    lam = jnp.exp(jnp.sum(lv[0] * lv[1])) - jnp.exp(jnp.sum(lv[2] * lv[3])) + lam_init
    lam_row = jnp.full((1, LANES), lam, F32)
    norm_row = (p["diff_norm_g"].astype(F32) * (1.0 - lam_init)).reshape(1, DIFF_DV)
    yc = _diff_call(proj, lam_row, norm_row, B, S)

    w_router = jnp.zeros((D_MODEL, LANES), F32)
    w_router = w_router.at[:, :N_GROUPS].set(p["router_g_w"]).at[:, N_GROUPS:N_GROUPS + N_EXPERTS].set(p["router_e_w"])
    b_router = jnp.zeros((1, LANES), F32)
    b_router = b_router.at[0, :N_GROUPS].set(p["router_g_b"]).at[0, N_GROUPS:N_GROUPS + N_EXPERTS].set(
        p["router_e_b"].reshape(-1))
    x1, ids, wts = _merge_call(x, ya, yb, yc, w_gates, p["w_branch"].astype(BF16), p["w_out"].astype(BF16),
                               p["ln1_g"], p["ln1_b"], w_router.astype(BF16), b_router, alpha, tm)

    pos, tile_expert, n_used, pad_rows, n_tiles = _dispatch_plan(ids[:, :2], T)
    xs = _dispatch_call(pos, pad_rows, x1, n_tiles * EXPERT_TILE, tm)
    y = _expert_call(tile_expert, n_used, xs, n_tiles, p["expert_w_gate"], p["expert_w_up"], p["expert_w_down"],
                     p["layer"])
    return _combine_call(pos, x1, wts, p["ln2_g"], p["ln2_b"], y, alpha, tm)


def kernel(x, positions, ln_in_g, ln_in_b, w_in, hgrn_lb_logits, hgrn_norm_g, fox_f_bias, diff_lambda,
           diff_norm_g, w_branch, w_out, ln1_g, ln1_b, router_g_w, router_g_b, router_e_w, router_e_b,
           expert_w_gate, expert_w_up, expert_w_down, ln2_g, ln2_b):
    B, S, D = x.shape
    T = B * S
    depth = w_in.shape[0]
    alpha = (2 * depth) ** 0.25
    cosf, sina, sinb = _rope_tables(positions)
    lb_soft = jax.nn.softmax(hgrn_lb_logits.astype(F32), axis=0)
    lower_bounds = jnp.maximum(jnp.cumsum(lb_soft, axis=0) - lb_soft[0], 0.0)

    h = _ln_call(x.reshape(T, D), ln_in_g, ln_in_b, min(512, T))
    for l in range(depth):
        p = dict(w_in=w_in[l], hgrn_norm_g=hgrn_norm_g[l], fox_f_bias=fox_f_bias[l], diff_lambda=diff_lambda[l],
                 diff_norm_g=diff_norm_g[l], w_branch=w_branch[l], w_out=w_out[l], ln1_g=ln1_g[l], ln1_b=ln1_b[l],
                 router_g_w=router_g_w[l], router_g_b=router_g_b[l], router_e_w=router_e_w[l],
                 router_e_b=router_e_b[l], expert_w_gate=expert_w_gate, expert_w_up=expert_w_up,
                 expert_w_down=expert_w_down, layer=l, ln2_g=ln2_g[l], ln2_b=ln2_b[l])
        lam_init = 0.8 - 0.6 * float(math.exp(-0.3 * l))
        h = _layer(h, cosf, sina, sinb, B, S, lower_bounds[l], p, alpha, lam_init)
    return h.reshape(B, S, D)
```

```python
import functools
import math

import numpy as np
import jax
import jax.numpy as jnp
from jax import lax
from jax.experimental import pallas as pl
from jax.experimental.pallas import tpu as pltpu

F32 = jnp.float32
BF16 = jnp.bfloat16

D_MODEL = 1024
HG_HEADS, HG_D = 4, 128
FOX_HEADS, FOX_DH = 8, 64
DIFF_HEADS, DIFF_DH, DIFF_DV = 4, 64, 128
BRANCH_WIDTH = 512
N_BRANCHES = 3
ROPE_THETA = 500000.0
ROPE_DIM = DIFF_DH // 4
N_GROUPS, EXPERTS_PER_GROUP = 4, 8
N_EXPERTS = N_GROUPS * EXPERTS_PER_GROUP
D_EXPERT = 512
LN_EPS = 1e-5
RMS_EPS = 1e-6
NEG_INF = -1e30
EXP_CLAMP = 60.0
LOG2E = 1.4426950408889634

LANES = 128
N_MAIN = 5120
COL_HQ, COL_HF, COL_HI, COL_HG = 0, 4, 8, 12
COL_FQ, COL_FK, COL_FV = 16, 20, 24
COL_DQ, COL_DK, COL_DV = 28, 32, 36
ROPE_TILES = (7, 8)

HG_CHUNK = 128
HG_SUB = 8
HG_BATCH = 2
GATE_CHUNK = 256
ATTN_BLOCK = 512
EXPERT_TILE = 256
VMEM_LIMIT = 56 * 1024 * 1024


def _dot(a, b):
    return jnp.dot(a, b, preferred_element_type=F32)


def _dot_nt(a, b):
    return lax.dot_general(a, b, (((1,), (1,)), ((), ())), preferred_element_type=F32)


def _log_sigmoid(z):
    return jnp.minimum(z, 0.0) - jnp.log1p(jnp.exp(-jnp.abs(z)))


def _split3(x):
    h1 = x.astype(BF16)
    r1 = x - h1.astype(F32)
    h2 = r1.astype(BF16)
    h3 = (r1 - h2.astype(F32)).astype(BF16)
    return h1, h2, h3


def _cumsum_rows(tri, x):
    h1, h2, h3 = _split3(x)
    return _dot(tri, h1) + _dot(tri, h2) + _dot(tri, h3)


def _layer_norm(u, g, b):
    mu = jnp.mean(u, axis=-1, keepdims=True)
    d = u - mu
    var = jnp.mean(d * d, axis=-1, keepdims=True)
    return d * lax.rsqrt(var + LN_EPS) * g + b


def _params(sem):
    return pltpu.CompilerParams(dimension_semantics=sem, vmem_limit_bytes=VMEM_LIMIT)


def _ln_kernel(x_ref, g_ref, b_ref, o_ref):
    o_ref[...] = _layer_norm(x_ref[...], g_ref[...], b_ref[...])


def _ln_call(x, g, b, tm):
    T, D = x.shape
    return pl.pallas_call(
        _ln_kernel,
        grid=(T // tm,),
        in_specs=[pl.BlockSpec((tm, D), lambda i: (i, 0)),
                  pl.BlockSpec((1, D), lambda i: (0, 0)),
                  pl.BlockSpec((1, D), lambda i: (0, 0))],
        out_specs=pl.BlockSpec((tm, D), lambda i: (i, 0)),
        out_shape=jax.ShapeDtypeStruct((T, D), F32),
        compiler_params=_params(("parallel",)),
        name="ln_in",
    )(x, g.reshape(1, D), b.reshape(1, D))


def _in_proj_kernel(x_ref, w_ref, wff_ref, cos_ref, sa_ref, sb_ref, o_ref, ff_ref):
    xb = x_ref[...].astype(BF16)
    ff_ref[...] = _dot(xb, wff_ref[...])
    for j in range(N_MAIN // 512):
        acc = _dot(xb, w_ref[:, j * 512:(j + 1) * 512])
        if j in ROPE_TILES:
            cosf, sa, sb = cos_ref[...], sa_ref[...], sb_ref[...]
            for g in range(4):
                t = acc[:, g * LANES:(g + 1) * LANES]
                r = t * cosf + pltpu.roll(t, 8, 1) * sa + pltpu.roll(t, LANES - 8, 1) * sb
                o_ref[:, j * 512 + g * LANES:j * 512 + (g + 1) * LANES] = r.astype(BF16)
        else:
            o_ref[:, j * 512:(j + 1) * 512] = acc.astype(BF16)


def _in_proj_call(x, w_main, w_ff, cosf, sina, sinb, tm):
    T, D = x.shape
    const = lambda i: (0, 0)
    row = lambda i: (i, 0)
    return pl.pallas_call(
        _in_proj_kernel,
        grid=(T // tm,),
        in_specs=[pl.BlockSpec((tm, D), row),
                  pl.BlockSpec((D, N_MAIN), const),
                  pl.BlockSpec((D, LANES), const),
                  pl.BlockSpec((tm, LANES), row),
                  pl.BlockSpec((tm, LANES), row),
                  pl.BlockSpec((tm, LANES), row)],
        out_specs=[pl.BlockSpec((tm, N_MAIN), row),
                   pl.BlockSpec((tm, LANES), row)],
        out_shape=[jax.ShapeDtypeStruct((T, N_MAIN), BF16),
                   jax.ShapeDtypeStruct((T, LANES), F32)],
        compiler_params=_params(("parallel",)),
        name="in_proj",
    )(x, w_main, w_ff, cosf, sina, sinb)


def _hgrn_chunk(z, ql, v, gl, lb, ng, state_t, b_scr, g_scr, consts):
    C = HG_CHUNK
    tri, levels, diag_masks, lane_c = consts
    u = jnp.exp(-jnp.abs(z))
    log1pu = jnp.log(1.0 + u)
    log_f = (jnp.minimum(z, 0.0) - log1pu) + jnp.log(1.0 + lb * jnp.exp(jnp.minimum(-z, EXP_CLAMP)))
    log2k = (jnp.minimum(-z, 0.0) - log1pu) * LOG2E + jnp.log2(1.0 - lb)
    k = jnp.exp2(log2k)
    q = ql * jax.nn.sigmoid(ql)
    b2 = _cumsum_rows(tri, log_f) * LOG2E
    b_scr[...] = b2
    g_scr[...] = log2k - b2
    vb = v.astype(BF16)
    b_last = b_scr[C - 1:C, :]

    o = _dot_nt((q * jnp.exp2(b2)).astype(BF16), state_t.astype(BF16))

    scores = jnp.zeros((C, C), F32)
    for m, is_query, pair in levels:
        pieces = [jnp.broadcast_to(b_scr[p * 2 * m + m - 1:p * 2 * m + m, :], (2 * m, LANES))
                  for p in range(C // (2 * m))]
        b_ref_rows = pieces[0] if len(pieces) == 1 else jnp.concatenate(pieces, axis=0)
        decay = jnp.exp2(-jnp.abs(b2 - b_ref_rows))
        qd = jnp.where(is_query, q * decay, 0.0).astype(BF16)
        kd = jnp.where(is_query, 0.0, k * decay).astype(BF16)
        scores = scores + jnp.where(pair, _dot_nt(qd, kd), 0.0)

    diag = []
    for blk in range(C // HG_SUB):
        lo = blk * HG_SUB
        bb = b2[lo:lo + HG_SUB]
        qq = q[lo:lo + HG_SUB]
        blk_scores = jnp.zeros((HG_SUB, C), F32)
        for s in range(HG_SUB):
            w = qq * jnp.exp2(bb + g_scr[lo + s:lo + s + 1, :])
            blk_scores = jnp.where(lane_c == lo + s, jnp.sum(w, axis=-1, keepdims=True), blk_scores)
        diag.append(jnp.where(diag_masks[blk], blk_scores, 0.0))
    scores = scores + jnp.concatenate(diag, axis=0)
    o = o + _dot(scores.astype(BF16), vb)

    k_dec = (k * jnp.exp2(b_last - b2)).astype(BF16)
    state_t = state_t * jnp.exp2(b_last) + _dot(v.T.astype(BF16), k_dec)

    ms = jnp.mean(o * o, axis=-1, keepdims=True)
    y = o * lax.rsqrt(ms + RMS_EPS) * ng * (gl * jax.nn.sigmoid(gl))
    return y, state_t


def _hgrn_kernel(q_ref, f_ref, i_ref, g_ref, lb_ref, ng_ref, o_ref, b_scr, g_scr, *, nchunks):
    C = HG_CHUNK
    ng = ng_ref[...]
    ri = lax.broadcasted_iota(jnp.int32, (C, C), 0)
    ci = lax.broadcasted_iota(jnp.int32, (C, C), 1)
    tri = jnp.where(ri >= ci, 1.0, 0.0).astype(BF16)
    rows = lax.broadcasted_iota(jnp.int32, (C, LANES), 0)
    sub_rows = lax.broadcasted_iota(jnp.int32, (HG_SUB, C), 0)
    lane_c = lax.broadcasted_iota(jnp.int32, (HG_SUB, C), 1)
    diag_masks = [sub_rows + blk * HG_SUB >= lane_c for blk in range(C // HG_SUB)]
    levels = []
    m = HG_SUB
    while m < C:
        shift = int(math.log2(2 * m))
        is_query = (rows & (2 * m - 1)) >= m
        pair = ((ri >> shift) == (ci >> shift)) & ((ri & (2 * m - 1)) >= m) & ((ci & (2 * m - 1)) < m)
        levels.append((m, is_query, pair))
        m *= 2
    consts = (tri, levels, diag_masks, lane_c)

    nb = q_ref.shape[0]

    def body(n, states):
        r0 = pl.multiple_of(n * C, C)
        new_states = []
        for bi in range(nb):
            for h in range(HG_HEADS):
                cols = slice(h * LANES, (h + 1) * LANES)
                y, st = _hgrn_chunk(f_ref[bi, pl.ds(r0, C), cols].astype(F32), q_ref[bi, pl.ds(r0, C), cols].astype(F32),
                                    i_ref[bi, pl.ds(r0, C), cols].astype(F32), g_ref[bi, pl.ds(r0, C), cols].astype(F32),
                                    lb_ref[:, cols], ng, states[bi * HG_HEADS + h],
                                    b_scr.at[bi * HG_HEADS + h], g_scr.at[bi * HG_HEADS + h], consts)
                o_ref[bi, pl.ds(r0, C), cols] = y.astype(o_ref.dtype)
                new_states.append(st)
        return tuple(new_states)

    lax.fori_loop(0, nchunks, body, tuple(jnp.zeros((HG_D, HG_D), F32) for _ in range(nb * HG_HEADS)))


def _hgrn_call(proj, lb, norm_g, B, S):
    W = HG_HEADS * HG_D
    nb = HG_BATCH if B % HG_BATCH == 0 else 1
    proj3 = proj.reshape(B, S, N_MAIN)
    blk = lambda off: pl.BlockSpec((nb, S, W), lambda b, off=off: (b, 0, off))
    out = pl.pallas_call(
        functools.partial(_hgrn_kernel, nchunks=S // HG_CHUNK),
        grid=(B // nb,),
        in_specs=[blk(COL_HQ // 4), blk(COL_HF // 4), blk(COL_HI // 4), blk(COL_HG // 4),
                  pl.BlockSpec((1, W), lambda b: (0, 0)),
                  pl.BlockSpec((1, LANES), lambda b: (0, 0))],
        out_specs=pl.BlockSpec((nb, S, W), lambda b: (b, 0, 0)),
        out_shape=jax.ShapeDtypeStruct((B, S, BRANCH_WIDTH), BF16),
        scratch_shapes=[pltpu.VMEM((nb * HG_HEADS, HG_CHUNK, LANES), F32)] * 2,
        compiler_params=_params(("parallel",)),
        name="hgrn",
    )(proj3, proj3, proj3, proj3, lb.reshape(1, W), norm_g.reshape(1, HG_D))
    return out.reshape(B * S, BRANCH_WIDTH)


def _fox_aug_tables():
    sel_q = np.zeros((3, LANES, BRANCH_WIDTH), np.float32)
    sel_k = np.zeros((3, LANES, BRANCH_WIDTH), np.float32)
    ones_q = np.zeros((1, BRANCH_WIDTH), np.float32)
    ones_k = np.zeros((1, BRANCH_WIDTH), np.float32)
    for col in range(BRANCH_WIDTH):
        pair, within = divmod(col, LANES)
        half, slot = divmod(within, FOX_DH)
        head = 2 * pair + 1 - half
        if slot < 3:
            sel_q[slot, head, col] = 1.0
            ones_k[0, col] = 1.0
        elif slot < 6:
            sel_k[slot - 3, head, col] = -1.0
            ones_q[0, col] = 1.0
    return sel_q, sel_k, ones_q, ones_k


def _fox_gate_kernel(ff_ref, bias_ref, selq_ref, selk_ref, oq_ref, ok_ref, qa_ref, ka_ref, *, nchunks):
    CH = GATE_CHUNK
    ri = lax.broadcasted_iota(jnp.int32, (CH, CH), 0)
    ci = lax.broadcasted_iota(jnp.int32, (CH, CH), 1)
    tri = jnp.where(ri >= ci, 1.0, 0.0).astype(BF16)
    bias = bias_ref[...]

    def body(n, carry):
        r0 = pl.multiple_of(n * CH, CH)
        c = carry + _cumsum_rows(tri, _log_sigmoid(ff_ref[pl.ds(r0, CH), :] + bias))
        parts = _split3(c * LOG2E)
        qa = oq_ref[...] + _dot(parts[0], selq_ref[0]) + _dot(parts[1], selq_ref[1]) + _dot(parts[2], selq_ref[2])
        ka = ok_ref[...] + _dot(parts[0], selk_ref[0]) + _dot(parts[1], selk_ref[1]) + _dot(parts[2], selk_ref[2])
        qa_ref[pl.ds(r0, CH), :] = qa.astype(BF16)
        ka_ref[pl.ds(r0, CH), :] = ka.astype(BF16)
        return c[CH - 1:CH, :]

    lax.fori_loop(0, nchunks, body, jnp.zeros((1, LANES), F32))


def _fox_gate_call(ff, bias, B, S):
    T = B * S
    bias_row = jnp.zeros((1, LANES), F32).at[0, :FOX_HEADS].set(bias.astype(F32))
    sel_q, sel_k, ones_q, ones_k = _fox_aug_tables()
    const2 = lambda b: (0, 0)
    const3 = lambda b: (0, 0, 0)
    return pl.pallas_call(
        functools.partial(_fox_gate_kernel, nchunks=S // GATE_CHUNK),
        grid=(B,),
        in_specs=[pl.BlockSpec((S, LANES), lambda b: (b, 0)),
                  pl.BlockSpec((1, LANES), const2),
                  pl.BlockSpec((3, LANES, BRANCH_WIDTH), const3),
                  pl.BlockSpec((3, LANES, BRANCH_WIDTH), const3),
                  pl.BlockSpec((1, BRANCH_WIDTH), const2),
                  pl.BlockSpec((1, BRANCH_WIDTH), const2)],
        out_specs=[pl.BlockSpec((S, BRANCH_WIDTH), lambda b: (b, 0)),
                   pl.BlockSpec((S, BRANCH_WIDTH), lambda b: (b, 0))],
        out_shape=[jax.ShapeDtypeStruct((T, BRANCH_WIDTH), BF16),
                   jax.ShapeDtypeStruct((T, BRANCH_WIDTH), BF16)],
        compiler_params=_params(("parallel",)),
        name="fox_gate",
    )(ff, bias_row, jnp.asarray(sel_q, BF16), jnp.asarray(sel_k, BF16), jnp.asarray(ones_q), jnp.asarray(ones_k))


def _flash_t(qs, k_fns, vt_fns, n_full, s_a, s_b):
    tq = tk = ATTN_BLOCK
    nmap = len(qs)

    def produce(dst, j):
        for i in range(nmap):
            dst[i] = _dot_nt(k_fns[i](j * tk, tk), qs[i])

    def update(s, m, l, acc, vt):
        m_new = jnp.maximum(m, jnp.max(s, axis=0, keepdims=True))
        alpha = jnp.exp2(m - m_new)
        p = jnp.exp2(s - m_new)
        return m_new, alpha * l + jnp.sum(p, axis=0, keepdims=True), alpha * acc + _dot(vt, p.astype(BF16))

    def consume(src, stats, j):
        return tuple(update(src[i], *stats[i], vt_fns[i](j * tk, tk)) for i in range(nmap))

    h = tk // 2
    mask_a = lax.broadcasted_iota(jnp.int32, (h, tq), 0) <= lax.broadcasted_iota(jnp.int32, (h, tq), 1)
    mask_b = lax.broadcasted_iota(jnp.int32, (h, h), 0) <= lax.broadcasted_iota(jnp.int32, (h, h), 1)

    def produce_diag(dst, j):
        for i in range(nmap):
            dst[i, 0:h, :] = _dot_nt(k_fns[i](j * tk, h), qs[i])
            dst[i, h:tk, h:tq] = _dot_nt(k_fns[i](j * tk + h, h), qs[i][h:tq])

    def consume_diag(src, stats, j):
        out = []
        for i in range(nmap):
            m, l, acc = update(jnp.where(mask_a, src[i, 0:h, :], NEG_INF), *stats[i], vt_fns[i](j * tk, h))
            m_b, l_b, acc_b = update(jnp.where(mask_b, src[i, h:tk, h:tq], NEG_INF), m[:, h:], l[:, h:], acc[:, h:],
                                     vt_fns[i](j * tk + h, h))
            out.append((jnp.concatenate([m[:, :h], m_b], axis=1), jnp.concatenate([l[:, :h], l_b], axis=1),
                        jnp.concatenate([acc[:, :h], acc_b], axis=1)))
        return tuple(out)

    stats = []
    for vt_fn in vt_fns:
        dv = vt_fn(0, tk).shape[0]
        stats.append((jnp.full((1, tq), NEG_INF, F32), jnp.zeros((1, tq), F32), jnp.zeros((dv, tq), F32)))

    def branch(n):
        def run(stats):
            bufs = (s_a, s_b)
            (produce if n > 0 else produce_diag)(bufs[0], 0)
            for j in range(n):
                (produce if j + 1 < n else produce_diag)(bufs[(j + 1) % 2], j + 1)
                stats = consume(bufs[j % 2], stats, j)
            return consume_diag(bufs[n % 2], stats, n)
        return run

    stats = branch(n_full)(tuple(stats))
    return [(acc, l) for (_, l, acc) in stats]


def _fill_vt(v_ref, vt_scr):
    S = v_ref.shape[0]
    for c in range(S // ATTN_BLOCK):
        sl = slice(c * ATTN_BLOCK, (c + 1) * ATTN_BLOCK)
        vt_scr[:, sl] = v_ref[sl, :].astype(F32).T.astype(BF16)


def _fox_kernel(q_ref, qa_ref, k_ref, ka_ref, v_ref, o_ref, vt_scr, s_a, s_b):
    tq = tk = ATTN_BLOCK
    _fill_vt(v_ref, vt_scr)
    lo = lax.broadcasted_iota(jnp.int32, (1, LANES), 1) < FOX_DH
    k_fns = [lambda k0, n: jnp.where(lo, k_ref[pl.ds(k0, n), :], ka_ref[pl.ds(k0, n), :]),
             lambda k0, n: jnp.where(lo, ka_ref[pl.ds(k0, n), :], k_ref[pl.ds(k0, n), :])]
    vt_fns = [lambda k0, n: vt_scr[0:FOX_DH, pl.ds(k0, n)],
              lambda k0, n: vt_scr[FOX_DH:2 * FOX_DH, pl.ds(k0, n)]]
    for qi in range(q_ref.shape[0] // tq):
        rows = slice(qi * tq, (qi + 1) * tq)
        qf = q_ref[rows, :].astype(F32) * (FOX_DH ** -0.5 * LOG2E)
        qa = qa_ref[rows, :].astype(F32)
        qs = [jnp.where(lo, qf, qa).astype(BF16), jnp.where(lo, qa, qf).astype(BF16)]
        (acc0, l0), (acc1, l1) = _flash_t(qs, k_fns, vt_fns, qi, s_a, s_b)
        o_t = jnp.concatenate([acc0 / l0, acc1 / l1], axis=0)
        o_ref[rows, :] = o_t.T.astype(o_ref.dtype)


def _fox_call(proj, qa, ka, B, S):
    T = B * S
    tq = ATTN_BLOCK
    seq = lambda off: pl.BlockSpec((S, LANES), lambda b, p, off=off: (b, off + p))
    return pl.pallas_call(
        _fox_kernel,
        grid=(B, FOX_HEADS // 2),
        in_specs=[seq(COL_FQ), seq(0), seq(COL_FK), seq(0), seq(COL_FV)],
        out_specs=seq(0),
        out_shape=jax.ShapeDtypeStruct((T, BRANCH_WIDTH), BF16),
        scratch_shapes=[pltpu.VMEM((LANES, S), BF16)] + [pltpu.VMEM((2, tq, tq), F32)] * 2,
        compiler_params=_params(("parallel", "parallel")),
        name="fox_attn",
    )(proj, qa, proj, ka, proj)


def _diff_kernel(q_ref, k_ref, v_ref, lam_ref, ng_ref, o_ref, vt_scr, s_a, s_b):
    tq = tk = ATTN_BLOCK
    _fill_vt(v_ref, vt_scr)
    lo = lax.broadcasted_iota(jnp.int32, (1, LANES), 1) < DIFF_DH
    k_fns = [lambda k0, n: k_ref[pl.ds(k0, n), :]] * 2
    vt_fns = [lambda k0, n: vt_scr[:, pl.ds(k0, n)]] * 2
    for qi in range(q_ref.shape[0] // tq):
        rows = slice(qi * tq, (qi + 1) * tq)
        qf = q_ref[rows, :].astype(F32) * (DIFF_DH ** -0.5 * LOG2E)
        qs = [jnp.where(lo, qf, 0.0).astype(BF16), jnp.where(lo, 0.0, qf).astype(BF16)]
        (acc0, l0), (acc1, l1) = _flash_t(qs, k_fns, vt_fns, qi, s_a, s_b)
        o = (acc0 / l0 - lam_ref[0:1, 0:1] * (acc1 / l1)).T
        ms = jnp.mean(o * o, axis=-1, keepdims=True)
        o_ref[rows, :] = (o * lax.rsqrt(ms + RMS_EPS) * ng_ref[...]).astype(o_ref.dtype)


def _diff_call(proj, lam_row, norm_row, B, S):
    T = B * S
    tq = ATTN_BLOCK
    seq = lambda off: pl.BlockSpec((S, LANES), lambda b, h, off=off: (b, off + h))
    const = pl.BlockSpec((1, LANES), lambda b, h: (0, 0))
    return pl.pallas_call(
        _diff_kernel,
        grid=(B, DIFF_HEADS),
        in_specs=[seq(COL_DQ), seq(COL_DK), seq(COL_DV), const, const],
        out_specs=seq(0),
        out_shape=jax.ShapeDtypeStruct((T, BRANCH_WIDTH), BF16),
        scratch_shapes=[pltpu.VMEM((LANES, S), BF16)] + [pltpu.VMEM((2, tq, tq), F32)] * 2,
        compiler_params=_params(("parallel", "parallel")),
        name="diff_attn",
    )(proj, proj, proj, lam_row, norm_row)


def _merge_kernel(x_ref, ya_ref, yb_ref, yc_ref, wg_ref, wb_ref, wo_ref, g1_ref, b1_ref, wr_ref, br_ref,
                  x1_ref, ids_ref, wts_ref, *, alpha):
    x = x_ref[...]
    xb = x.astype(BF16)
    merged = None
    for r, y_ref in enumerate((ya_ref, yb_ref, yc_ref)):
        gate = jax.nn.sigmoid(_dot(xb, wg_ref[:, r * D_MODEL:(r + 1) * D_MODEL]))
        term = gate * _dot(y_ref[...], wb_ref[r])
        merged = term if merged is None else merged + term
    h = _dot(merged.astype(BF16), wo_ref[...])
    x1 = _layer_norm(alpha * x + h, g1_ref[...], b1_ref[...])
    x1_ref[...] = x1

    logits = _dot(x1.astype(BF16), wr_ref[...]) + br_ref[...]
    lane = lax.broadcasted_iota(jnp.int32, logits.shape, 1)
    lane_f = lane.astype(F32)
    is_group = lane < N_GROUPS
    gl = jnp.where(is_group, logits, NEG_INF)
    gmax = jnp.max(gl, axis=-1, keepdims=True)
    gsum = jnp.sum(jnp.where(is_group, jnp.exp(gl - gmax), 0.0), axis=-1, keepdims=True)
    g_p = 1.0 / gsum
    g_idx = jnp.min(jnp.where(gl == gmax, lane_f, float(LANES)), axis=-1, keepdims=True)
    lo = N_GROUPS + EXPERTS_PER_GROUP * g_idx
    in_group = (lane_f >= lo) & (lane_f < lo + EXPERTS_PER_GROUP)
    el = jnp.where(in_group, logits, NEG_INF)
    v1 = jnp.max(el, axis=-1, keepdims=True)
    i1 = jnp.min(jnp.where(el == v1, lane_f, float(LANES)), axis=-1, keepdims=True)
    el2 = jnp.where(lane_f == i1, NEG_INF, el)
    v2 = jnp.max(el2, axis=-1, keepdims=True)
    i2 = jnp.min(jnp.where(el2 == v2, lane_f, float(LANES)), axis=-1, keepdims=True)
    t = jnp.exp(v2 - v1)
    w1 = g_p / (1.0 + t)
    w2 = g_p * t / (1.0 + t)
    ids = jnp.where(lane == 0, i1 - N_GROUPS, jnp.where(lane == 1, i2 - N_GROUPS, 0.0))
    ids_ref[...] = ids.astype(jnp.int32)
    wts_ref[...] = jnp.where(lane == 0, w1, jnp.where(lane == 1, w2, 0.0))


def _merge_call(x, ya, yb, yc, w_gates, w_branch, w_out, ln_g, ln_b, w_router, b_router, alpha, tm):
    T, D = x.shape
    row = lambda i: (i, 0)
    const2 = lambda i: (0, 0)
    return pl.pallas_call(
        functools.partial(_merge_kernel, alpha=alpha),
        grid=(T // tm,),
        in_specs=[pl.BlockSpec((tm, D), row),
                  pl.BlockSpec((tm, BRANCH_WIDTH), row),
                  pl.BlockSpec((tm, BRANCH_WIDTH), row),
                  pl.BlockSpec((tm, BRANCH_WIDTH), row),
                  pl.BlockSpec((D, N_BRANCHES * D), const2),
                  pl.BlockSpec((N_BRANCHES, BRANCH_WIDTH, D), lambda i: (0, 0, 0)),
                  pl.BlockSpec((D, D), const2),
                  pl.BlockSpec((1, D), const2),
                  pl.BlockSpec((1, D), const2),
                  pl.BlockSpec((D, LANES), const2),
                  pl.BlockSpec((1, LANES), const2)],
        out_specs=[pl.BlockSpec((tm, D), row),
                   pl.BlockSpec((tm, LANES), row),
                   pl.BlockSpec((tm, LANES), row)],
        out_shape=[jax.ShapeDtypeStruct((T, D), F32),
                   jax.ShapeDtypeStruct((T, LANES), jnp.int32),
                   jax.ShapeDtypeStruct((T, LANES), F32)],
        compiler_params=_params(("parallel",)),
        name="merge_ln1_router",
    )(x, ya, yb, yc, w_gates, w_branch, w_out, ln_g.reshape(1, D), ln_b.reshape(1, D), w_router, b_router)


def _row_copy(src, src_row, dst, dst_row, sem):
    return pltpu.make_async_copy(src.at[pl.ds(src_row, 1), :], dst.at[pl.ds(dst_row, 1), :], sem)


def _dispatch_kernel(pos_ref, pad_ref, x_ref, xs_hbm, zero_scr, sem, *, tm, npad, pad_steps):
    def issue(r, carry):
        _row_copy(x_ref, r, xs_hbm, pos_ref[2 * r], sem).start()
        _row_copy(x_ref, r, xs_hbm, pos_ref[2 * r + 1], sem).start()
        return carry

    lax.fori_loop(0, tm, issue, 0, unroll=8)

    @pl.when(pl.program_id(0) < pad_steps)
    def _():
        zero_scr[...] = jnp.zeros_like(zero_scr)

        def issue_pad(r, carry):
            _row_copy(zero_scr, r, xs_hbm, pad_ref[r], sem).start()
            return carry

        lax.fori_loop(0, npad, issue_pad, 0, unroll=8)
        pltpu.make_async_copy(zero_scr, xs_hbm.at[pl.ds(0, npad), :], sem).wait()

    for _ in range(2):
        pltpu.make_async_copy(x_ref, xs_hbm.at[pl.ds(0, tm), :], sem).wait()


def _dispatch_call(pos, pad_rows, x1, n_rows, tm):
    T, D = x1.shape
    nsteps = T // tm
    npad = max(LANES, pad_rows.shape[0] // nsteps)
    pad_steps = pad_rows.shape[0] // npad
    return pl.pallas_call(
        functools.partial(_dispatch_kernel, tm=tm, npad=npad, pad_steps=pad_steps),
        grid=(nsteps,),
        in_specs=[pl.BlockSpec((2 * tm,), lambda i: (i,), memory_space=pltpu.SMEM),
                  pl.BlockSpec((npad,), lambda i: (jnp.minimum(i, pad_steps - 1),), memory_space=pltpu.SMEM),
                  pl.BlockSpec((tm, D), lambda i: (i, 0))],
        out_specs=pl.BlockSpec(memory_space=pl.ANY),
        out_shape=jax.ShapeDtypeStruct((n_rows, D), F32),
        scratch_shapes=[pltpu.VMEM((npad, D), F32), pltpu.SemaphoreType.DMA(())],
        compiler_params=_params(("arbitrary",)),
        name="moe_dispatch",
    )(pos, pad_rows, x1)


def _expert_kernel(te_ref, nu_ref, xs_ref, wg_ref, wu_ref, wd_ref, o_ref, wgu_scr, wd_scr):
    t = pl.program_id(0)
    used = t < nu_ref[0]
    new_expert = (t == 0) | (te_ref[t] != te_ref[jnp.maximum(t - 1, 0)])

    @pl.when(used & new_expert)
    def _():
        wgu_scr[:, 0:D_EXPERT] = wg_ref[0, 0].astype(BF16)
        wgu_scr[:, D_EXPERT:2 * D_EXPERT] = wu_ref[0, 0].astype(BF16)
        wd_scr[...] = wd_ref[0, 0].astype(BF16)

    @pl.when(used)
    def _():
        gu = _dot(xs_ref[...].astype(BF16), wgu_scr[...])
        g = gu[:, 0:D_EXPERT]
        h = (g * jax.nn.sigmoid(g)) * gu[:, D_EXPERT:2 * D_EXPERT]
        o_ref[...] = _dot(h.astype(BF16), wd_scr[...])

    @pl.when(jnp.logical_not(used))
    def _():
        o_ref[...] = jnp.zeros_like(o_ref)


def _expert_call(tile_expert, n_used, xs, n_tiles, w_gate, w_up, w_down, layer):
    D = xs.shape[1]
    tm = EXPERT_TILE
    used = lambda t, te, nu: (jnp.minimum(t, nu[0] - 1), 0)
    expert = lambda t, te, nu: (layer, te[t], 0, 0)
    grid_spec = pltpu.PrefetchScalarGridSpec(
        num_scalar_prefetch=2,
        grid=(n_tiles,),
        in_specs=[pl.BlockSpec((tm, D), used),
                  pl.BlockSpec((1, 1, D, D_EXPERT), expert),
                  pl.BlockSpec((1, 1, D, D_EXPERT), expert),
                  pl.BlockSpec((1, 1, D_EXPERT, D), expert)],
        out_specs=pl.BlockSpec((tm, D), lambda t, te, nu: (t, 0)),
        scratch_shapes=[pltpu.VMEM((D, 2 * D_EXPERT), BF16), pltpu.VMEM((D_EXPERT, D), BF16)],
    )
    return pl.pallas_call(
        _expert_kernel,
        grid_spec=grid_spec,
        out_shape=jax.ShapeDtypeStruct((n_tiles * tm, D), F32),
        compiler_params=_params(("arbitrary",)),
        name="experts",
    )(tile_expert, n_used, xs, w_gate, w_up, w_down)


def _combine_kernel(pos_ref, x_ref, w_ref, g_ref, b_ref, y_hbm, o_ref, ybuf, sem, *, alpha, tm):
    def issue(r, carry):
        _row_copy(y_hbm, pos_ref[2 * r], ybuf.at[0], r, sem).start()
        _row_copy(y_hbm, pos_ref[2 * r + 1], ybuf.at[1], r, sem).start()
        return carry

    lax.fori_loop(0, tm, issue, 0, unroll=8)
    for k in range(2):
        pltpu.make_async_copy(y_hbm.at[pl.ds(0, tm), :], ybuf.at[k], sem).wait()
    w = w_ref[...]
    u = alpha * x_ref[...] + w[:, 0:1] * ybuf[0] + w[:, 1:2] * ybuf[1]
    o_ref[...] = _layer_norm(u, g_ref[...], b_ref[...])


def _combine_call(pos, x1, wts, ln_g, ln_b, y, alpha, tm):
    T, D = x1.shape
    row = lambda i: (i, 0)
    const2 = lambda i: (0, 0)
    return pl.pallas_call(
        functools.partial(_combine_kernel, alpha=alpha, tm=tm),
        grid=(T // tm,),
        in_specs=[pl.BlockSpec((2 * tm,), lambda i: (i,), memory_space=pltpu.SMEM),
                  pl.BlockSpec((tm, D), row),
                  pl.BlockSpec((tm, LANES), row),
                  pl.BlockSpec((1, D), const2), pl.BlockSpec((1, D), const2),
                  pl.BlockSpec(memory_space=pl.ANY)],
        out_specs=pl.BlockSpec((tm, D), row),
        out_shape=jax.ShapeDtypeStruct((T, D), F32),
        scratch_shapes=[pltpu.VMEM((2, tm, D), F32), pltpu.SemaphoreType.DMA(())],
        compiler_params=_params(("arbitrary",)),
        name="combine_ln2",
    )(pos, x1, wts, ln_g.reshape(1, D), ln_b.reshape(1, D), y)


def _rope_tables(positions):
    half = ROPE_DIM // 2
    inv_freq = ROPE_THETA ** (-jnp.arange(0, ROPE_DIM, 2, dtype=F32) / ROPE_DIM)
    ang = positions.astype(F32).reshape(-1, 1) * inv_freq[None, :]
    cos, sin = jnp.cos(ang), jnp.sin(ang)
    lane = jnp.arange(LANES)
    in_head = lane % DIFF_DH
    freq = in_head % half
    first = in_head < half
    second = (in_head >= half) & (in_head < ROPE_DIM)
    cosf = jnp.where((first | second)[None, :], cos[:, freq], 1.0)
    sin_a = jnp.where(second[None, :], sin[:, freq], 0.0)
    sin_b = jnp.where(first[None, :], -sin[:, freq], 0.0)
    return cosf, sin_a, sin_b


def _dispatch_plan(ids, T):
    tm = EXPERT_TILE
    flat = ids.reshape(-1)
    onehot = (flat[:, None] == jnp.arange(N_EXPERTS, dtype=jnp.int32)[None, :]).astype(jnp.int32)
    csum = jnp.cumsum(onehot, axis=0)
    counts = csum[-1]
    rank = jnp.take_along_axis(csum, flat[:, None], axis=1)[:, 0] - 1
    padded = ((counts + tm - 1) // tm) * tm
    ends = jnp.cumsum(padded)
    starts = ends - padded
    pos = (starts[flat] + rank).astype(jnp.int32)
    n_slab = 2 * T + N_EXPERTS * tm
    n_tiles = n_slab // tm
    tile_start = jnp.arange(n_tiles, dtype=jnp.int32) * tm
    tile_expert = jnp.minimum(jnp.sum((tile_start[:, None] >= ends[None, :]).astype(jnp.int32), axis=1),
                              N_EXPERTS - 1).astype(jnp.int32)
    n_used = (ends[-1] // tm).astype(jnp.int32).reshape(1)
    gap = padded - counts
    gap_end = jnp.cumsum(gap)
    gap_start = gap_end - gap
    j = jnp.arange(N_EXPERTS * tm, dtype=jnp.int32)
    e = jnp.minimum(jnp.sum((j[:, None] >= gap_end[None, :]).astype(jnp.int32), axis=1), N_EXPERTS - 1)
    in_tile = starts[e] + counts[e] + (j - gap_start[e])
    pad_rows = jnp.where(j < gap_end[-1], in_tile, ends[-1] + (j - gap_end[-1]))
    return pos, tile_expert, n_used, pad_rows.astype(jnp.int32), n_tiles


def _layer(x, cosf, sina, sinb, B, S, lb, p, alpha, lam_init):
    T = B * S
    tm = min(512, T)
    w_in = p["w_in"]
    w_main = jnp.concatenate([w_in[:, :3584], w_in[:, 3592:5128]], axis=1).astype(BF16)
    w_ff = jnp.zeros((D_MODEL, LANES), BF16).at[:, :FOX_HEADS].set(w_in[:, 3584:3592].astype(BF16))
    w_gates = w_in[:, 5128:].astype(BF16)

    proj, ff = _in_proj_call(x, w_main, w_ff, cosf, sina, sinb, tm)
    ya = _hgrn_call(proj, lb, p["hgrn_norm_g"], B, S)
    qa, ka = _fox_gate_call(ff, p["fox_f_bias"], B, S)
    yb = _fox_call(proj, qa, ka, B, S)
    lv = p["diff_lambda"].astype(F32)
    lam = jnp.exp(jnp.sum(lv[0] * lv[1])) - jnp.exp(jnp.sum(lv[2] * lv[3])) + lam_init
    lam_row = jnp.full((1, LANES), lam, F32)
    norm_row = (p["diff_norm_g"].astype(F32) * (1.0 - lam_init)).reshape(1, DIFF_DV)
    yc = _diff_call(proj, lam_row, norm_row, B, S)

    w_router = jnp.zeros((D_MODEL, LANES), F32)
    w_router = w_router.at[:, :N_GROUPS].set(p["router_g_w"]).at[:, N_GROUPS:N_GROUPS + N_EXPERTS].set(p["router_e_w"])
    b_router = jnp.zeros((1, LANES), F32)
    b_router = b_router.at[0, :N_GROUPS].set(p["router_g_b"]).at[0, N_GROUPS:N_GROUPS + N_EXPERTS].set(
        p["router_e_b"].reshape(-1))
    x1, ids, wts = _merge_call(x, ya, yb, yc, w_gates, p["w_branch"].astype(BF16), p["w_out"].astype(BF16),
                               p["ln1_g"], p["ln1_b"], w_router.astype(BF16), b_router, alpha, tm)

    pos, tile_expert, n_used, pad_rows, n_tiles = _dispatch_plan(ids[:, :2], T)
    xs = _dispatch_call(pos, pad_rows, x1, n_tiles * EXPERT_TILE, tm)
    y = _expert_call(tile_expert, n_used, xs, n_tiles, p["expert_w_gate"], p["expert_w_up"], p["expert_w_down"],
                     p["layer"])
    return _combine_call(pos, x1, wts, p["ln2_g"], p["ln2_b"], y, alpha, tm)


def kernel(x, positions, ln_in_g, ln_in_b, w_in, hgrn_lb_logits, hgrn_norm_g, fox_f_bias, diff_lambda,
           diff_norm_g, w_branch, w_out, ln1_g, ln1_b, router_g_w, router_g_b, router_e_w, router_e_b,
           expert_w_gate, expert_w_up, expert_w_down, ln2_g, ln2_b):
    B, S, D = x.shape
    T = B * S
    depth = w_in.shape[0]
    alpha = (2 * depth) ** 0.25
    cosf, sina, sinb = _rope_tables(positions)
    lb_soft = jax.nn.softmax(hgrn_lb_logits.astype(F32), axis=0)
    lower_bounds = jnp.maximum(jnp.cumsum(lb_soft, axis=0) - lb_soft[0], 0.0)

    h = _ln_call(x.reshape(T, D), ln_in_g, ln_in_b, min(512, T))
    for l in range(depth):
        p = dict(w_in=w_in[l], hgrn_norm_g=hgrn_norm_g[l], fox_f_bias=fox_f_bias[l], diff_lambda=diff_lambda[l],
                 diff_norm_g=diff_norm_g[l], w_branch=w_branch[l], w_out=w_out[l], ln1_g=ln1_g[l], ln1_b=ln1_b[l],
                 router_g_w=router_g_w[l], router_g_b=router_g_b[l], router_e_w=router_e_w[l],
                 router_e_b=router_e_b[l], expert_w_gate=expert_w_gate, expert_w_up=expert_w_up,
                 expert_w_down=expert_w_down, layer=l, ln2_g=ln2_g[l], ln2_b=ln2_b[l])
        lam_init = 0.8 - 0.6 * float(math.exp(-0.3 * l))
        h = _layer(h, cosf, sina, sinb, B, S, lower_bounds[l], p, alpha, lam_init)
    return h.reshape(B, S, D)
```

```python
import functools
import math

import numpy as np
import jax
import jax.numpy as jnp
from jax import lax
from jax.experimental import pallas as pl
from jax.experimental.pallas import tpu as pltpu

F32 = jnp.float32
BF16 = jnp.bfloat16

D_MODEL = 1024
HG_HEADS, HG_D = 4, 128
FOX_HEADS, FOX_DH = 8, 64
DIFF_HEADS, DIFF_DH, DIFF_DV = 4, 64, 128
BRANCH_WIDTH = 512
N_BRANCHES = 3
ROPE_THETA = 500000.0
ROPE_DIM = DIFF_DH // 4
N_GROUPS, EXPERTS_PER_GROUP = 4, 8
N_EXPERTS = N_GROUPS * EXPERTS_PER_GROUP
D_EXPERT = 512
LN_EPS = 1e-5
RMS_EPS = 1e-6
NEG_INF = -1e30
EXP_CLAMP = 60.0
LOG2E = 1.4426950408889634

LANES = 128
N_MAIN = 5120
COL_HQ, COL_HF, COL_HI, COL_HG = 0, 4, 8, 12
COL_FQ, COL_FK, COL_FV = 16, 20, 24
COL_DQ, COL_DK, COL_DV = 28, 32, 36
ROPE_TILES = (7, 8)

HG_CHUNK = 128
HG_SUB = 8
HG_BATCH = 2
GATE_CHUNK = 256
ATTN_BLOCK = 512
EXPERT_TILE = 512
PERMUTE_TILE = 1024
VMEM_LIMIT = 56 * 1024 * 1024


def _dot(a, b):
    return jnp.dot(a, b, preferred_element_type=F32)


def _dot_nt(a, b):
    return lax.dot_general(a, b, (((1,), (1,)), ((), ())), preferred_element_type=F32)


def _log_sigmoid(z):
    return jnp.minimum(z, 0.0) - jnp.log1p(jnp.exp(-jnp.abs(z)))


def _split3(x):
    h1 = x.astype(BF16)
    r1 = x - h1.astype(F32)
    h2 = r1.astype(BF16)
    h3 = (r1 - h2.astype(F32)).astype(BF16)
    return h1, h2, h3


def _cumsum_rows(tri, x):
    h1, h2, h3 = _split3(x)
    return _dot(tri, h1) + _dot(tri, h2) + _dot(tri, h3)


def _layer_norm(u, g, b):
    mu = jnp.mean(u, axis=-1, keepdims=True)
    d = u - mu
    var = jnp.mean(d * d, axis=-1, keepdims=True)
    return d * lax.rsqrt(var + LN_EPS) * g + b


def _params(sem):
    return pltpu.CompilerParams(dimension_semantics=sem, vmem_limit_bytes=VMEM_LIMIT)


def _ln_kernel(x_ref, g_ref, b_ref, o_ref):
    o_ref[...] = _layer_norm(x_ref[...], g_ref[...], b_ref[...])


def _ln_call(x, g, b, tm):
    T, D = x.shape
    return pl.pallas_call(
        _ln_kernel,
        grid=(T // tm,),
        in_specs=[pl.BlockSpec((tm, D), lambda i: (i, 0)),
                  pl.BlockSpec((1, D), lambda i: (0, 0)),
                  pl.BlockSpec((1, D), lambda i: (0, 0))],
        out_specs=pl.BlockSpec((tm, D), lambda i: (i, 0)),
        out_shape=jax.ShapeDtypeStruct((T, D), F32),
        compiler_params=_params(("parallel",)),
        name="ln_in",
    )(x, g.reshape(1, D), b.reshape(1, D))


def _in_proj_kernel(x_ref, w_ref, wff_ref, cos_ref, sa_ref, sb_ref, o_ref, ff_ref):
    xb = x_ref[...].astype(BF16)
    ff_ref[...] = _dot(xb, wff_ref[...])
    for j in range(N_MAIN // 512):
        acc = _dot(xb, w_ref[:, j * 512:(j + 1) * 512])
        if j in ROPE_TILES:
            cosf, sa, sb = cos_ref[...], sa_ref[...], sb_ref[...]
            for g in range(4):
                t = acc[:, g * LANES:(g + 1) * LANES]
                r = t * cosf + pltpu.roll(t, 8, 1) * sa + pltpu.roll(t, LANES - 8, 1) * sb
                o_ref[:, j * 512 + g * LANES:j * 512 + (g + 1) * LANES] = r.astype(BF16)
        else:
            o_ref[:, j * 512:(j + 1) * 512] = acc.astype(BF16)


def _in_proj_call(x, w_main, w_ff, cosf, sina, sinb, tm):
    T, D = x.shape
    const = lambda i: (0, 0)
    row = lambda i: (i, 0)
    return pl.pallas_call(
        _in_proj_kernel,
        grid=(T // tm,),
        in_specs=[pl.BlockSpec((tm, D), row),
                  pl.BlockSpec((D, N_MAIN), const),
                  pl.BlockSpec((D, LANES), const),
                  pl.BlockSpec((tm, LANES), row),
                  pl.BlockSpec((tm, LANES), row),
                  pl.BlockSpec((tm, LANES), row)],
        out_specs=[pl.BlockSpec((tm, N_MAIN), row),
                   pl.BlockSpec((tm, LANES), row)],
        out_shape=[jax.ShapeDtypeStruct((T, N_MAIN), BF16),
                   jax.ShapeDtypeStruct((T, LANES), F32)],
        compiler_params=_params(("parallel",)),
        name="in_proj",
    )(x, w_main, w_ff, cosf, sina, sinb)


def _hgrn_chunk(z, ql, v, gl, lb, ng, state_t, b_scr, g_scr, consts):
    C = HG_CHUNK
    tri, levels, diag_masks, lane_c = consts
    u = jnp.exp(-jnp.abs(z))
    log1pu = jnp.log(1.0 + u)
    log_f = (jnp.minimum(z, 0.0) - log1pu) + jnp.log(1.0 + lb * jnp.exp(jnp.minimum(-z, EXP_CLAMP)))
    log2k = (jnp.minimum(-z, 0.0) - log1pu) * LOG2E + jnp.log2(1.0 - lb)
    k = jnp.exp2(log2k)
    q = ql * jax.nn.sigmoid(ql)
    b2 = _cumsum_rows(tri, log_f) * LOG2E
    b_scr[...] = b2
    g_scr[...] = log2k - b2
    vb = v.astype(BF16)
    b_last = b_scr[C - 1:C, :]

    o = _dot_nt((q * jnp.exp2(b2)).astype(BF16), state_t.astype(BF16))

    scores = jnp.zeros((C, C), F32)
    for m, is_query, pair in levels:
        pieces = [jnp.broadcast_to(b_scr[p * 2 * m + m - 1:p * 2 * m + m, :], (2 * m, LANES))
                  for p in range(C // (2 * m))]
        b_ref_rows = pieces[0] if len(pieces) == 1 else jnp.concatenate(pieces, axis=0)
        decay = jnp.exp2(-jnp.abs(b2 - b_ref_rows))
        qd = jnp.where(is_query, q * decay, 0.0).astype(BF16)
        kd = jnp.where(is_query, 0.0, k * decay).astype(BF16)
        scores = scores + jnp.where(pair, _dot_nt(qd, kd), 0.0)

    diag = []
    for blk in range(C // HG_SUB):
        lo = blk * HG_SUB
        bb = b2[lo:lo + HG_SUB]
        qq = q[lo:lo + HG_SUB]
        blk_scores = jnp.zeros((HG_SUB, C), F32)
        for s in range(HG_SUB):
            w = qq * jnp.exp2(bb + g_scr[lo + s:lo + s + 1, :])
            blk_scores = jnp.where(lane_c == lo + s, jnp.sum(w, axis=-1, keepdims=True), blk_scores)
        diag.append(jnp.where(diag_masks[blk], blk_scores, 0.0))
    scores = scores + jnp.concatenate(diag, axis=0)
    o = o + _dot(scores.astype(BF16), vb)

    k_dec = (k * jnp.exp2(b_last - b2)).astype(BF16)
    state_t = state_t * jnp.exp2(b_last) + _dot(v.T.astype(BF16), k_dec)

    ms = jnp.mean(o * o, axis=-1, keepdims=True)
    y = o * lax.rsqrt(ms + RMS_EPS) * ng * (gl * jax.nn.sigmoid(gl))
    return y, state_t


def _hgrn_kernel(q_ref, f_ref, i_ref, g_ref, lb_ref, ng_ref, o_ref, b_scr, g_scr, *, nchunks):
    C = HG_CHUNK
    ng = ng_ref[...]
    ri = lax.broadcasted_iota(jnp.int32, (C, C), 0)
    ci = lax.broadcasted_iota(jnp.int32, (C, C), 1)
    tri = jnp.where(ri >= ci, 1.0, 0.0).astype(BF16)
    rows = lax.broadcasted_iota(jnp.int32, (C, LANES), 0)
    sub_rows = lax.broadcasted_iota(jnp.int32, (HG_SUB, C), 0)
    lane_c = lax.broadcasted_iota(jnp.int32, (HG_SUB, C), 1)
    diag_masks = [sub_rows + blk * HG_SUB >= lane_c for blk in range(C // HG_SUB)]
    levels = []
    m = HG_SUB
    while m < C:
        shift = int(math.log2(2 * m))
        is_query = (rows & (2 * m - 1)) >= m
        pair = ((ri >> shift) == (ci >> shift)) & ((ri & (2 * m - 1)) >= m) & ((ci & (2 * m - 1)) < m)
        levels.append((m, is_query, pair))
        m *= 2
    consts = (tri, levels, diag_masks, lane_c)

    nb = q_ref.shape[0]

    def body(n, states):
        r0 = pl.multiple_of(n * C, C)
        new_states = []
        for bi in range(nb):
            for h in range(HG_HEADS):
                cols = slice(h * LANES, (h + 1) * LANES)
                y, st = _hgrn_chunk(f_ref[bi, pl.ds(r0, C), cols].astype(F32), q_ref[bi, pl.ds(r0, C), cols].astype(F32),
                                    i_ref[bi, pl.ds(r0, C), cols].astype(F32), g_ref[bi, pl.ds(r0, C), cols].astype(F32),
                                    lb_ref[:, cols], ng, states[bi * HG_HEADS + h],
                                    b_scr.at[bi * HG_HEADS + h], g_scr.at[bi * HG_HEADS + h], consts)
                o_ref[bi, pl.ds(r0, C), cols] = y.astype(o_ref.dtype)
                new_states.append(st)
        return tuple(new_states)

    lax.fori_loop(0, nchunks, body, tuple(jnp.zeros((HG_D, HG_D), F32) for _ in range(nb * HG_HEADS)))


def _hgrn_call(proj, lb, norm_g, B, S):
    W = HG_HEADS * HG_D
    nb = HG_BATCH if B % HG_BATCH == 0 else 1
    proj3 = proj.reshape(B, S, N_MAIN)
    blk = lambda off: pl.BlockSpec((nb, S, W), lambda b, off=off: (b, 0, off))
    out = pl.pallas_call(
        functools.partial(_hgrn_kernel, nchunks=S // HG_CHUNK),
        grid=(B // nb,),
        in_specs=[blk(COL_HQ // 4), blk(COL_HF // 4), blk(COL_HI // 4), blk(COL_HG // 4),
                  pl.BlockSpec((1, W), lambda b: (0, 0)),
                  pl.BlockSpec((1, LANES), lambda b: (0, 0))],
        out_specs=pl.BlockSpec((nb, S, W), lambda b: (b, 0, 0)),
        out_shape=jax.ShapeDtypeStruct((B, S, BRANCH_WIDTH), BF16),
        scratch_shapes=[pltpu.VMEM((nb * HG_HEADS, HG_CHUNK, LANES), F32)] * 2,
        compiler_params=_params(("parallel",)),
        name="hgrn",
    )(proj3, proj3, proj3, proj3, lb.reshape(1, W), norm_g.reshape(1, HG_D))
    return out.reshape(B * S, BRANCH_WIDTH)


def _fox_aug_tables():
    sel_q = np.zeros((3, LANES, BRANCH_WIDTH), np.float32)
    sel_k = np.zeros((3, LANES, BRANCH_WIDTH), np.float32)
    ones_q = np.zeros((1, BRANCH_WIDTH), np.float32)
    ones_k = np.zeros((1, BRANCH_WIDTH), np.float32)
    for col in range(BRANCH_WIDTH):
        pair, within = divmod(col, LANES)
        half, slot = divmod(within, FOX_DH)
        head = 2 * pair + 1 - half
        if slot < 3:
            sel_q[slot, head, col] = 1.0
            ones_k[0, col] = 1.0
        elif slot < 6:
            sel_k[slot - 3, head, col] = -1.0
            ones_q[0, col] = 1.0
    return sel_q, sel_k, ones_q, ones_k


def _fox_gate_kernel(ff_ref, bias_ref, selq_ref, selk_ref, oq_ref, ok_ref, qa_ref, ka_ref, *, nchunks):
    CH = GATE_CHUNK
    ri = lax.broadcasted_iota(jnp.int32, (CH, CH), 0)
    ci = lax.broadcasted_iota(jnp.int32, (CH, CH), 1)
    tri = jnp.where(ri >= ci, 1.0, 0.0).astype(BF16)
    bias = bias_ref[...]

    def body(n, carry):
        r0 = pl.multiple_of(n * CH, CH)
        c = carry + _cumsum_rows(tri, _log_sigmoid(ff_ref[pl.ds(r0, CH), :] + bias))
        parts = _split3(c * LOG2E)
        qa = oq_ref[...] + _dot(parts[0], selq_ref[0]) + _dot(parts[1], selq_ref[1]) + _dot(parts[2], selq_ref[2])
        ka = ok_ref[...] + _dot(parts[0], selk_ref[0]) + _dot(parts[1], selk_ref[1]) + _dot(parts[2], selk_ref[2])
        qa_ref[pl.ds(r0, CH), :] = qa.astype(BF16)
        ka_ref[pl.ds(r0, CH), :] = ka.astype(BF16)
        return c[CH - 1:CH, :]

    lax.fori_loop(0, nchunks, body, jnp.zeros((1, LANES), F32))


def _fox_gate_call(ff, bias, B, S):
    T = B * S
    bias_row = jnp.zeros((1, LANES), F32).at[0, :FOX_HEADS].set(bias.astype(F32))
    sel_q, sel_k, ones_q, ones_k = _fox_aug_tables()
    const2 = lambda b: (0, 0)
    const3 = lambda b: (0, 0, 0)
    return pl.pallas_call(
        functools.partial(_fox_gate_kernel, nchunks=S // GATE_CHUNK),
        grid=(B,),
        in_specs=[pl.BlockSpec((S, LANES), lambda b: (b, 0)),
                  pl.BlockSpec((1, LANES), const2),
                  pl.BlockSpec((3, LANES, BRANCH_WIDTH), const3),
                  pl.BlockSpec((3, LANES, BRANCH_WIDTH), const3),
                  pl.BlockSpec((1, BRANCH_WIDTH), const2),
                  pl.BlockSpec((1, BRANCH_WIDTH), const2)],
        out_specs=[pl.BlockSpec((S, BRANCH_WIDTH), lambda b: (b, 0)),
                   pl.BlockSpec((S, BRANCH_WIDTH), lambda b: (b, 0))],
        out_shape=[jax.ShapeDtypeStruct((T, BRANCH_WIDTH), BF16),
                   jax.ShapeDtypeStruct((T, BRANCH_WIDTH), BF16)],
        compiler_params=_params(("parallel",)),
        name="fox_gate",
    )(ff, bias_row, jnp.asarray(sel_q, BF16), jnp.asarray(sel_k, BF16), jnp.asarray(ones_q), jnp.asarray(ones_k))


def _flash_t(qs, k_fns, vt_fns, n_full, s_a, s_b):
    tq = tk = ATTN_BLOCK
    nmap = len(qs)

    def produce(dst, j):
        for i in range(nmap):
            dst[i] = _dot_nt(k_fns[i](j * tk, tk), qs[i])

    def update(s, m, l, acc, vt):
        m_new = jnp.maximum(m, jnp.max(s, axis=0, keepdims=True))
        alpha = jnp.exp2(m - m_new)
        p = jnp.exp2(s - m_new)
        return m_new, alpha * l + jnp.sum(p, axis=0, keepdims=True), alpha * acc + _dot(vt, p.astype(BF16))

    def consume(src, stats, j):
        return tuple(update(src[i], *stats[i], vt_fns[i](j * tk, tk)) for i in range(nmap))

    h = tk // 2
    mask_a = lax.broadcasted_iota(jnp.int32, (h, tq), 0) <= lax.broadcasted_iota(jnp.int32, (h, tq), 1)
    mask_b = lax.broadcasted_iota(jnp.int32, (h, h), 0) <= lax.broadcasted_iota(jnp.int32, (h, h), 1)

    def produce_diag(dst, j):
        for i in range(nmap):
            dst[i, 0:h, :] = _dot_nt(k_fns[i](j * tk, h), qs[i])
            dst[i, h:tk, h:tq] = _dot_nt(k_fns[i](j * tk + h, h), qs[i][h:tq])

    def consume_diag(src, stats, j):
        out = []
        for i in range(nmap):
            m, l, acc = update(jnp.where(mask_a, src[i, 0:h, :], NEG_INF), *stats[i], vt_fns[i](j * tk, h))
            m_b, l_b, acc_b = update(jnp.where(mask_b, src[i, h:tk, h:tq], NEG_INF), m[:, h:], l[:, h:], acc[:, h:],
                                     vt_fns[i](j * tk + h, h))
            out.append((jnp.concatenate([m[:, :h], m_b], axis=1), jnp.concatenate([l[:, :h], l_b], axis=1),
                        jnp.concatenate([acc[:, :h], acc_b], axis=1)))
        return tuple(out)

    stats = []
    for vt_fn in vt_fns:
        dv = vt_fn(0, tk).shape[0]
        stats.append((jnp.full((1, tq), NEG_INF, F32), jnp.zeros((1, tq), F32), jnp.zeros((dv, tq), F32)))

    def branch(n):
        def run(stats):
            bufs = (s_a, s_b)
            (produce if n > 0 else produce_diag)(bufs[0], 0)
            for j in range(n):
                (produce if j + 1 < n else produce_diag)(bufs[(j + 1) % 2], j + 1)
                stats = consume(bufs[j % 2], stats, j)
            return consume_diag(bufs[n % 2], stats, n)
        return run

    stats = branch(n_full)(tuple(stats))
    return [(acc, l) for (_, l, acc) in stats]


def _fill_vt(v_ref, vt_scr):
    S = v_ref.shape[0]
    for c in range(S // ATTN_BLOCK):
        sl = slice(c * ATTN_BLOCK, (c + 1) * ATTN_BLOCK)
        vt_scr[:, sl] = v_ref[sl, :].astype(F32).T.astype(BF16)


def _fox_kernel(q_ref, qa_ref, k_ref, ka_ref, v_ref, o_ref, vt_scr, s_a, s_b):
    tq = tk = ATTN_BLOCK
    _fill_vt(v_ref, vt_scr)
    lo = lax.broadcasted_iota(jnp.int32, (1, LANES), 1) < FOX_DH
    k_fns = [lambda k0, n: jnp.where(lo, k_ref[pl.ds(k0, n), :], ka_ref[pl.ds(k0, n), :]),
             lambda k0, n: jnp.where(lo, ka_ref[pl.ds(k0, n), :], k_ref[pl.ds(k0, n), :])]
    vt_fns = [lambda k0, n: vt_scr[0:FOX_DH, pl.ds(k0, n)],
              lambda k0, n: vt_scr[FOX_DH:2 * FOX_DH, pl.ds(k0, n)]]
    for qi in range(q_ref.shape[0] // tq):
        rows = slice(qi * tq, (qi + 1) * tq)
        qf = q_ref[rows, :].astype(F32) * (FOX_DH ** -0.5 * LOG2E)
        qa = qa_ref[rows, :].astype(F32)
        qs = [jnp.where(lo, qf, qa).astype(BF16), jnp.where(lo, qa, qf).astype(BF16)]
        (acc0, l0), (acc1, l1) = _flash_t(qs, k_fns, vt_fns, qi, s_a, s_b)
        o_t = jnp.concatenate([acc0 / l0, acc1 / l1], axis=0)
        o_ref[rows, :] = o_t.T.astype(o_ref.dtype)


def _fox_call(proj, qa, ka, B, S):
    T = B * S
    tq = ATTN_BLOCK
    seq = lambda off: pl.BlockSpec((S, LANES), lambda b, p, off=off: (b, off + p))
    return pl.pallas_call(
        _fox_kernel,
        grid=(B, FOX_HEADS // 2),
        in_specs=[seq(COL_FQ), seq(0), seq(COL_FK), seq(0), seq(COL_FV)],
        out_specs=seq(0),
        out_shape=jax.ShapeDtypeStruct((T, BRANCH_WIDTH), BF16),
        scratch_shapes=[pltpu.VMEM((LANES, S), BF16)] + [pltpu.VMEM((2, tq, tq), F32)] * 2,
        compiler_params=_params(("parallel", "parallel")),
        name="fox_attn",
    )(proj, qa, proj, ka, proj)


def _diff_kernel(q_ref, k_ref, v_ref, lam_ref, ng_ref, o_ref, vt_scr, s_a, s_b):
    tq = tk = ATTN_BLOCK
    _fill_vt(v_ref, vt_scr)
    lo = lax.broadcasted_iota(jnp.int32, (1, LANES), 1) < DIFF_DH
    k_fns = [lambda k0, n: k_ref[pl.ds(k0, n), :]] * 2
    vt_fns = [lambda k0, n: vt_scr[:, pl.ds(k0, n)]] * 2
    for qi in range(q_ref.shape[0] // tq):
        rows = slice(qi * tq, (qi + 1) * tq)
        qf = q_ref[rows, :].astype(F32) * (DIFF_DH ** -0.5 * LOG2E)
        qs = [jnp.where(lo, qf, 0.0).astype(BF16), jnp.where(lo, 0.0, qf).astype(BF16)]
        (acc0, l0), (acc1, l1) = _flash_t(qs, k_fns, vt_fns, qi, s_a, s_b)
        o = (acc0 / l0 - lam_ref[0:1, 0:1] * (acc1 / l1)).T
        ms = jnp.mean(o * o, axis=-1, keepdims=True)
        o_ref[rows, :] = (o * lax.rsqrt(ms + RMS_EPS) * ng_ref[...]).astype(o_ref.dtype)


def _diff_call(proj, lam_row, norm_row, B, S):
    T = B * S
    tq = ATTN_BLOCK
    seq = lambda off: pl.BlockSpec((S, LANES), lambda b, h, off=off: (b, off + h))
    const = pl.BlockSpec((1, LANES), lambda b, h: (0, 0))
    return pl.pallas_call(
        _diff_kernel,
        grid=(B, DIFF_HEADS),
        in_specs=[seq(COL_DQ), seq(COL_DK), seq(COL_DV), const, const],
        out_specs=seq(0),
        out_shape=jax.ShapeDtypeStruct((T, BRANCH_WIDTH), BF16),
        scratch_shapes=[pltpu.VMEM((LANES, S), BF16)] + [pltpu.VMEM((2, tq, tq), F32)] * 2,
        compiler_params=_params(("parallel", "parallel")),
        name="diff_attn",
    )(proj, proj, proj, lam_row, norm_row)


def _merge_kernel(x_ref, ya_ref, yb_ref, yc_ref, wg_ref, wb_ref, wo_ref, g1_ref, b1_ref, wr_ref, br_ref,
                  x1_ref, ids_ref, wts_ref, *, alpha):
    x = x_ref[...]
    xb = x.astype(BF16)
    merged = None
    for r, y_ref in enumerate((ya_ref, yb_ref, yc_ref)):
        gate = jax.nn.sigmoid(_dot(xb, wg_ref[:, r * D_MODEL:(r + 1) * D_MODEL]))
        term = gate * _dot(y_ref[...], wb_ref[r])
        merged = term if merged is None else merged + term
    h = _dot(merged.astype(BF16), wo_ref[...])
    x1 = _layer_norm(alpha * x + h, g1_ref[...], b1_ref[...])
    x1_ref[...] = x1

    logits = _dot(x1.astype(BF16), wr_ref[...]) + br_ref[...]
    lane = lax.broadcasted_iota(jnp.int32, logits.shape, 1)
    lane_f = lane.astype(F32)
    is_group = lane < N_GROUPS
    gl = jnp.where(is_group, logits, NEG_INF)
    gmax = jnp.max(gl, axis=-1, keepdims=True)
    gsum = jnp.sum(jnp.where(is_group, jnp.exp(gl - gmax), 0.0), axis=-1, keepdims=True)
    g_p = 1.0 / gsum
    g_idx = jnp.min(jnp.where(gl == gmax, lane_f, float(LANES)), axis=-1, keepdims=True)
    lo = N_GROUPS + EXPERTS_PER_GROUP * g_idx
    in_group = (lane_f >= lo) & (lane_f < lo + EXPERTS_PER_GROUP)
    el = jnp.where(in_group, logits, NEG_INF)
    v1 = jnp.max(el, axis=-1, keepdims=True)
    i1 = jnp.min(jnp.where(el == v1, lane_f, float(LANES)), axis=-1, keepdims=True)
    el2 = jnp.where(lane_f == i1, NEG_INF, el)
    v2 = jnp.max(el2, axis=-1, keepdims=True)
    i2 = jnp.min(jnp.where(el2 == v2, lane_f, float(LANES)), axis=-1, keepdims=True)
    t = jnp.exp(v2 - v1)
    w1 = g_p / (1.0 + t)
    w2 = g_p * t / (1.0 + t)
    ids = jnp.where(lane == 0, i1 - N_GROUPS, jnp.where(lane == 1, i2 - N_GROUPS, 0.0))
    ids_ref[...] = ids.astype(jnp.int32)
    wts_ref[...] = jnp.where(lane == 0, w1, jnp.where(lane == 1, w2, 0.0))


def _merge_call(x, ya, yb, yc, w_gates, w_branch, w_out, ln_g, ln_b, w_router, b_router, alpha, tm):
    T, D = x.shape
    row = lambda i: (i, 0)
    const2 = lambda i: (0, 0)
    return pl.pallas_call(
        functools.partial(_merge_kernel, alpha=alpha),
        grid=(T // tm,),
        in_specs=[pl.BlockSpec((tm, D), row),
                  pl.BlockSpec((tm, BRANCH_WIDTH), row),
                  pl.BlockSpec((tm, BRANCH_WIDTH), row),
                  pl.BlockSpec((tm, BRANCH_WIDTH), row),
                  pl.BlockSpec((D, N_BRANCHES * D), const2),
                  pl.BlockSpec((N_BRANCHES, BRANCH_WIDTH, D), lambda i: (0, 0, 0)),
                  pl.BlockSpec((D, D), const2),
                  pl.BlockSpec((1, D), const2),
                  pl.BlockSpec((1, D), const2),
                  pl.BlockSpec((D, LANES), const2),
                  pl.BlockSpec((1, LANES), const2)],
        out_specs=[pl.BlockSpec((tm, D), row),
                   pl.BlockSpec((tm, LANES), row),
                   pl.BlockSpec((tm, LANES), row)],
        out_shape=[jax.ShapeDtypeStruct((T, D), F32),
                   jax.ShapeDtypeStruct((T, LANES), jnp.int32),
                   jax.ShapeDtypeStruct((T, LANES), F32)],
        compiler_params=_params(("parallel",)),
        name="merge_ln1_router",
    )(x, ya, yb, yc, w_gates, w_branch, w_out, ln_g.reshape(1, D), ln_b.reshape(1, D), w_router, b_router)


def _row_copy(src, src_row, dst, dst_row, sem):
    return pltpu.make_async_copy(src.at[pl.ds(src_row, 1), :], dst.at[pl.ds(dst_row, 1), :], sem)


def _dispatch_kernel(pos_ref, pad_ref, x_ref, xs_hbm, zero_scr, sem, *, tm, npad, pad_steps):
    def issue(r, carry):
        _row_copy(x_ref, r, xs_hbm, pos_ref[2 * r], sem).start()
        _row_copy(x_ref, r, xs_hbm, pos_ref[2 * r + 1], sem).start()
        return carry

    lax.fori_loop(0, tm, issue, 0, unroll=8)

    @pl.when(pl.program_id(0) < pad_steps)
    def _():
        zero_scr[...] = jnp.zeros_like(zero_scr)

        def issue_pad(r, carry):
            _row_copy(zero_scr, r, xs_hbm, pad_ref[r], sem).start()
            return carry

        lax.fori_loop(0, npad, issue_pad, 0, unroll=8)
        pltpu.make_async_copy(zero_scr, xs_hbm.at[pl.ds(0, npad), :], sem).wait()

    for _ in range(2):
        pltpu.make_async_copy(x_ref, xs_hbm.at[pl.ds(0, tm), :], sem).wait()


def _dispatch_call(pos, pad_rows, x1, n_rows, tm):
    T, D = x1.shape
    nsteps = T // tm
    npad = max(LANES, pad_rows.shape[0] // nsteps)
    pad_steps = pad_rows.shape[0] // npad
    return pl.pallas_call(
        functools.partial(_dispatch_kernel, tm=tm, npad=npad, pad_steps=pad_steps),
        grid=(nsteps,),
        in_specs=[pl.BlockSpec((2 * tm,), lambda i: (i,), memory_space=pltpu.SMEM),
                  pl.BlockSpec((npad,), lambda i: (jnp.minimum(i, pad_steps - 1),), memory_space=pltpu.SMEM),
                  pl.BlockSpec((tm, D), lambda i: (i, 0))],
        out_specs=pl.BlockSpec(memory_space=pl.ANY),
        out_shape=jax.ShapeDtypeStruct((n_rows, D), F32),
        scratch_shapes=[pltpu.VMEM((npad, D), F32), pltpu.SemaphoreType.DMA(())],
        compiler_params=_params(("arbitrary",)),
        name="moe_dispatch",
    )(pos, pad_rows, x1)


def _expert_kernel(te_ref, nu_ref, xs_ref, wg_ref, wu_ref, wd_ref, o_ref, wgu_scr, wd_scr):
    t = pl.program_id(0)
    used = t < nu_ref[0]
    new_expert = (t == 0) | (te_ref[t] != te_ref[jnp.maximum(t - 1, 0)])

    @pl.when(used & new_expert)
    def _():
        wgu_scr[:, 0:D_EXPERT] = wg_ref[0, 0].astype(BF16)
        wgu_scr[:, D_EXPERT:2 * D_EXPERT] = wu_ref[0, 0].astype(BF16)
        wd_scr[...] = wd_ref[0, 0].astype(BF16)

    @pl.when(used)
    def _():
        gu = _dot(xs_ref[...].astype(BF16), wgu_scr[...])
        g = gu[:, 0:D_EXPERT]
        h = (g * jax.nn.sigmoid(g)) * gu[:, D_EXPERT:2 * D_EXPERT]
        o_ref[...] = _dot(h.astype(BF16), wd_scr[...])

    @pl.when(jnp.logical_not(used))
    def _():
        o_ref[...] = jnp.zeros_like(o_ref)


def _expert_call(tile_expert, n_used, xs, n_tiles, w_gate, w_up, w_down, layer):
    D = xs.shape[1]
    tm = EXPERT_TILE
    used = lambda t, te, nu: (jnp.minimum(t, nu[0] - 1), 0)
    expert = lambda t, te, nu: (layer, te[t], 0, 0)
    grid_spec = pltpu.PrefetchScalarGridSpec(
        num_scalar_prefetch=2,
        grid=(n_tiles,),
        in_specs=[pl.BlockSpec((tm, D), used),
                  pl.BlockSpec((1, 1, D, D_EXPERT), expert),
                  pl.BlockSpec((1, 1, D, D_EXPERT), expert),
                  pl.BlockSpec((1, 1, D_EXPERT, D), expert)],
        out_specs=pl.BlockSpec((tm, D), lambda t, te, nu: (t, 0)),
        scratch_shapes=[pltpu.VMEM((D, 2 * D_EXPERT), BF16), pltpu.VMEM((D_EXPERT, D), BF16)],
    )
    return pl.pallas_call(
        _expert_kernel,
        grid_spec=grid_spec,
        out_shape=jax.ShapeDtypeStruct((n_tiles * tm, D), F32),
        compiler_params=_params(("arbitrary",)),
        name="experts",
    )(tile_expert, n_used, xs, w_gate, w_up, w_down)


def _combine_kernel(pos_ref, x_ref, w_ref, g_ref, b_ref, y_hbm, o_ref, ybuf, sem, *, alpha, tm):
    def issue(r, carry):
        _row_copy(y_hbm, pos_ref[2 * r], ybuf.at[0], r, sem).start()
        _row_copy(y_hbm, pos_ref[2 * r + 1], ybuf.at[1], r, sem).start()
        return carry

    lax.fori_loop(0, tm, issue, 0, unroll=8)
    for k in range(2):
        pltpu.make_async_copy(y_hbm.at[pl.ds(0, tm), :], ybuf.at[k], sem).wait()
    w = w_ref[...]
    u = alpha * x_ref[...] + w[:, 0:1] * ybuf[0] + w[:, 1:2] * ybuf[1]
    o_ref[...] = _layer_norm(u, g_ref[...], b_ref[...])


def _combine_call(pos, x1, wts, ln_g, ln_b, y, alpha, tm):
    T, D = x1.shape
    row = lambda i: (i, 0)
    const2 = lambda i: (0, 0)
    return pl.pallas_call(
        functools.partial(_combine_kernel, alpha=alpha, tm=tm),
        grid=(T // tm,),
        in_specs=[pl.BlockSpec((2 * tm,), lambda i: (i,), memory_space=pltpu.SMEM),
                  pl.BlockSpec((tm, D), row),
                  pl.BlockSpec((tm, LANES), row),
                  pl.BlockSpec((1, D), const2), pl.BlockSpec((1, D), const2),
                  pl.BlockSpec(memory_space=pl.ANY)],
        out_specs=pl.BlockSpec((tm, D), row),
        out_shape=jax.ShapeDtypeStruct((T, D), F32),
        scratch_shapes=[pltpu.VMEM((2, tm, D), F32), pltpu.SemaphoreType.DMA(())],
        compiler_params=_params(("arbitrary",)),
        name="combine_ln2",
    )(pos, x1, wts, ln_g.reshape(1, D), ln_b.reshape(1, D), y)


def _rope_tables(positions):
    half = ROPE_DIM // 2
    inv_freq = ROPE_THETA ** (-jnp.arange(0, ROPE_DIM, 2, dtype=F32) / ROPE_DIM)
    ang = positions.astype(F32).reshape(-1, 1) * inv_freq[None, :]
    cos, sin = jnp.cos(ang), jnp.sin(ang)
    lane = jnp.arange(LANES)
    in_head = lane % DIFF_DH
    freq = in_head % half
    first = in_head < half
    second = (in_head >= half) & (in_head < ROPE_DIM)
    cosf = jnp.where((first | second)[None, :], cos[:, freq], 1.0)
    sin_a = jnp.where(second[None, :], sin[:, freq], 0.0)
    sin_b = jnp.where(first[None, :], -sin[:, freq], 0.0)
    return cosf, sin_a, sin_b


def _dispatch_plan(ids, T):
    tm = EXPERT_TILE
    flat = ids.reshape(-1)
    onehot = (flat[:, None] == jnp.arange(N_EXPERTS, dtype=jnp.int32)[None, :]).astype(jnp.int32)
    csum = jnp.cumsum(onehot, axis=0)
    counts = csum[-1]
    rank = jnp.take_along_axis(csum, flat[:, None], axis=1)[:, 0] - 1
    padded = ((counts + tm - 1) // tm) * tm
    ends = jnp.cumsum(padded)
    starts = ends - padded
    pos = (starts[flat] + rank).astype(jnp.int32)
    n_slab = 2 * T + N_EXPERTS * tm
    n_tiles = n_slab // tm
    tile_start = jnp.arange(n_tiles, dtype=jnp.int32) * tm
    tile_expert = jnp.minimum(jnp.sum((tile_start[:, None] >= ends[None, :]).astype(jnp.int32), axis=1),
                              N_EXPERTS - 1).astype(jnp.int32)
    n_used = (ends[-1] // tm).astype(jnp.int32).reshape(1)
    gap = padded - counts
    gap_end = jnp.cumsum(gap)
    gap_start = gap_end - gap
    j = jnp.arange(N_EXPERTS * tm, dtype=jnp.int32)
    e = jnp.minimum(jnp.sum((j[:, None] >= gap_end[None, :]).astype(jnp.int32), axis=1), N_EXPERTS - 1)
    in_tile = starts[e] + counts[e] + (j - gap_start[e])
    pad_rows = jnp.where(j < gap_end[-1], in_tile, ends[-1] + (j - gap_end[-1]))
    return pos, tile_expert, n_used, pad_rows.astype(jnp.int32), n_tiles


def _layer(x, cosf, sina, sinb, B, S, lb, p, alpha, lam_init):
    T = B * S
    tm = min(512, T)
    w_in = p["w_in"]
    w_main = jnp.concatenate([w_in[:, :3584], w_in[:, 3592:5128]], axis=1).astype(BF16)
    w_ff = jnp.zeros((D_MODEL, LANES), BF16).at[:, :FOX_HEADS].set(w_in[:, 3584:3592].astype(BF16))
    w_gates = w_in[:, 5128:].astype(BF16)

    proj, ff = _in_proj_call(x, w_main, w_ff, cosf, sina, sinb, tm)
    ya = _hgrn_call(proj, lb, p["hgrn_norm_g"], B, S)
    qa, ka = _fox_gate_call(ff, p["fox_f_bias"], B, S)
    yb = _fox_call(proj, qa, ka, B, S)
    lv = p["diff_lambda"].astype(F32)
    lam = jnp.exp(jnp.sum(lv[0] * lv[1])) - jnp.exp(jnp.sum(lv[2] * lv[3])) + lam_init
    lam_row = jnp.full((1, LANES), lam, F32)
    norm_row = (p["diff_norm_g"].astype(F32) * (1.0 - lam_init)).reshape(1, DIFF_DV)
    yc = _diff_call(proj, lam_row, norm_row, B, S)

    w_router = jnp.zeros((D_MODEL, LANES), F32)
    w_router = w_router.at[:, :N_GROUPS].set(p["router_g_w"]).at[:, N_GROUPS:N_GROUPS + N_EXPERTS].set(p["router_e_w"])
    b_router = jnp.zeros((1, LANES), F32)
    b_router = b_router.at[0, :N_GROUPS].set(p["router_g_b"]).at[0, N_GROUPS:N_GROUPS + N_EXPERTS].set(
        p["router_e_b"].reshape(-1))
    x1, ids, wts = _merge_call(x, ya, yb, yc, w_gates, p["w_branch"].astype(BF16), p["w_out"].astype(BF16),
                               p["ln1_g"], p["ln1_b"], w_router.astype(BF16), b_router, alpha, tm)

    pos, tile_expert, n_used, pad_rows, n_tiles = _dispatch_plan(ids[:, :2], T)
    tp = min(PERMUTE_TILE, T)
    xs = _dispatch_call(pos, pad_rows, x1, n_tiles * EXPERT_TILE, tp)
    y = _expert_call(tile_expert, n_used, xs, n_tiles, p["expert_w_gate"], p["expert_w_up"], p["expert_w_down"],
                     p["layer"])
    return _combine_call(pos, x1, wts, p["ln2_g"], p["ln2_b"], y, alpha, tp)


def kernel(x, positions, ln_in_g, ln_in_b, w_in, hgrn_lb_logits, hgrn_norm_g, fox_f_bias, diff_lambda,
           diff_norm_g, w_branch, w_out, ln1_g, ln1_b, router_g_w, router_g_b, router_e_w, router_e_b,
           expert_w_gate, expert_w_up, expert_w_down, ln2_g, ln2_b):
    B, S, D = x.shape
    T = B * S
    depth = w_in.shape[0]
    alpha = (2 * depth) ** 0.25
    cosf, sina, sinb = _rope_tables(positions)
    lb_soft = jax.nn.softmax(hgrn_lb_logits.astype(F32), axis=0)
    lower_bounds = jnp.maximum(jnp.cumsum(lb_soft, axis=0) - lb_soft[0], 0.0)

    h = _ln_call(x.reshape(T, D), ln_in_g, ln_in_b, min(512, T))
    for l in range(depth):
        p = dict(w_in=w_in[l], hgrn_norm_g=hgrn_norm_g[l], fox_f_bias=fox_f_bias[l], diff_lambda=diff_lambda[l],
                 diff_norm_g=diff_norm_g[l], w_branch=w_branch[l], w_out=w_out[l], ln1_g=ln1_g[l], ln1_b=ln1_b[l],
                 router_g_w=router_g_w[l], router_g_b=router_g_b[l], router_e_w=router_e_w[l],
                 router_e_b=router_e_b[l], expert_w_gate=expert_w_gate, expert_w_up=expert_w_up,
                 expert_w_down=expert_w_down, layer=l, ln2_g=ln2_g[l], ln2_b=ln2_b[l])
        lam_init = 0.8 - 0.6 * float(math.exp(-0.3 * l))
        h = _layer(h, cosf, sina, sinb, B, S, lower_bounds[l], p, alpha, lam_init)
    return h.reshape(B, S, D)
```

```python
import functools
import math

import numpy as np
import jax
import jax.numpy as jnp
from jax import lax
from jax.experimental import pallas as pl
from jax.experimental.pallas import tpu as pltpu

F32 = jnp.float32
BF16 = jnp.bfloat16

D_MODEL = 1024
HG_HEADS, HG_D = 4, 128
FOX_HEADS, FOX_DH = 8, 64
DIFF_HEADS, DIFF_DH, DIFF_DV = 4, 64, 128
BRANCH_WIDTH = 512
N_BRANCHES = 3
ROPE_THETA = 500000.0
ROPE_DIM = DIFF_DH // 4
N_GROUPS, EXPERTS_PER_GROUP = 4, 8
N_EXPERTS = N_GROUPS * EXPERTS_PER_GROUP
D_EXPERT = 512
LN_EPS = 1e-5
RMS_EPS = 1e-6
NEG_INF = -1e30
EXP_CLAMP = 60.0
LOG2E = 1.4426950408889634

LANES = 128
N_MAIN = 5120
COL_HQ, COL_HF, COL_HI, COL_HG = 0, 4, 8, 12
COL_FQ, COL_FK, COL_FV = 16, 20, 24
COL_DQ, COL_DK, COL_DV = 28, 32, 36
ROPE_TILES = (7, 8)

HG_CHUNK = 128
HG_SUB = 8
HG_BATCH = 2
GATE_CHUNK = 256
ATTN_BLOCK = 512
EXPERT_TILE = 512
PERMUTE_TILE = 1024
VMEM_LIMIT = 56 * 1024 * 1024


def _dot(a, b):
    return jnp.dot(a, b, preferred_element_type=F32)


def _dot_nt(a, b):
    return lax.dot_general(a, b, (((1,), (1,)), ((), ())), preferred_element_type=F32)


def _log_sigmoid(z):
    return jnp.minimum(z, 0.0) - jnp.log1p(jnp.exp(-jnp.abs(z)))


def _split3(x):
    h1 = x.astype(BF16)
    r1 = x - h1.astype(F32)
    h2 = r1.astype(BF16)
    h3 = (r1 - h2.astype(F32)).astype(BF16)
    return h1, h2, h3


def _cumsum_rows(tri, x):
    h1, h2, h3 = _split3(x)
    return _dot(tri, h1) + _dot(tri, h2) + _dot(tri, h3)


def _layer_norm(u, g, b):
    mu = jnp.mean(u, axis=-1, keepdims=True)
    d = u - mu
    var = jnp.mean(d * d, axis=-1, keepdims=True)
    return d * lax.rsqrt(var + LN_EPS) * g + b


def _params(sem):
    return pltpu.CompilerParams(dimension_semantics=sem, vmem_limit_bytes=VMEM_LIMIT)


def _in_proj_kernel(x_ref, w_ref, wff_ref, cos_ref, sa_ref, sb_ref, *rest, pre_ln):
    if pre_ln:
        g_ref, b_ref, o_ref, ff_ref, xn_ref = rest
        x = _layer_norm(x_ref[...], g_ref[...], b_ref[...])
        xn_ref[...] = x
    else:
        o_ref, ff_ref = rest
        x = x_ref[...]
    xb = x.astype(BF16)
    ff_ref[...] = _dot(xb, wff_ref[...])
    for j in range(N_MAIN // 512):
        acc = _dot(xb, w_ref[:, j * 512:(j + 1) * 512])
        if j in ROPE_TILES:
            cosf, sa, sb = cos_ref[...], sa_ref[...], sb_ref[...]
            for g in range(4):
                t = acc[:, g * LANES:(g + 1) * LANES]
                r = t * cosf + pltpu.roll(t, 8, 1) * sa + pltpu.roll(t, LANES - 8, 1) * sb
                o_ref[:, j * 512 + g * LANES:j * 512 + (g + 1) * LANES] = r.astype(BF16)
        else:
            o_ref[:, j * 512:(j + 1) * 512] = acc.astype(BF16)


def _in_proj_call(x, w_main, w_ff, cosf, sina, sinb, tm, pre_ln=None):
    T, D = x.shape
    const = lambda i: (0, 0)
    row = lambda i: (i, 0)
    in_specs = [pl.BlockSpec((tm, D), row),
                pl.BlockSpec((D, N_MAIN), const),
                pl.BlockSpec((D, LANES), const),
                pl.BlockSpec((tm, LANES), row),
                pl.BlockSpec((tm, LANES), row),
                pl.BlockSpec((tm, LANES), row)]
    out_specs = [pl.BlockSpec((tm, N_MAIN), row), pl.BlockSpec((tm, LANES), row)]
    out_shape = [jax.ShapeDtypeStruct((T, N_MAIN), BF16), jax.ShapeDtypeStruct((T, LANES), F32)]
    args = [x, w_main, w_ff, cosf, sina, sinb]
    if pre_ln is not None:
        in_specs += [pl.BlockSpec((1, D), const)] * 2
        out_specs.append(pl.BlockSpec((tm, D), row))
        out_shape.append(jax.ShapeDtypeStruct((T, D), F32))
        args += [pre_ln[0].reshape(1, D), pre_ln[1].reshape(1, D)]
    return pl.pallas_call(
        functools.partial(_in_proj_kernel, pre_ln=pre_ln is not None),
        grid=(T // tm,),
        in_specs=in_specs,
        out_specs=out_specs,
        out_shape=out_shape,
        compiler_params=_params(("parallel",)),
        name="in_proj",
    )(*args)


def _hgrn_chunk(z, ql, v, gl, lb, ng, state_t, b_scr, g_scr, consts):
    C = HG_CHUNK
    tri, levels, diag_masks, lane_c = consts
    u = jnp.exp(-jnp.abs(z))
    log1pu = jnp.log(1.0 + u)
    log_f = (jnp.minimum(z, 0.0) - log1pu) + jnp.log(1.0 + lb * jnp.exp(jnp.minimum(-z, EXP_CLAMP)))
    log2k = (jnp.minimum(-z, 0.0) - log1pu) * LOG2E + jnp.log2(1.0 - lb)
    k = jnp.exp2(log2k)
    q = ql * jax.nn.sigmoid(ql)
    b2 = _cumsum_rows(tri, log_f) * LOG2E
    b_scr[...] = b2
    g_scr[...] = log2k - b2
    vb = v.astype(BF16)
    b_last = b_scr[C - 1:C, :]

    o = _dot_nt((q * jnp.exp2(b2)).astype(BF16), state_t.astype(BF16))

    scores = jnp.zeros((C, C), F32)
    for m, is_query, pair in levels:
        pieces = [jnp.broadcast_to(b_scr[p * 2 * m + m - 1:p * 2 * m + m, :], (2 * m, LANES))
                  for p in range(C // (2 * m))]
        b_ref_rows = pieces[0] if len(pieces) == 1 else jnp.concatenate(pieces, axis=0)
        decay = jnp.exp2(-jnp.abs(b2 - b_ref_rows))
        qd = jnp.where(is_query, q * decay, 0.0).astype(BF16)
        kd = jnp.where(is_query, 0.0, k * decay).astype(BF16)
        scores = scores + jnp.where(pair, _dot_nt(qd, kd), 0.0)

    diag = []
    for blk in range(C // HG_SUB):
        lo = blk * HG_SUB
        bb = b2[lo:lo + HG_SUB]
        qq = q[lo:lo + HG_SUB]
        blk_scores = jnp.zeros((HG_SUB, C), F32)
        for s in range(HG_SUB):
            w = qq * jnp.exp2(bb + g_scr[lo + s:lo + s + 1, :])
            blk_scores = jnp.where(lane_c == lo + s, jnp.sum(w, axis=-1, keepdims=True), blk_scores)
        diag.append(jnp.where(diag_masks[blk], blk_scores, 0.0))
    scores = scores + jnp.concatenate(diag, axis=0)
    o = o + _dot(scores.astype(BF16), vb)

    k_dec = (k * jnp.exp2(b_last - b2)).astype(BF16)
    state_t = state_t * jnp.exp2(b_last) + _dot(v.T.astype(BF16), k_dec)

    ms = jnp.mean(o * o, axis=-1, keepdims=True)
    y = o * lax.rsqrt(ms + RMS_EPS) * ng * (gl * jax.nn.sigmoid(gl))
    return y, state_t


def _hgrn_kernel(q_ref, f_ref, i_ref, g_ref, lb_ref, ng_ref, o_ref, b_scr, g_scr, *, nchunks):
    C = HG_CHUNK
    ng = ng_ref[...]
    ri = lax.broadcasted_iota(jnp.int32, (C, C), 0)
    ci = lax.broadcasted_iota(jnp.int32, (C, C), 1)
    tri = jnp.where(ri >= ci, 1.0, 0.0).astype(BF16)
    rows = lax.broadcasted_iota(jnp.int32, (C, LANES), 0)
    sub_rows = lax.broadcasted_iota(jnp.int32, (HG_SUB, C), 0)
    lane_c = lax.broadcasted_iota(jnp.int32, (HG_SUB, C), 1)
    diag_masks = [sub_rows + blk * HG_SUB >= lane_c for blk in range(C // HG_SUB)]
    levels = []
    m = HG_SUB
    while m < C:
        shift = int(math.log2(2 * m))
        is_query = (rows & (2 * m - 1)) >= m
        pair = ((ri >> shift) == (ci >> shift)) & ((ri & (2 * m - 1)) >= m) & ((ci & (2 * m - 1)) < m)
        levels.append((m, is_query, pair))
        m *= 2
    consts = (tri, levels, diag_masks, lane_c)

    nb = q_ref.shape[0]

    def body(n, states):
        r0 = pl.multiple_of(n * C, C)
        new_states = []
        for bi in range(nb):
            for h in range(HG_HEADS):
                cols = slice(h * LANES, (h + 1) * LANES)
                y, st = _hgrn_chunk(f_ref[bi, pl.ds(r0, C), cols].astype(F32), q_ref[bi, pl.ds(r0, C), cols].astype(F32),
                                    i_ref[bi, pl.ds(r0, C), cols].astype(F32), g_ref[bi, pl.ds(r0, C), cols].astype(F32),
                                    lb_ref[:, cols], ng, states[bi * HG_HEADS + h],
                                    b_scr.at[bi * HG_HEADS + h], g_scr.at[bi * HG_HEADS + h], consts)
                o_ref[bi, pl.ds(r0, C), cols] = y.astype(o_ref.dtype)
                new_states.append(st)
        return tuple(new_states)

    lax.fori_loop(0, nchunks, body, tuple(jnp.zeros((HG_D, HG_D), F32) for _ in range(nb * HG_HEADS)))


def _hgrn_call(proj, lb, norm_g, B, S):
    W = HG_HEADS * HG_D
    nb = HG_BATCH if B % HG_BATCH == 0 else 1
    proj3 = proj.reshape(B, S, N_MAIN)
    blk = lambda off: pl.BlockSpec((nb, S, W), lambda b, off=off: (b, 0, off))
    out = pl.pallas_call(
        functools.partial(_hgrn_kernel, nchunks=S // HG_CHUNK),
        grid=(B // nb,),
        in_specs=[blk(COL_HQ // 4), blk(COL_HF // 4), blk(COL_HI // 4), blk(COL_HG // 4),
                  pl.BlockSpec((1, W), lambda b: (0, 0)),
                  pl.BlockSpec((1, LANES), lambda b: (0, 0))],
        out_specs=pl.BlockSpec((nb, S, W), lambda b: (b, 0, 0)),
        out_shape=jax.ShapeDtypeStruct((B, S, BRANCH_WIDTH), BF16),
        scratch_shapes=[pltpu.VMEM((nb * HG_HEADS, HG_CHUNK, LANES), F32)] * 2,
        compiler_params=_params(("parallel",)),
        name="hgrn",
    )(proj3, proj3, proj3, proj3, lb.reshape(1, W), norm_g.reshape(1, HG_D))
    return out.reshape(B * S, BRANCH_WIDTH)


def _fox_aug_tables():
    sel_q = np.zeros((3, LANES, BRANCH_WIDTH), np.float32)
    sel_k = np.zeros((3, LANES, BRANCH_WIDTH), np.float32)
    ones_q = np.zeros((1, BRANCH_WIDTH), np.float32)
    ones_k = np.zeros((1, BRANCH_WIDTH), np.float32)
    for col in range(BRANCH_WIDTH):
        pair, within = divmod(col, LANES)
        half, slot = divmod(within, FOX_DH)
        head = 2 * pair + 1 - half
        if slot < 3:
            sel_q[slot, head, col] = 1.0
            ones_k[0, col] = 1.0
        elif slot < 6:
            sel_k[slot - 3, head, col] = -1.0
            ones_q[0, col] = 1.0
    return sel_q, sel_k, ones_q, ones_k


def _fox_gate_kernel(ff_ref, bias_ref, selq_ref, selk_ref, oq_ref, ok_ref, qa_ref, ka_ref, *, nchunks):
    CH = GATE_CHUNK
    ri = lax.broadcasted_iota(jnp.int32, (CH, CH), 0)
    ci = lax.broadcasted_iota(jnp.int32, (CH, CH), 1)
    tri = jnp.where(ri >= ci, 1.0, 0.0).astype(BF16)
    bias = bias_ref[...]

    def body(n, carry):
        r0 = pl.multiple_of(n * CH, CH)
        c = carry + _cumsum_rows(tri, _log_sigmoid(ff_ref[pl.ds(r0, CH), :] + bias))
        parts = _split3(c * LOG2E)
        qa = oq_ref[...] + _dot(parts[0], selq_ref[0]) + _dot(parts[1], selq_ref[1]) + _dot(parts[2], selq_ref[2])
        ka = ok_ref[...] + _dot(parts[0], selk_ref[0]) + _dot(parts[1], selk_ref[1]) + _dot(parts[2], selk_ref[2])
        qa_ref[pl.ds(r0, CH), :] = qa.astype(BF16)
        ka_ref[pl.ds(r0, CH), :] = ka.astype(BF16)
        return c[CH - 1:CH, :]

    lax.fori_loop(0, nchunks, body, jnp.zeros((1, LANES), F32))


def _fox_gate_call(ff, bias, B, S):
    T = B * S
    bias_row = jnp.zeros((1, LANES), F32).at[0, :FOX_HEADS].set(bias.astype(F32))
    sel_q, sel_k, ones_q, ones_k = _fox_aug_tables()
    const2 = lambda b: (0, 0)
    const3 = lambda b: (0, 0, 0)
    return pl.pallas_call(
        functools.partial(_fox_gate_kernel, nchunks=S // GATE_CHUNK),
        grid=(B,),
        in_specs=[pl.BlockSpec((S, LANES), lambda b: (b, 0)),
                  pl.BlockSpec((1, LANES), const2),
                  pl.BlockSpec((3, LANES, BRANCH_WIDTH), const3),
                  pl.BlockSpec((3, LANES, BRANCH_WIDTH), const3),
                  pl.BlockSpec((1, BRANCH_WIDTH), const2),
                  pl.BlockSpec((1, BRANCH_WIDTH), const2)],
        out_specs=[pl.BlockSpec((S, BRANCH_WIDTH), lambda b: (b, 0)),
                   pl.BlockSpec((S, BRANCH_WIDTH), lambda b: (b, 0))],
        out_shape=[jax.ShapeDtypeStruct((T, BRANCH_WIDTH), BF16),
                   jax.ShapeDtypeStruct((T, BRANCH_WIDTH), BF16)],
        compiler_params=_params(("parallel",)),
        name="fox_gate",
    )(ff, bias_row, jnp.asarray(sel_q, BF16), jnp.asarray(sel_k, BF16), jnp.asarray(ones_q), jnp.asarray(ones_k))


def _flash_t(qs, k_fns, vt_fns, n_full, s_a, s_b):
    tq = tk = ATTN_BLOCK
    nmap = len(qs)

    def produce(dst, j):
        for i in range(nmap):
            dst[i] = _dot_nt(k_fns[i](j * tk, tk), qs[i])

    def update(s, m, l, acc, vt):
        m_new = jnp.maximum(m, jnp.max(s, axis=0, keepdims=True))
        alpha = jnp.exp2(m - m_new)
        p = jnp.exp2(s - m_new)
        return m_new, alpha * l + jnp.sum(p, axis=0, keepdims=True), alpha * acc + _dot(vt, p.astype(BF16))

    def consume(src, stats, j):
        return tuple(update(src[i], *stats[i], vt_fns[i](j * tk, tk)) for i in range(nmap))

    h = tk // 2
    mask_a = lax.broadcasted_iota(jnp.int32, (h, tq), 0) <= lax.broadcasted_iota(jnp.int32, (h, tq), 1)
    mask_b = lax.broadcasted_iota(jnp.int32, (h, h), 0) <= lax.broadcasted_iota(jnp.int32, (h, h), 1)

    def produce_diag(dst, j):
        for i in range(nmap):
            dst[i, 0:h, :] = _dot_nt(k_fns[i](j * tk, h), qs[i])
            dst[i, h:tk, h:tq] = _dot_nt(k_fns[i](j * tk + h, h), qs[i][h:tq])

    def consume_diag(src, stats, j):
        out = []
        for i in range(nmap):
            m, l, acc = update(jnp.where(mask_a, src[i, 0:h, :], NEG_INF), *stats[i], vt_fns[i](j * tk, h))
            m_b, l_b, acc_b = update(jnp.where(mask_b, src[i, h:tk, h:tq], NEG_INF), m[:, h:], l[:, h:], acc[:, h:],
                                     vt_fns[i](j * tk + h, h))
            out.append((jnp.concatenate([m[:, :h], m_b], axis=1), jnp.concatenate([l[:, :h], l_b], axis=1),
                        jnp.concatenate([acc[:, :h], acc_b], axis=1)))
        return tuple(out)

    stats = []
    for vt_fn in vt_fns:
        dv = vt_fn(0, tk).shape[0]
        stats.append((jnp.full((1, tq), NEG_INF, F32), jnp.zeros((1, tq), F32), jnp.zeros((dv, tq), F32)))

    def branch(n):
        def run(stats):
            bufs = (s_a, s_b)
            (produce if n > 0 else produce_diag)(bufs[0], 0)
            for j in range(n):
                (produce if j + 1 < n else produce_diag)(bufs[(j + 1) % 2], j + 1)
                stats = consume(bufs[j % 2], stats, j)
            return consume_diag(bufs[n % 2], stats, n)
        return run

    stats = branch(n_full)(tuple(stats))
    return [(acc, l) for (_, l, acc) in stats]


def _fill_vt(v_ref, vt_scr):
    S = v_ref.shape[0]
    for c in range(S // ATTN_BLOCK):
        sl = slice(c * ATTN_BLOCK, (c + 1) * ATTN_BLOCK)
        vt_scr[:, sl] = v_ref[sl, :].astype(F32).T.astype(BF16)


def _fox_kernel(q_ref, qa_ref, k_ref, ka_ref, v_ref, o_ref, vt_scr, s_a, s_b):
    tq = tk = ATTN_BLOCK
    _fill_vt(v_ref, vt_scr)
    lo = lax.broadcasted_iota(jnp.int32, (1, LANES), 1) < FOX_DH
    k_fns = [lambda k0, n: jnp.where(lo, k_ref[pl.ds(k0, n), :], ka_ref[pl.ds(k0, n), :]),
             lambda k0, n: jnp.where(lo, ka_ref[pl.ds(k0, n), :], k_ref[pl.ds(k0, n), :])]
    vt_fns = [lambda k0, n: vt_scr[0:FOX_DH, pl.ds(k0, n)],
              lambda k0, n: vt_scr[FOX_DH:2 * FOX_DH, pl.ds(k0, n)]]
    for qi in range(q_ref.shape[0] // tq):
        rows = slice(qi * tq, (qi + 1) * tq)
        qf = q_ref[rows, :].astype(F32) * (FOX_DH ** -0.5 * LOG2E)
        qa = qa_ref[rows, :].astype(F32)
        qs = [jnp.where(lo, qf, qa).astype(BF16), jnp.where(lo, qa, qf).astype(BF16)]
        (acc0, l0), (acc1, l1) = _flash_t(qs, k_fns, vt_fns, qi, s_a, s_b)
        o_t = jnp.concatenate([acc0 / l0, acc1 / l1], axis=0)
        o_ref[rows, :] = o_t.T.astype(o_ref.dtype)


def _fox_call(proj, qa, ka, B, S):
    T = B * S
    tq = ATTN_BLOCK
    seq = lambda off: pl.BlockSpec((S, LANES), lambda b, p, off=off: (b, off + p))
    return pl.pallas_call(
        _fox_kernel,
        grid=(B, FOX_HEADS // 2),
        in_specs=[seq(COL_FQ), seq(0), seq(COL_FK), seq(0), seq(COL_FV)],
        out_specs=seq(0),
        out_shape=jax.ShapeDtypeStruct((T, BRANCH_WIDTH), BF16),
        scratch_shapes=[pltpu.VMEM((LANES, S), BF16)] + [pltpu.VMEM((2, tq, tq), F32)] * 2,
        compiler_params=_params(("parallel", "parallel")),
        name="fox_attn",
    )(proj, qa, proj, ka, proj)


def _diff_kernel(q_ref, k_ref, v_ref, lam_ref, ng_ref, o_ref, vt_scr, s_a, s_b):
    tq = tk = ATTN_BLOCK
    _fill_vt(v_ref, vt_scr)
    lo = lax.broadcasted_iota(jnp.int32, (1, LANES), 1) < DIFF_DH
    k_fns = [lambda k0, n: k_ref[pl.ds(k0, n), :]] * 2
    vt_fns = [lambda k0, n: vt_scr[:, pl.ds(k0, n)]] * 2
    for qi in range(q_ref.shape[0] // tq):
        rows = slice(qi * tq, (qi + 1) * tq)
        qf = q_ref[rows, :].astype(F32) * (DIFF_DH ** -0.5 * LOG2E)
        qs = [jnp.where(lo, qf, 0.0).astype(BF16), jnp.where(lo, 0.0, qf).astype(BF16)]
        (acc0, l0), (acc1, l1) = _flash_t(qs, k_fns, vt_fns, qi, s_a, s_b)
        o = (acc0 / l0 - lam_ref[0:1, 0:1] * (acc1 / l1)).T
        ms = jnp.mean(o * o, axis=-1, keepdims=True)
        o_ref[rows, :] = (o * lax.rsqrt(ms + RMS_EPS) * ng_ref[...]).astype(o_ref.dtype)


def _diff_call(proj, lam_row, norm_row, B, S):
    T = B * S
    tq = ATTN_BLOCK
    seq = lambda off: pl.BlockSpec((S, LANES), lambda b, h, off=off: (b, off + h))
    const = pl.BlockSpec((1, LANES), lambda b, h: (0, 0))
    return pl.pallas_call(
        _diff_kernel,
        grid=(B, DIFF_HEADS),
        in_specs=[seq(COL_DQ), seq(COL_DK), seq(COL_DV), const, const],
        out_specs=seq(0),
        out_shape=jax.ShapeDtypeStruct((T, BRANCH_WIDTH), BF16),
        scratch_shapes=[pltpu.VMEM((LANES, S), BF16)] + [pltpu.VMEM((2, tq, tq), F32)] * 2,
        compiler_params=_params(("parallel", "parallel")),
        name="diff_attn",
    )(proj, proj, proj, lam_row, norm_row)


def _merge_kernel(x_ref, ya_ref, yb_ref, yc_ref, wg_ref, wb_ref, wo_ref, g1_ref, b1_ref, wr_ref, br_ref,
                  x1_ref, ids_ref, wts_ref, *, alpha):
    x = x_ref[...]
    xb = x.astype(BF16)
    merged = None
    for r, y_ref in enumerate((ya_ref, yb_ref, yc_ref)):
        gate = jax.nn.sigmoid(_dot(xb, wg_ref[:, r * D_MODEL:(r + 1) * D_MODEL]))
        term = gate * _dot(y_ref[...], wb_ref[r])
        merged = term if merged is None else merged + term
    h = _dot(merged.astype(BF16), wo_ref[...])
    x1 = _layer_norm(alpha * x + h, g1_ref[...], b1_ref[...])
    x1_ref[...] = x1

    logits = _dot(x1.astype(BF16), wr_ref[...]) + br_ref[...]
    lane = lax.broadcasted_iota(jnp.int32, logits.shape, 1)
    lane_f = lane.astype(F32)
    is_group = lane < N_GROUPS
    gl = jnp.where(is_group, logits, NEG_INF)
    gmax = jnp.max(gl, axis=-1, keepdims=True)
    gsum = jnp.sum(jnp.where(is_group, jnp.exp(gl - gmax), 0.0), axis=-1, keepdims=True)
    g_p = 1.0 / gsum
    g_idx = jnp.min(jnp.where(gl == gmax, lane_f, float(LANES)), axis=-1, keepdims=True)
    lo = N_GROUPS + EXPERTS_PER_GROUP * g_idx
    in_group = (lane_f >= lo) & (lane_f < lo + EXPERTS_PER_GROUP)
    el = jnp.where(in_group, logits, NEG_INF)
    v1 = jnp.max(el, axis=-1, keepdims=True)
    i1 = jnp.min(jnp.where(el == v1, lane_f, float(LANES)), axis=-1, keepdims=True)
    el2 = jnp.where(lane_f == i1, NEG_INF, el)
    v2 = jnp.max(el2, axis=-1, keepdims=True)
    i2 = jnp.min(jnp.where(el2 == v2, lane_f, float(LANES)), axis=-1, keepdims=True)
    t = jnp.exp(v2 - v1)
    w1 = g_p / (1.0 + t)
    w2 = g_p * t / (1.0 + t)
    ids = jnp.where(lane == 0, i1 - N_GROUPS, jnp.where(lane == 1, i2 - N_GROUPS, 0.0))
    ids_ref[...] = ids.astype(jnp.int32)
    wts_ref[...] = jnp.where(lane == 0, w1, jnp.where(lane == 1, w2, 0.0))


def _merge_call(x, ya, yb, yc, w_gates, w_branch, w_out, ln_g, ln_b, w_router, b_router, alpha, tm):
    T, D = x.shape
    row = lambda i: (i, 0)
    const2 = lambda i: (0, 0)
    return pl.pallas_call(
        functools.partial(_merge_kernel, alpha=alpha),
        grid=(T // tm,),
        in_specs=[pl.BlockSpec((tm, D), row),
                  pl.BlockSpec((tm, BRANCH_WIDTH), row),
                  pl.BlockSpec((tm, BRANCH_WIDTH), row),
                  pl.BlockSpec((tm, BRANCH_WIDTH), row),
                  pl.BlockSpec((D, N_BRANCHES * D), const2),
                  pl.BlockSpec((N_BRANCHES, BRANCH_WIDTH, D), lambda i: (0, 0, 0)),
                  pl.BlockSpec((D, D), const2),
                  pl.BlockSpec((1, D), const2),
                  pl.BlockSpec((1, D), const2),
                  pl.BlockSpec((D, LANES), const2),
                  pl.BlockSpec((1, LANES), const2)],
        out_specs=[pl.BlockSpec((tm, D), row),
                   pl.BlockSpec((tm, LANES), row),
                   pl.BlockSpec((tm, LANES), row)],
        out_shape=[jax.ShapeDtypeStruct((T, D), F32),
                   jax.ShapeDtypeStruct((T, LANES), jnp.int32),
                   jax.ShapeDtypeStruct((T, LANES), F32)],
        compiler_params=_params(("parallel",)),
        name="merge_ln1_router",
    )(x, ya, yb, yc, w_gates, w_branch, w_out, ln_g.reshape(1, D), ln_b.reshape(1, D), w_router, b_router)


def _row_copy(src, src_row, dst, dst_row, sem):
    return pltpu.make_async_copy(src.at[pl.ds(src_row, 1), :], dst.at[pl.ds(dst_row, 1), :], sem)


def _dispatch_kernel(pos_ref, pad_ref, x_ref, xs_hbm, zero_scr, sem, *, tm, npad, pad_steps):
    def issue(r, carry):
        _row_copy(x_ref, r, xs_hbm, pos_ref[2 * r], sem).start()
        _row_copy(x_ref, r, xs_hbm, pos_ref[2 * r + 1], sem).start()
        return carry

    lax.fori_loop(0, tm, issue, 0, unroll=8)

    @pl.when(pl.program_id(0) < pad_steps)
    def _():
        zero_scr[...] = jnp.zeros_like(zero_scr)

        def issue_pad(r, carry):
            _row_copy(zero_scr, r, xs_hbm, pad_ref[r], sem).start()
            return carry

        lax.fori_loop(0, npad, issue_pad, 0, unroll=8)
        pltpu.make_async_copy(zero_scr, xs_hbm.at[pl.ds(0, npad), :], sem).wait()

    for _ in range(2):
        pltpu.make_async_copy(x_ref, xs_hbm.at[pl.ds(0, tm), :], sem).wait()


def _dispatch_call(pos, pad_rows, x1, n_rows, tm):
    T, D = x1.shape
    nsteps = T // tm
    npad = max(LANES, pad_rows.shape[0] // nsteps)
    pad_steps = pad_rows.shape[0] // npad
    return pl.pallas_call(
        functools.partial(_dispatch_kernel, tm=tm, npad=npad, pad_steps=pad_steps),
        grid=(nsteps,),
        in_specs=[pl.BlockSpec((2 * tm,), lambda i: (i,), memory_space=pltpu.SMEM),
                  pl.BlockSpec((npad,), lambda i: (jnp.minimum(i, pad_steps - 1),), memory_space=pltpu.SMEM),
                  pl.BlockSpec((tm, D), lambda i: (i, 0))],
        out_specs=pl.BlockSpec(memory_space=pl.ANY),
        out_shape=jax.ShapeDtypeStruct((n_rows, D), F32),
        scratch_shapes=[pltpu.VMEM((npad, D), F32), pltpu.SemaphoreType.DMA(())],
        compiler_params=_params(("arbitrary",)),
        name="moe_dispatch",
    )(pos, pad_rows, x1)


def _expert_kernel(te_ref, nu_ref, xs_ref, wg_ref, wu_ref, wd_ref, o_ref, wgu_scr, wd_scr):
    t = pl.program_id(0)
    used = t < nu_ref[0]
    new_expert = (t == 0) | (te_ref[t] != te_ref[jnp.maximum(t - 1, 0)])

    @pl.when(used & new_expert)
    def _():
        wgu_scr[:, 0:D_EXPERT] = wg_ref[0, 0].astype(BF16)
        wgu_scr[:, D_EXPERT:2 * D_EXPERT] = wu_ref[0, 0].astype(BF16)
        wd_scr[...] = wd_ref[0, 0].astype(BF16)

    @pl.when(used)
    def _():
        gu = _dot(xs_ref[...].astype(BF16), wgu_scr[...])
        g = gu[:, 0:D_EXPERT]
        h = (g * jax.nn.sigmoid(g)) * gu[:, D_EXPERT:2 * D_EXPERT]
        o_ref[...] = _dot(h.astype(BF16), wd_scr[...])

    @pl.when(jnp.logical_not(used))
    def _():
        o_ref[...] = jnp.zeros_like(o_ref)


def _expert_call(tile_expert, n_used, xs, n_tiles, w_gate, w_up, w_down, layer):
    D = xs.shape[1]
    tm = EXPERT_TILE
    used = lambda t, te, nu: (jnp.minimum(t, nu[0] - 1), 0)
    expert = lambda t, te, nu: (layer, te[t], 0, 0)
    grid_spec = pltpu.PrefetchScalarGridSpec(
        num_scalar_prefetch=2,
        grid=(n_tiles,),
        in_specs=[pl.BlockSpec((tm, D), used),
                  pl.BlockSpec((1, 1, D, D_EXPERT), expert),
                  pl.BlockSpec((1, 1, D, D_EXPERT), expert),
                  pl.BlockSpec((1, 1, D_EXPERT, D), expert)],
        out_specs=pl.BlockSpec((tm, D), lambda t, te, nu: (t, 0)),
        scratch_shapes=[pltpu.VMEM((D, 2 * D_EXPERT), BF16), pltpu.VMEM((D_EXPERT, D), BF16)],
    )
    return pl.pallas_call(
        _expert_kernel,
        grid_spec=grid_spec,
        out_shape=jax.ShapeDtypeStruct((n_tiles * tm, D), F32),
        compiler_params=_params(("arbitrary",)),
        name="experts",
    )(tile_expert, n_used, xs, w_gate, w_up, w_down)


def _combine_kernel(pos_ref, x_ref, w_ref, g_ref, b_ref, y_hbm, o_ref, ybuf, sem, *, alpha, tm):
    def issue(r, carry):
        _row_copy(y_hbm, pos_ref[2 * r], ybuf.at[0], r, sem).start()
        _row_copy(y_hbm, pos_ref[2 * r + 1], ybuf.at[1], r, sem).start()
        return carry

    lax.fori_loop(0, tm, issue, 0, unroll=8)
    for k in range(2):
        pltpu.make_async_copy(y_hbm.at[pl.ds(0, tm), :], ybuf.at[k], sem).wait()
    w = w_ref[...]
    u = alpha * x_ref[...] + w[:, 0:1] * ybuf[0] + w[:, 1:2] * ybuf[1]
    o_ref[...] = _layer_norm(u, g_ref[...], b_ref[...])


def _combine_call(pos, x1, wts, ln_g, ln_b, y, alpha, tm):
    T, D = x1.shape
    row = lambda i: (i, 0)
    const2 = lambda i: (0, 0)
    return pl.pallas_call(
        functools.partial(_combine_kernel, alpha=alpha, tm=tm),
        grid=(T // tm,),
        in_specs=[pl.BlockSpec((2 * tm,), lambda i: (i,), memory_space=pltpu.SMEM),
                  pl.BlockSpec((tm, D), row),
                  pl.BlockSpec((tm, LANES), row),
                  pl.BlockSpec((1, D), const2), pl.BlockSpec((1, D), const2),
                  pl.BlockSpec(memory_space=pl.ANY)],
        out_specs=pl.BlockSpec((tm, D), row),
        out_shape=jax.ShapeDtypeStruct((T, D), F32),
        scratch_shapes=[pltpu.VMEM((2, tm, D), F32), pltpu.SemaphoreType.DMA(())],
        compiler_params=_params(("arbitrary",)),
        name="combine_ln2",
    )(pos, x1, wts, ln_g.reshape(1, D), ln_b.reshape(1, D), y)


def _rope_tables(positions):
    half = ROPE_DIM // 2
    inv_freq = ROPE_THETA ** (-jnp.arange(0, ROPE_DIM, 2, dtype=F32) / ROPE_DIM)
    ang = positions.astype(F32).reshape(-1, 1) * inv_freq[None, :]
    cos, sin = jnp.cos(ang), jnp.sin(ang)
    lane = jnp.arange(LANES)
    in_head = lane % DIFF_DH
    freq = in_head % half
    first = in_head < half
    second = (in_head >= half) & (in_head < ROPE_DIM)
    cosf = jnp.where((first | second)[None, :], cos[:, freq], 1.0)
    sin_a = jnp.where(second[None, :], sin[:, freq], 0.0)
    sin_b = jnp.where(first[None, :], -sin[:, freq], 0.0)
    return cosf, sin_a, sin_b


def _dispatch_plan(ids, T):
    tm = EXPERT_TILE
    flat = ids.reshape(-1)
    onehot = (flat[:, None] == jnp.arange(N_EXPERTS, dtype=jnp.int32)[None, :]).astype(jnp.int32)
    csum = jnp.cumsum(onehot, axis=0)
    counts = csum[-1]
    rank = jnp.take_along_axis(csum, flat[:, None], axis=1)[:, 0] - 1
    padded = ((counts + tm - 1) // tm) * tm
    ends = jnp.cumsum(padded)
    starts = ends - padded
    pos = (starts[flat] + rank).astype(jnp.int32)
    n_slab = 2 * T + N_EXPERTS * tm
    n_tiles = n_slab // tm
    tile_start = jnp.arange(n_tiles, dtype=jnp.int32) * tm
    tile_expert = jnp.minimum(jnp.sum((tile_start[:, None] >= ends[None, :]).astype(jnp.int32), axis=1),
                              N_EXPERTS - 1).astype(jnp.int32)
    n_used = (ends[-1] // tm).astype(jnp.int32).reshape(1)
    gap = padded - counts
    gap_end = jnp.cumsum(gap)
    gap_start = gap_end - gap
    j = jnp.arange(N_EXPERTS * tm, dtype=jnp.int32)
    e = jnp.minimum(jnp.sum((j[:, None] >= gap_end[None, :]).astype(jnp.int32), axis=1), N_EXPERTS - 1)
    in_tile = starts[e] + counts[e] + (j - gap_start[e])
    pad_rows = jnp.where(j < gap_end[-1], in_tile, ends[-1] + (j - gap_end[-1]))
    return pos, tile_expert, n_used, pad_rows.astype(jnp.int32), n_tiles


def _layer(x, cosf, sina, sinb, B, S, lb, p, alpha, lam_init, pre_ln=None):
    T = B * S
    tm = min(512, T)
    w_in = p["w_in"]
    w_main = jnp.concatenate([w_in[:, :3584], w_in[:, 3592:5128]], axis=1).astype(BF16)
    w_ff = jnp.zeros((D_MODEL, LANES), BF16).at[:, :FOX_HEADS].set(w_in[:, 3584:3592].astype(BF16))
    w_gates = w_in[:, 5128:].astype(BF16)

    if pre_ln is None:
        proj, ff = _in_proj_call(x, w_main, w_ff, cosf, sina, sinb, tm)
    else:
        proj, ff, x = _in_proj_call(x, w_main, w_ff, cosf, sina, sinb, tm, pre_ln)
    ya = _hgrn_call(proj, lb, p["hgrn_norm_g"], B, S)
    qa, ka = _fox_gate_call(ff, p["fox_f_bias"], B, S)
    yb = _fox_call(proj, qa, ka, B, S)
    lv = p["diff_lambda"].astype(F32)
    lam = jnp.exp(jnp.sum(lv[0] * lv[1])) - jnp.exp(jnp.sum(lv[2] * lv[3])) + lam_init
    lam_row = jnp.full((1, LANES), lam, F32)
    norm_row = (p["diff_norm_g"].astype(F32) * (1.0 - lam_init)).reshape(1, DIFF_DV)
    yc = _diff_call(proj, lam_row, norm_row, B, S)

    w_router = jnp.zeros((D_MODEL, LANES), F32)
    w_router = w_router.at[:, :N_GROUPS].set(p["router_g_w"]).at[:, N_GROUPS:N_GROUPS + N_EXPERTS].set(p["router_e_w"])
    b_router = jnp.zeros((1, LANES), F32)
    b_router = b_router.at[0, :N_GROUPS].set(p["router_g_b"]).at[0, N_GROUPS:N_GROUPS + N_EXPERTS].set(
        p["router_e_b"].reshape(-1))
    x1, ids, wts = _merge_call(x, ya, yb, yc, w_gates, p["w_branch"].astype(BF16), p["w_out"].astype(BF16),
                               p["ln1_g"], p["ln1_b"], w_router.astype(BF16), b_router, alpha, tm)

    pos, tile_expert, n_used, pad_rows, n_tiles = _dispatch_plan(ids[:, :2], T)
    tp = min(PERMUTE_TILE, T)
    xs = _dispatch_call(pos, pad_rows, x1, n_tiles * EXPERT_TILE, tp)
    y = _expert_call(tile_expert, n_used, xs, n_tiles, p["expert_w_gate"], p["expert_w_up"], p["expert_w_down"],
                     p["layer"])
    return _combine_call(pos, x1, wts, p["ln2_g"], p["ln2_b"], y, alpha, tp)


def kernel(x, positions, ln_in_g, ln_in_b, w_in, hgrn_lb_logits, hgrn_norm_g, fox_f_bias, diff_lambda,
           diff_norm_g, w_branch, w_out, ln1_g, ln1_b, router_g_w, router_g_b, router_e_w, router_e_b,
           expert_w_gate, expert_w_up, expert_w_down, ln2_g, ln2_b):
    B, S, D = x.shape
    T = B * S
    depth = w_in.shape[0]
    alpha = (2 * depth) ** 0.25
    cosf, sina, sinb = _rope_tables(positions)
    lb_soft = jax.nn.softmax(hgrn_lb_logits.astype(F32), axis=0)
    lower_bounds = jnp.maximum(jnp.cumsum(lb_soft, axis=0) - lb_soft[0], 0.0)

    h = x.reshape(T, D)
    for l in range(depth):
        p = dict(w_in=w_in[l], hgrn_norm_g=hgrn_norm_g[l], fox_f_bias=fox_f_bias[l], diff_lambda=diff_lambda[l],
                 diff_norm_g=diff_norm_g[l], w_branch=w_branch[l], w_out=w_out[l], ln1_g=ln1_g[l], ln1_b=ln1_b[l],
                 router_g_w=router_g_w[l], router_g_b=router_g_b[l], router_e_w=router_e_w[l],
                 router_e_b=router_e_b[l], expert_w_gate=expert_w_gate, expert_w_up=expert_w_up,
                 expert_w_down=expert_w_down, layer=l, ln2_g=ln2_g[l], ln2_b=ln2_b[l])
        lam_init = 0.8 - 0.6 * float(math.exp(-0.3 * l))
        h = _layer(h, cosf, sina, sinb, B, S, lower_bounds[l], p, alpha, lam_init,
                   pre_ln=(ln_in_g, ln_in_b) if l == 0 else None)
    return h.reshape(B, S, D)
```

```python
import functools
import math

import numpy as np
import jax
import jax.numpy as jnp
from jax import lax
from jax.experimental import pallas as pl
from jax.experimental.pallas import tpu as pltpu

F32 = jnp.float32
BF16 = jnp.bfloat16

D_MODEL = 1024
HG_HEADS, HG_D = 4, 128
FOX_HEADS, FOX_DH = 8, 64
DIFF_HEADS, DIFF_DH, DIFF_DV = 4, 64, 128
BRANCH_WIDTH = 512
N_BRANCHES = 3
ROPE_THETA = 500000.0
ROPE_DIM = DIFF_DH // 4
N_GROUPS, EXPERTS_PER_GROUP = 4, 8
N_EXPERTS = N_GROUPS * EXPERTS_PER_GROUP
D_EXPERT = 512
LN_EPS = 1e-5
RMS_EPS = 1e-6
NEG_INF = -1e30
EXP_CLAMP = 60.0
LOG2E = 1.4426950408889634

LANES = 128
N_MAIN = 5120
COL_HQ, COL_HF, COL_HI, COL_HG = 0, 4, 8, 12
COL_FQ, COL_FK, COL_FV = 16, 20, 24
COL_DQ, COL_DK, COL_DV = 28, 32, 36
COL_TILE = 512
ROW_TILE = 512
ROPE_TILES = (7, 8)
OFF_FF = 4 * HG_HEADS * HG_D + 3 * FOX_HEADS * FOX_DH
OFF_DQ = OFF_FF + FOX_HEADS
OFF_GATES = OFF_DQ + 3 * BRANCH_WIDTH

HG_CHUNK = 128
HG_SUB = 8
HG_BATCH = 2
GATE_CHUNK = 256
ATTN_BLOCK = 512
EXPERT_TILE = 512
PERMUTE_TILE = 1024
VMEM_LIMIT = 56 * 1024 * 1024


def _dot(a, b):
    return jnp.dot(a, b, preferred_element_type=F32)


def _dot_nt(a, b):
    return lax.dot_general(a, b, (((1,), (1,)), ((), ())), preferred_element_type=F32)


def _log_sigmoid(z):
    return jnp.minimum(z, 0.0) - jnp.log1p(jnp.exp(-jnp.abs(z)))


def _split3(x):
    h1 = x.astype(BF16)
    r1 = x - h1.astype(F32)
    h2 = r1.astype(BF16)
    h3 = (r1 - h2.astype(F32)).astype(BF16)
    return h1, h2, h3


def _cumsum_rows(tri, x):
    h1, h2, h3 = _split3(x)
    return _dot(tri, h1) + _dot(tri, h2) + _dot(tri, h3)


def _layer_norm(u, g, b):
    mu = jnp.mean(u, axis=-1, keepdims=True)
    d = u - mu
    var = jnp.mean(d * d, axis=-1, keepdims=True)
    return d * lax.rsqrt(var + LN_EPS) * g + b


def _params(sem):
    return pltpu.CompilerParams(dimension_semantics=sem, vmem_limit_bytes=VMEM_LIMIT)


def _in_proj_kernel(x_ref, w_ref, wff_ref, cos_ref, sa_ref, sb_ref, *rest, pre_ln):
    if pre_ln:
        g_ref, b_ref, o_ref, ff_ref, xn_ref = rest
        x = _layer_norm(x_ref[...], g_ref[...], b_ref[...])
        xn_ref[...] = x
    else:
        o_ref, ff_ref = rest
        x = x_ref[...]
    xb = x.astype(BF16)
    ff_ref[...] = _dot(xb, wff_ref[...])
    for j in range(N_MAIN // COL_TILE):
        acc = _dot(xb, w_ref[:, j * COL_TILE:(j + 1) * COL_TILE])
        if j in ROPE_TILES:
            cosf, sa, sb = cos_ref[...], sa_ref[...], sb_ref[...]
            for g in range(COL_TILE // LANES):
                t = acc[:, g * LANES:(g + 1) * LANES]
                r = t * cosf + pltpu.roll(t, 8, 1) * sa + pltpu.roll(t, LANES - 8, 1) * sb
                o_ref[:, j * COL_TILE + g * LANES:j * COL_TILE + (g + 1) * LANES] = r.astype(BF16)
        else:
            o_ref[:, j * COL_TILE:(j + 1) * COL_TILE] = acc.astype(BF16)


def _in_proj_call(x, w_main, w_ff, cosf, sina, sinb, tm, pre_ln=None):
    T, D = x.shape
    const = lambda i: (0, 0)
    row = lambda i: (i, 0)
    in_specs = [pl.BlockSpec((tm, D), row),
                pl.BlockSpec((D, N_MAIN), const),
                pl.BlockSpec((D, LANES), const),
                pl.BlockSpec((tm, LANES), row),
                pl.BlockSpec((tm, LANES), row),
                pl.BlockSpec((tm, LANES), row)]
    out_specs = [pl.BlockSpec((tm, N_MAIN), row), pl.BlockSpec((tm, LANES), row)]
    out_shape = [jax.ShapeDtypeStruct((T, N_MAIN), BF16), jax.ShapeDtypeStruct((T, LANES), F32)]
    args = [x, w_main, w_ff, cosf, sina, sinb]
    if pre_ln is not None:
        in_specs += [pl.BlockSpec((1, D), const)] * 2
        out_specs.append(pl.BlockSpec((tm, D), row))
        out_shape.append(jax.ShapeDtypeStruct((T, D), F32))
        args += [pre_ln[0].reshape(1, D), pre_ln[1].reshape(1, D)]
    return pl.pallas_call(
        functools.partial(_in_proj_kernel, pre_ln=pre_ln is not None),
        grid=(T // tm,),
        in_specs=in_specs,
        out_specs=out_specs,
        out_shape=out_shape,
        compiler_params=_params(("parallel",)),
        name="in_proj",
    )(*args)


def _hgrn_chunk(z, ql, v, gl, lb, ng, state_t, b_scr, g_scr, consts):
    C = HG_CHUNK
    tri, levels, diag_masks, lane_c = consts
    u = jnp.exp(-jnp.abs(z))
    log1pu = jnp.log(1.0 + u)
    log_f = (jnp.minimum(z, 0.0) - log1pu) + jnp.log(1.0 + lb * jnp.exp(jnp.minimum(-z, EXP_CLAMP)))
    log2k = (jnp.minimum(-z, 0.0) - log1pu) * LOG2E + jnp.log2(1.0 - lb)
    k = jnp.exp2(log2k)
    q = ql * jax.nn.sigmoid(ql)
    b2 = _cumsum_rows(tri, log_f) * LOG2E
    b_scr[...] = b2
    g_scr[...] = log2k - b2
    vb = v.astype(BF16)
    b_last = b_scr[C - 1:C, :]

    o = _dot_nt((q * jnp.exp2(b2)).astype(BF16), state_t.astype(BF16))

    scores = jnp.zeros((C, C), F32)
    for m, is_query, pair in levels:
        pieces = [jnp.broadcast_to(b_scr[p * 2 * m + m - 1:p * 2 * m + m, :], (2 * m, LANES))
                  for p in range(C // (2 * m))]
        b_ref_rows = pieces[0] if len(pieces) == 1 else jnp.concatenate(pieces, axis=0)
        decay = jnp.exp2(-jnp.abs(b2 - b_ref_rows))
        qd = jnp.where(is_query, q * decay, 0.0).astype(BF16)
        kd = jnp.where(is_query, 0.0, k * decay).astype(BF16)
        scores = scores + jnp.where(pair, _dot_nt(qd, kd), 0.0)

    diag = []
    for blk in range(C // HG_SUB):
        lo = blk * HG_SUB
        bb = b2[lo:lo + HG_SUB]
        qq = q[lo:lo + HG_SUB]
        blk_scores = jnp.zeros((HG_SUB, C), F32)
        for s in range(HG_SUB):
            w = qq * jnp.exp2(bb + g_scr[lo + s:lo + s + 1, :])
            blk_scores = jnp.where(lane_c == lo + s, jnp.sum(w, axis=-1, keepdims=True), blk_scores)
        diag.append(jnp.where(diag_masks[blk], blk_scores, 0.0))
    scores = scores + jnp.concatenate(diag, axis=0)
    o = o + _dot(scores.astype(BF16), vb)

    k_dec = (k * jnp.exp2(b_last - b2)).astype(BF16)
    state_t = state_t * jnp.exp2(b_last) + _dot(v.T.astype(BF16), k_dec)

    ms = jnp.mean(o * o, axis=-1, keepdims=True)
    y = o * lax.rsqrt(ms + RMS_EPS) * ng * (gl * jax.nn.sigmoid(gl))
    return y, state_t


def _hgrn_kernel(q_ref, f_ref, i_ref, g_ref, lb_ref, ng_ref, o_ref, b_scr, g_scr, *, nchunks):
    C = HG_CHUNK
    ng = ng_ref[...]
    ri = lax.broadcasted_iota(jnp.int32, (C, C), 0)
    ci = lax.broadcasted_iota(jnp.int32, (C, C), 1)
    tri = jnp.where(ri >= ci, 1.0, 0.0).astype(BF16)
    rows = lax.broadcasted_iota(jnp.int32, (C, LANES), 0)
    sub_rows = lax.broadcasted_iota(jnp.int32, (HG_SUB, C), 0)
    lane_c = lax.broadcasted_iota(jnp.int32, (HG_SUB, C), 1)
    diag_masks = [sub_rows + blk * HG_SUB >= lane_c for blk in range(C // HG_SUB)]
    levels = []
    m = HG_SUB
    while m < C:
        shift = int(math.log2(2 * m))
        is_query = (rows & (2 * m - 1)) >= m
        pair = ((ri >> shift) == (ci >> shift)) & ((ri & (2 * m - 1)) >= m) & ((ci & (2 * m - 1)) < m)
        levels.append((m, is_query, pair))
        m *= 2
    consts = (tri, levels, diag_masks, lane_c)

    nb = q_ref.shape[0]

    def body(n, states):
        r0 = pl.multiple_of(n * C, C)
        new_states = []
        for bi in range(nb):
            for h in range(HG_HEADS):
                cols = slice(h * LANES, (h + 1) * LANES)
                y, st = _hgrn_chunk(f_ref[bi, pl.ds(r0, C), cols].astype(F32), q_ref[bi, pl.ds(r0, C), cols].astype(F32),
                                    i_ref[bi, pl.ds(r0, C), cols].astype(F32), g_ref[bi, pl.ds(r0, C), cols].astype(F32),
                                    lb_ref[:, cols], ng, states[bi * HG_HEADS + h],
                                    b_scr.at[bi * HG_HEADS + h], g_scr.at[bi * HG_HEADS + h], consts)
                o_ref[bi, pl.ds(r0, C), cols] = y.astype(o_ref.dtype)
                new_states.append(st)
        return tuple(new_states)

    lax.fori_loop(0, nchunks, body, tuple(jnp.zeros((HG_D, HG_D), F32) for _ in range(nb * HG_HEADS)))


def _hgrn_call(proj, lb, norm_g, B, S):
    W = HG_HEADS * HG_D
    nb = HG_BATCH if B % HG_BATCH == 0 else 1
    proj3 = proj.reshape(B, S, N_MAIN)
    blk = lambda off: pl.BlockSpec((nb, S, W), lambda b, off=off: (b, 0, off))
    out = pl.pallas_call(
        functools.partial(_hgrn_kernel, nchunks=S // HG_CHUNK),
        grid=(B // nb,),
        in_specs=[blk(COL_HQ // 4), blk(COL_HF // 4), blk(COL_HI // 4), blk(COL_HG // 4),
                  pl.BlockSpec((1, W), lambda b: (0, 0)),
                  pl.BlockSpec((1, LANES), lambda b: (0, 0))],
        out_specs=pl.BlockSpec((nb, S, W), lambda b: (b, 0, 0)),
        out_shape=jax.ShapeDtypeStruct((B, S, BRANCH_WIDTH), BF16),
        scratch_shapes=[pltpu.VMEM((nb * HG_HEADS, HG_CHUNK, LANES), F32)] * 2,
        compiler_params=_params(("parallel",)),
        name="hgrn",
    )(proj3, proj3, proj3, proj3, lb.reshape(1, W), norm_g.reshape(1, HG_D))
    return out.reshape(B * S, BRANCH_WIDTH)


def _fox_aug_tables():
    W = BRANCH_WIDTH
    sel = np.zeros((3 * LANES, 2 * W), np.float32)
    ones = np.zeros((1, 2 * W), np.float32)
    for col in range(W):
        pair, within = divmod(col, LANES)
        half, slot = divmod(within, FOX_DH)
        head = 2 * pair + 1 - half
        if slot < 3:
            sel[slot * LANES + head, col] = 1.0
            ones[0, W + col] = 1.0
        elif slot < 6:
            sel[(slot - 3) * LANES + head, W + col] = -1.0
            ones[0, col] = 1.0
    return sel, ones


def _fox_gate_kernel(ff_ref, bias_ref, sel_ref, ones_ref, qa_ref, ka_ref, *, nchunks):
    CH = GATE_CHUNK
    ri = lax.broadcasted_iota(jnp.int32, (CH, CH), 0)
    ci = lax.broadcasted_iota(jnp.int32, (CH, CH), 1)
    tri = jnp.where(ri >= ci, 1.0, 0.0).astype(BF16)
    bias = bias_ref[...]

    def body(n, carry):
        r0 = pl.multiple_of(n * CH, CH)
        c = carry + _cumsum_rows(tri, _log_sigmoid(ff_ref[pl.ds(r0, CH), :] + bias))
        parts = jnp.concatenate(_split3(c * LOG2E), axis=1)
        aug = (ones_ref[...] + _dot(parts, sel_ref[...])).astype(BF16)
        qa_ref[pl.ds(r0, CH), :] = aug[:, 0:BRANCH_WIDTH]
        ka_ref[pl.ds(r0, CH), :] = aug[:, BRANCH_WIDTH:2 * BRANCH_WIDTH]
        return c[CH - 1:CH, :]

    lax.fori_loop(0, nchunks, body, jnp.zeros((1, LANES), F32))


def _fox_gate_call(ff, bias, B, S):
    T = B * S
    bias_row = jnp.zeros((1, LANES), F32).at[0, :FOX_HEADS].set(bias.astype(F32))
    sel, ones = _fox_aug_tables()
    const2 = lambda b: (0, 0)
    return pl.pallas_call(
        functools.partial(_fox_gate_kernel, nchunks=S // GATE_CHUNK),
        grid=(B,),
        in_specs=[pl.BlockSpec((S, LANES), lambda b: (b, 0)),
                  pl.BlockSpec((1, LANES), const2),
                  pl.BlockSpec((3 * LANES, 2 * BRANCH_WIDTH), const2),
                  pl.BlockSpec((1, 2 * BRANCH_WIDTH), const2)],
        out_specs=[pl.BlockSpec((S, BRANCH_WIDTH), lambda b: (b, 0)),
                   pl.BlockSpec((S, BRANCH_WIDTH), lambda b: (b, 0))],
        out_shape=[jax.ShapeDtypeStruct((T, BRANCH_WIDTH), BF16),
                   jax.ShapeDtypeStruct((T, BRANCH_WIDTH), BF16)],
        compiler_params=_params(("parallel",)),
        name="fox_gate",
    )(ff, bias_row, jnp.asarray(sel, BF16), jnp.asarray(ones))


def _flash_t(qs, k_fns, vt_fns, n_full, s_a, s_b):
    tq = tk = ATTN_BLOCK
    nmap = len(qs)

    def produce(dst, j):
        for i in range(nmap):
            dst[i] = _dot_nt(k_fns[i](j * tk, tk), qs[i])

    def update(s, m, l, acc, vt):
        m_new = jnp.maximum(m, jnp.max(s, axis=0, keepdims=True))
        alpha = jnp.exp2(m - m_new)
        p = jnp.exp2(s - m_new)
        return m_new, alpha * l + jnp.sum(p, axis=0, keepdims=True), alpha * acc + _dot(vt, p.astype(BF16))

    def consume(src, stats, j):
        return tuple(update(src[i], *stats[i], vt_fns[i](j * tk, tk)) for i in range(nmap))

    h = tk // 2
    mask_a = lax.broadcasted_iota(jnp.int32, (h, tq), 0) <= lax.broadcasted_iota(jnp.int32, (h, tq), 1)
    mask_b = lax.broadcasted_iota(jnp.int32, (h, h), 0) <= lax.broadcasted_iota(jnp.int32, (h, h), 1)

    def produce_diag(dst, j):
        for i in range(nmap):
            dst[i, 0:h, :] = _dot_nt(k_fns[i](j * tk, h), qs[i])
            dst[i, h:tk, h:tq] = _dot_nt(k_fns[i](j * tk + h, h), qs[i][h:tq])

    def consume_diag(src, stats, j):
        out = []
        for i in range(nmap):
            m, l, acc = update(jnp.where(mask_a, src[i, 0:h, :], NEG_INF), *stats[i], vt_fns[i](j * tk, h))
            m_b, l_b, acc_b = update(jnp.where(mask_b, src[i, h:tk, h:tq], NEG_INF), m[:, h:], l[:, h:], acc[:, h:],
                                     vt_fns[i](j * tk + h, h))
            out.append((jnp.concatenate([m[:, :h], m_b], axis=1), jnp.concatenate([l[:, :h], l_b], axis=1),
                        jnp.concatenate([acc[:, :h], acc_b], axis=1)))
        return tuple(out)

    stats = []
    for vt_fn in vt_fns:
        dv = vt_fn(0, tk).shape[0]
        stats.append((jnp.full((1, tq), NEG_INF, F32), jnp.zeros((1, tq), F32), jnp.zeros((dv, tq), F32)))

    def branch(n):
        def run(stats):
            bufs = (s_a, s_b)
            (produce if n > 0 else produce_diag)(bufs[0], 0)
            for j in range(n):
                (produce if j + 1 < n else produce_diag)(bufs[(j + 1) % 2], j + 1)
                stats = consume(bufs[j % 2], stats, j)
            return consume_diag(bufs[n % 2], stats, n)
        return run

    stats = branch(n_full)(tuple(stats))
    return [(acc, l) for (_, l, acc) in stats]


def _fill_vt(v_ref, vt_scr):
    S = v_ref.shape[0]
    for c in range(S // ATTN_BLOCK):
        sl = slice(c * ATTN_BLOCK, (c + 1) * ATTN_BLOCK)
        vt_scr[:, sl] = v_ref[sl, :].astype(F32).T.astype(BF16)


def _fox_kernel(q_ref, qa_ref, k_ref, ka_ref, v_ref, o_ref, vt_scr, s_a, s_b):
    tq = tk = ATTN_BLOCK
    _fill_vt(v_ref, vt_scr)
    lo = lax.broadcasted_iota(jnp.int32, (1, LANES), 1) < FOX_DH
    k_fns = [lambda k0, n: jnp.where(lo, k_ref[pl.ds(k0, n), :], ka_ref[pl.ds(k0, n), :]),
             lambda k0, n: jnp.where(lo, ka_ref[pl.ds(k0, n), :], k_ref[pl.ds(k0, n), :])]
    vt_fns = [lambda k0, n: vt_scr[0:FOX_DH, pl.ds(k0, n)],
              lambda k0, n: vt_scr[FOX_DH:2 * FOX_DH, pl.ds(k0, n)]]
    for qi in range(q_ref.shape[0] // tq):
        rows = slice(qi * tq, (qi + 1) * tq)
        qf = q_ref[rows, :].astype(F32) * (FOX_DH ** -0.5 * LOG2E)
        qa = qa_ref[rows, :].astype(F32)
        qs = [jnp.where(lo, qf, qa).astype(BF16), jnp.where(lo, qa, qf).astype(BF16)]
        (acc0, l0), (acc1, l1) = _flash_t(qs, k_fns, vt_fns, qi, s_a, s_b)
        o_t = jnp.concatenate([acc0 / l0, acc1 / l1], axis=0)
        o_ref[rows, :] = o_t.T.astype(o_ref.dtype)


def _fox_call(proj, qa, ka, B, S):
    T = B * S
    tq = ATTN_BLOCK
    seq = lambda off: pl.BlockSpec((S, LANES), lambda b, p, off=off: (b, off + p))
    return pl.pallas_call(
        _fox_kernel,
        grid=(B, FOX_HEADS // 2),
        in_specs=[seq(COL_FQ), seq(0), seq(COL_FK), seq(0), seq(COL_FV)],
        out_specs=seq(0),
        out_shape=jax.ShapeDtypeStruct((T, BRANCH_WIDTH), BF16),
        scratch_shapes=[pltpu.VMEM((LANES, S), BF16)] + [pltpu.VMEM((2, tq, tq), F32)] * 2,
        compiler_params=_params(("parallel", "parallel")),
        name="fox_attn",
    )(proj, qa, proj, ka, proj)


def _diff_kernel(q_ref, k_ref, v_ref, lam_ref, ng_ref, o_ref, vt_scr, s_a, s_b):
    tq = tk = ATTN_BLOCK
    _fill_vt(v_ref, vt_scr)
    lo = lax.broadcasted_iota(jnp.int32, (1, LANES), 1) < DIFF_DH
    k_fns = [lambda k0, n: k_ref[pl.ds(k0, n), :]] * 2
    vt_fns = [lambda k0, n: vt_scr[:, pl.ds(k0, n)]] * 2
    for qi in range(q_ref.shape[0] // tq):
        rows = slice(qi * tq, (qi + 1) * tq)
        qf = q_ref[rows, :].astype(F32) * (DIFF_DH ** -0.5 * LOG2E)
        qs = [jnp.where(lo, qf, 0.0).astype(BF16), jnp.where(lo, 0.0, qf).astype(BF16)]
        (acc0, l0), (acc1, l1) = _flash_t(qs, k_fns, vt_fns, qi, s_a, s_b)
        o = (acc0 / l0 - lam_ref[0:1, 0:1] * (acc1 / l1)).T
        ms = jnp.mean(o * o, axis=-1, keepdims=True)
        o_ref[rows, :] = (o * lax.rsqrt(ms + RMS_EPS) * ng_ref[...]).astype(o_ref.dtype)


def _diff_call(proj, lam_row, norm_row, B, S):
    T = B * S
    tq = ATTN_BLOCK
    seq = lambda off: pl.BlockSpec((S, LANES), lambda b, h, off=off: (b, off + h))
    const = pl.BlockSpec((1, LANES), lambda b, h: (0, 0))
    return pl.pallas_call(
        _diff_kernel,
        grid=(B, DIFF_HEADS),
        in_specs=[seq(COL_DQ), seq(COL_DK), seq(COL_DV), const, const],
        out_specs=seq(0),
        out_shape=jax.ShapeDtypeStruct((T, BRANCH_WIDTH), BF16),
        scratch_shapes=[pltpu.VMEM((LANES, S), BF16)] + [pltpu.VMEM((2, tq, tq), F32)] * 2,
        compiler_params=_params(("parallel", "parallel")),
        name="diff_attn",
    )(proj, proj, proj, lam_row, norm_row)


def _merge_kernel(x_ref, ya_ref, yb_ref, yc_ref, wg_ref, wb_ref, wo_ref, g1_ref, b1_ref, wr_ref, br_ref,
                  x1_ref, ids_ref, wts_ref, *, alpha):
    x = x_ref[...]
    xb = x.astype(BF16)
    merged = None
    for r, y_ref in enumerate((ya_ref, yb_ref, yc_ref)):
        gate = jax.nn.sigmoid(_dot(xb, wg_ref[:, r * D_MODEL:(r + 1) * D_MODEL]))
        term = gate * _dot(y_ref[...], wb_ref[r])
        merged = term if merged is None else merged + term
    h = _dot(merged.astype(BF16), wo_ref[...])
    x1 = _layer_norm(alpha * x + h, g1_ref[...], b1_ref[...])
    x1_ref[...] = x1

    logits = _dot(x1.astype(BF16), wr_ref[...]) + br_ref[...]
    lane = lax.broadcasted_iota(jnp.int32, logits.shape, 1)
    lane_f = lane.astype(F32)
    is_group = lane < N_GROUPS
    gl = jnp.where(is_group, logits, NEG_INF)
    gmax = jnp.max(gl, axis=-1, keepdims=True)
    gsum = jnp.sum(jnp.where(is_group, jnp.exp(gl - gmax), 0.0), axis=-1, keepdims=True)
    g_p = 1.0 / gsum
    g_idx = jnp.min(jnp.where(gl == gmax, lane_f, float(LANES)), axis=-1, keepdims=True)
    lo = N_GROUPS + EXPERTS_PER_GROUP * g_idx
    in_group = (lane_f >= lo) & (lane_f < lo + EXPERTS_PER_GROUP)
    el = jnp.where(in_group, logits, NEG_INF)
    v1 = jnp.max(el, axis=-1, keepdims=True)
    i1 = jnp.min(jnp.where(el == v1, lane_f, float(LANES)), axis=-1, keepdims=True)
    el2 = jnp.where(lane_f == i1, NEG_INF, el)
    v2 = jnp.max(el2, axis=-1, keepdims=True)
    i2 = jnp.min(jnp.where(el2 == v2, lane_f, float(LANES)), axis=-1, keepdims=True)
    t = jnp.exp(v2 - v1)
    w1 = g_p / (1.0 + t)
    w2 = g_p * t / (1.0 + t)
    ids = jnp.where(lane == 0, i1 - N_GROUPS, jnp.where(lane == 1, i2 - N_GROUPS, 0.0))
    ids_ref[...] = ids.astype(jnp.int32)
    wts_ref[...] = jnp.where(lane == 0, w1, jnp.where(lane == 1, w2, 0.0))


def _merge_call(x, ya, yb, yc, w_gates, w_branch, w_out, ln_g, ln_b, w_router, b_router, alpha, tm):
    T, D = x.shape
    row = lambda i: (i, 0)
    const2 = lambda i: (0, 0)
    return pl.pallas_call(
        functools.partial(_merge_kernel, alpha=alpha),
        grid=(T // tm,),
        in_specs=[pl.BlockSpec((tm, D), row),
                  pl.BlockSpec((tm, BRANCH_WIDTH), row),
                  pl.BlockSpec((tm, BRANCH_WIDTH), row),
                  pl.BlockSpec((tm, BRANCH_WIDTH), row),
                  pl.BlockSpec((D, N_BRANCHES * D), const2),
                  pl.BlockSpec((N_BRANCHES, BRANCH_WIDTH, D), lambda i: (0, 0, 0)),
                  pl.BlockSpec((D, D), const2),
                  pl.BlockSpec((1, D), const2),
                  pl.BlockSpec((1, D), const2),
                  pl.BlockSpec((D, LANES), const2),
                  pl.BlockSpec((1, LANES), const2)],
        out_specs=[pl.BlockSpec((tm, D), row),
                   pl.BlockSpec((tm, LANES), row),
                   pl.BlockSpec((tm, LANES), row)],
        out_shape=[jax.ShapeDtypeStruct((T, D), F32),
                   jax.ShapeDtypeStruct((T, LANES), jnp.int32),
                   jax.ShapeDtypeStruct((T, LANES), F32)],
        compiler_params=_params(("parallel",)),
        name="merge_ln1_router",
    )(x, ya, yb, yc, w_gates, w_branch, w_out, ln_g.reshape(1, D), ln_b.reshape(1, D), w_router, b_router)


def _row_copy(src, src_row, dst, dst_row, sem):
    return pltpu.make_async_copy(src.at[pl.ds(src_row, 1), :], dst.at[pl.ds(dst_row, 1), :], sem)


def _dispatch_kernel(pos_ref, pad_ref, x_ref, xs_hbm, zero_scr, sem, *, tm, npad, pad_steps):
    def issue(r, carry):
        _row_copy(x_ref, r, xs_hbm, pos_ref[2 * r], sem).start()
        _row_copy(x_ref, r, xs_hbm, pos_ref[2 * r + 1], sem).start()
        return carry

    lax.fori_loop(0, tm, issue, 0, unroll=8)

    @pl.when(pl.program_id(0) < pad_steps)
    def _():
        zero_scr[...] = jnp.zeros_like(zero_scr)

        def issue_pad(r, carry):
            _row_copy(zero_scr, r, xs_hbm, pad_ref[r], sem).start()
            return carry

        lax.fori_loop(0, npad, issue_pad, 0, unroll=8)
        pltpu.make_async_copy(zero_scr, xs_hbm.at[pl.ds(0, npad), :], sem).wait()

    for _ in range(2):
        pltpu.make_async_copy(x_ref, xs_hbm.at[pl.ds(0, tm), :], sem).wait()


def _dispatch_call(pos, pad_rows, x1, n_rows, tm):
    T, D = x1.shape
    nsteps = T // tm
    npad = max(LANES, pad_rows.shape[0] // nsteps)
    pad_steps = pad_rows.shape[0] // npad
    return pl.pallas_call(
        functools.partial(_dispatch_kernel, tm=tm, npad=npad, pad_steps=pad_steps),
        grid=(nsteps,),
        in_specs=[pl.BlockSpec((2 * tm,), lambda i: (i,), memory_space=pltpu.SMEM),
                  pl.BlockSpec((npad,), lambda i: (jnp.minimum(i, pad_steps - 1),), memory_space=pltpu.SMEM),
                  pl.BlockSpec((tm, D), lambda i: (i, 0))],
        out_specs=pl.BlockSpec(memory_space=pl.ANY),
        out_shape=jax.ShapeDtypeStruct((n_rows, D), F32),
        scratch_shapes=[pltpu.VMEM((npad, D), F32), pltpu.SemaphoreType.DMA(())],
        compiler_params=_params(("arbitrary",)),
        name="moe_dispatch",
    )(pos, pad_rows, x1)


def _expert_kernel(te_ref, nu_ref, xs_ref, wg_ref, wu_ref, wd_ref, o_ref, wgu_scr, wd_scr):
    t = pl.program_id(0)
    used = t < nu_ref[0]
    new_expert = (t == 0) | (te_ref[t] != te_ref[jnp.maximum(t - 1, 0)])

    @pl.when(used & new_expert)
    def _():
        wgu_scr[:, 0:D_EXPERT] = wg_ref[0, 0].astype(BF16)
        wgu_scr[:, D_EXPERT:2 * D_EXPERT] = wu_ref[0, 0].astype(BF16)
        wd_scr[...] = wd_ref[0, 0].astype(BF16)

    @pl.when(used)
    def _():
        gu = _dot(xs_ref[...].astype(BF16), wgu_scr[...])
        g = gu[:, 0:D_EXPERT]
        h = (g * jax.nn.sigmoid(g)) * gu[:, D_EXPERT:2 * D_EXPERT]
        o_ref[...] = _dot(h.astype(BF16), wd_scr[...])

    @pl.when(jnp.logical_not(used))
    def _():
        o_ref[...] = jnp.zeros_like(o_ref)


def _expert_call(tile_expert, n_used, xs, n_tiles, w_gate, w_up, w_down, layer):
    D = xs.shape[1]
    tm = EXPERT_TILE
    used = lambda t, te, nu: (jnp.minimum(t, nu[0] - 1), 0)
    expert = lambda t, te, nu: (layer, te[t], 0, 0)
    grid_spec = pltpu.PrefetchScalarGridSpec(
        num_scalar_prefetch=2,
        grid=(n_tiles,),
        in_specs=[pl.BlockSpec((tm, D), used),
                  pl.BlockSpec((1, 1, D, D_EXPERT), expert),
                  pl.BlockSpec((1, 1, D, D_EXPERT), expert),
                  pl.BlockSpec((1, 1, D_EXPERT, D), expert)],
        out_specs=pl.BlockSpec((tm, D), lambda t, te, nu: (t, 0)),
        scratch_shapes=[pltpu.VMEM((D, 2 * D_EXPERT), BF16), pltpu.VMEM((D_EXPERT, D), BF16)],
    )
    return pl.pallas_call(
        _expert_kernel,
        grid_spec=grid_spec,
        out_shape=jax.ShapeDtypeStruct((n_tiles * tm, D), F32),
        compiler_params=_params(("arbitrary",)),
        name="experts",
    )(tile_expert, n_used, xs, w_gate, w_up, w_down)


def _combine_kernel(pos_ref, x_ref, w_ref, g_ref, b_ref, y_hbm, o_ref, ybuf, sem, *, alpha, tm):
    def issue(r, carry):
        _row_copy(y_hbm, pos_ref[2 * r], ybuf.at[0], r, sem).start()
        _row_copy(y_hbm, pos_ref[2 * r + 1], ybuf.at[1], r, sem).start()
        return carry

    lax.fori_loop(0, tm, issue, 0, unroll=8)
    for k in range(2):
        pltpu.make_async_copy(y_hbm.at[pl.ds(0, tm), :], ybuf.at[k], sem).wait()
    w = w_ref[...]
    u = alpha * x_ref[...] + w[:, 0:1] * ybuf[0] + w[:, 1:2] * ybuf[1]
    o_ref[...] = _layer_norm(u, g_ref[...], b_ref[...])


def _combine_call(pos, x1, wts, ln_g, ln_b, y, alpha, tm):
    T, D = x1.shape
    row = lambda i: (i, 0)
    const2 = lambda i: (0, 0)
    return pl.pallas_call(
        functools.partial(_combine_kernel, alpha=alpha, tm=tm),
        grid=(T // tm,),
        in_specs=[pl.BlockSpec((2 * tm,), lambda i: (i,), memory_space=pltpu.SMEM),
                  pl.BlockSpec((tm, D), row),
                  pl.BlockSpec((tm, LANES), row),
                  pl.BlockSpec((1, D), const2), pl.BlockSpec((1, D), const2),
                  pl.BlockSpec(memory_space=pl.ANY)],
        out_specs=pl.BlockSpec((tm, D), row),
        out_shape=jax.ShapeDtypeStruct((T, D), F32),
        scratch_shapes=[pltpu.VMEM((2, tm, D), F32), pltpu.SemaphoreType.DMA(())],
        compiler_params=_params(("arbitrary",)),
        name="combine_ln2",
    )(pos, x1, wts, ln_g.reshape(1, D), ln_b.reshape(1, D), y)


def _rope_tables(positions):
    half = ROPE_DIM // 2
    inv_freq = ROPE_THETA ** (-jnp.arange(0, ROPE_DIM, 2, dtype=F32) / ROPE_DIM)
    ang = positions.astype(F32).reshape(-1, 1) * inv_freq[None, :]
    cos, sin = jnp.cos(ang), jnp.sin(ang)
    lane = jnp.arange(LANES)
    in_head = lane % DIFF_DH
    freq = in_head % half
    first = in_head < half
    second = (in_head >= half) & (in_head < ROPE_DIM)
    cosf = jnp.where((first | second)[None, :], cos[:, freq], 1.0)
    sin_a = jnp.where(second[None, :], sin[:, freq], 0.0)
    sin_b = jnp.where(first[None, :], -sin[:, freq], 0.0)
    return cosf, sin_a, sin_b


def _dispatch_plan(ids, T):
    tm = EXPERT_TILE
    flat = ids.reshape(-1)
    onehot = (flat[:, None] == jnp.arange(N_EXPERTS, dtype=jnp.int32)[None, :]).astype(jnp.int32)
    csum = jnp.cumsum(onehot, axis=0)
    counts = csum[-1]
    rank = jnp.take_along_axis(csum, flat[:, None], axis=1)[:, 0] - 1
    padded = ((counts + tm - 1) // tm) * tm
    ends = jnp.cumsum(padded)
    starts = ends - padded
    pos = (starts[flat] + rank).astype(jnp.int32)
    n_slab = 2 * T + N_EXPERTS * tm
    n_tiles = n_slab // tm
    tile_start = jnp.arange(n_tiles, dtype=jnp.int32) * tm
    tile_expert = jnp.minimum(jnp.sum((tile_start[:, None] >= ends[None, :]).astype(jnp.int32), axis=1),
                              N_EXPERTS - 1).astype(jnp.int32)
    n_used = (ends[-1] // tm).astype(jnp.int32).reshape(1)
    gap = padded - counts
    gap_end = jnp.cumsum(gap)
    gap_start = gap_end - gap
    j = jnp.arange(N_EXPERTS * tm, dtype=jnp.int32)
    e = jnp.minimum(jnp.sum((j[:, None] >= gap_end[None, :]).astype(jnp.int32), axis=1), N_EXPERTS - 1)
    in_tile = starts[e] + counts[e] + (j - gap_start[e])
    pad_rows = jnp.where(j < gap_end[-1], in_tile, ends[-1] + (j - gap_end[-1]))
    return pos, tile_expert, n_used, pad_rows.astype(jnp.int32), n_tiles


def _layer(x, cosf, sina, sinb, B, S, lb, p, alpha, lam_init, pre_ln=None):
    T = B * S
    tm = min(ROW_TILE, T)
    w_in = p["w_in"]
    w_main = jnp.concatenate([w_in[:, :OFF_FF], w_in[:, OFF_DQ:OFF_GATES]], axis=1).astype(BF16)
    w_ff = jnp.zeros((D_MODEL, LANES), BF16).at[:, :FOX_HEADS].set(w_in[:, OFF_FF:OFF_DQ].astype(BF16))
    w_gates = w_in[:, OFF_GATES:].astype(BF16)

    if pre_ln is None:
        proj, ff = _in_proj_call(x, w_main, w_ff, cosf, sina, sinb, tm)
    else:
        proj, ff, x = _in_proj_call(x, w_main, w_ff, cosf, sina, sinb, tm, pre_ln)
    ya = _hgrn_call(proj, lb, p["hgrn_norm_g"], B, S)
    qa, ka = _fox_gate_call(ff, p["fox_f_bias"], B, S)
    yb = _fox_call(proj, qa, ka, B, S)
    lv = p["diff_lambda"].astype(F32)
    lam = jnp.exp(jnp.sum(lv[0] * lv[1])) - jnp.exp(jnp.sum(lv[2] * lv[3])) + lam_init
    lam_row = jnp.full((1, LANES), lam, F32)
    norm_row = (p["diff_norm_g"].astype(F32) * (1.0 - lam_init)).reshape(1, DIFF_DV)
    yc = _diff_call(proj, lam_row, norm_row, B, S)

    w_router = jnp.zeros((D_MODEL, LANES), F32)
    w_router = w_router.at[:, :N_GROUPS].set(p["router_g_w"]).at[:, N_GROUPS:N_GROUPS + N_EXPERTS].set(p["router_e_w"])
    b_router = jnp.zeros((1, LANES), F32)
    b_router = b_router.at[0, :N_GROUPS].set(p["router_g_b"]).at[0, N_GROUPS:N_GROUPS + N_EXPERTS].set(
        p["router_e_b"].reshape(-1))
    x1, ids, wts = _merge_call(x, ya, yb, yc, w_gates, p["w_branch"].astype(BF16), p["w_out"].astype(BF16),
                               p["ln1_g"], p["ln1_b"], w_router.astype(BF16), b_router, alpha, tm)

    pos, tile_expert, n_used, pad_rows, n_tiles = _dispatch_plan(ids[:, :2], T)
    tp = min(PERMUTE_TILE, T)
    xs = _dispatch_call(pos, pad_rows, x1, n_tiles * EXPERT_TILE, tp)
    y = _expert_call(tile_expert, n_used, xs, n_tiles, p["expert_w_gate"], p["expert_w_up"], p["expert_w_down"],
                     p["layer"])
    return _combine_call(pos, x1, wts, p["ln2_g"], p["ln2_b"], y, alpha, tp)


def kernel(x, positions, ln_in_g, ln_in_b, w_in, hgrn_lb_logits, hgrn_norm_g, fox_f_bias, diff_lambda,
           diff_norm_g, w_branch, w_out, ln1_g, ln1_b, router_g_w, router_g_b, router_e_w, router_e_b,
           expert_w_gate, expert_w_up, expert_w_down, ln2_g, ln2_b):
    B, S, D = x.shape
    T = B * S
    depth = w_in.shape[0]
    alpha = (2 * depth) ** 0.25
    cosf, sina, sinb = _rope_tables(positions)
    lb_soft = jax.nn.softmax(hgrn_lb_logits.astype(F32), axis=0)
    lower_bounds = jnp.maximum(jnp.cumsum(lb_soft, axis=0) - lb_soft[0], 0.0)

    h = x.reshape(T, D)
    for l in range(depth):
        p = dict(w_in=w_in[l], hgrn_norm_g=hgrn_norm_g[l], fox_f_bias=fox_f_bias[l], diff_lambda=diff_lambda[l],
                 diff_norm_g=diff_norm_g[l], w_branch=w_branch[l], w_out=w_out[l], ln1_g=ln1_g[l], ln1_b=ln1_b[l],
                 router_g_w=router_g_w[l], router_g_b=router_g_b[l], router_e_w=router_e_w[l],
                 router_e_b=router_e_b[l], expert_w_gate=expert_w_gate, expert_w_up=expert_w_up,
                 expert_w_down=expert_w_down, layer=l, ln2_g=ln2_g[l], ln2_b=ln2_b[l])
        lam_init = 0.8 - 0.6 * float(math.exp(-0.3 * l))
        h = _layer(h, cosf, sina, sinb, B, S, lower_bounds[l], p, alpha, lam_init,
                   pre_ln=(ln_in_g, ln_in_b) if l == 0 else None)
    return h.reshape(B, S, D)
```
